```python
import math
import jax
import jax.numpy as jnp
from jax import lax
import numpy as np

D_MODEL = 1024
BATCH = 32
SEQ = 256
DEPTH = 4
DEC_BATCH = 4
DEC_SEQ = 4096
PAST_LEN = 256

GRID_W = 64
D_MIX = D_MODEL
BRANCH = D_MIX // 4
S5_GROUP = 16
S5_GROUPS = BRANCH // S5_GROUP
S5_STATE = 64
SGU_CHUNK = 128
SGU_HEADS = 4
SGU_HEAD_DIM = BRANCH // SGU_HEADS
HY_ORDER = 2
HY_EMB = 33
HY_BANDS = (HY_EMB - 1) // 2
HY_HIDDEN = 64
HY_DECAY_TARGET = 1e-2
HY_FAST_PCT = 0.3
HY_SLOW_PCT = 1.5
DA_HEADS = 4
DA_HEAD_DIM = BRANCH // (2 * DA_HEADS)
DA_VALUE_DIM = 2 * DA_HEAD_DIM
ROPE_AXIS_DIM = DA_HEAD_DIM // 2
ROPE_BASE = 10000.0
Q_BLOCK = 128
D_FF_DENSE = 2816
N_EXPERTS = 8
TOP_K = 2
D_FF_EXPERT = 3584
N_DENSE_LAYERS = (DEPTH + 1) // 2
N_MOE_LAYERS = DEPTH // 2
PROJ_DIM = 9 * BRANCH
PROJ_SPLITS = (BRANCH, 3 * BRANCH, 6 * BRANCH, 7 * BRANCH, 8 * BRANCH)
EPS = 1e-6

kernel_name = 'hybrid_diffusion_prefix_step'


def _f(a):
    return a.astype(jnp.float32)


def rmsnorm(x, g):
    xf = _f(x)
    y = xf * lax.rsqrt(jnp.mean(xf * xf, axis=-1, keepdims=True) + EPS)
    return (y * _f(g)).astype(x.dtype)


def _complex_affine_combine(e1, e2):
    a1r, a1i, b1r, b1i = e1
    a2r, a2i, b2r, b2i = e2
    return (a2r * a1r - a2i * a1i, a2r * a1i + a2i * a1r,
            a2r * b1r - a2i * b1i + b2r, a2r * b1i + a2i * b1r + b2i)


def s5_scan(u, lam_re, lam_im, log_dt, b_re, b_im, h0, reverse):
    dt = jnp.exp(log_dt)[:, None]
    mag = jnp.exp(lam_re * dt)
    lb_re = mag * jnp.cos(lam_im * dt)
    lb_im = mag * jnp.sin(lam_im * dt)
    den = lam_re * lam_re + lam_im * lam_im
    nr = lb_re - 1.0
    coef_re = (nr * lam_re + lb_im * lam_im) / den
    coef_im = (lb_im * lam_re - nr * lam_im) / den
    bu_re = jnp.einsum('gnp,blgp->blgn', b_re, u)
    bu_im = jnp.einsum('gnp,blgp->blgn', b_im, u)
    x_re = coef_re * bu_re - coef_im * bu_im
    x_im = coef_re * bu_im + coef_im * bu_re
    if h0 is not None:
        first = -1 if reverse else 0
        h0_re, h0_im = h0[..., 0], h0[..., 1]
        x_re = x_re.at[:, first].add(lb_re * h0_re - lb_im * h0_im)
        x_im = x_im.at[:, first].add(lb_re * h0_im + lb_im * h0_re)
    a_re = jnp.broadcast_to(lb_re, x_re.shape)
    a_im = jnp.broadcast_to(lb_im, x_im.shape)
    _, _, h_re, h_im = lax.associative_scan(
        _complex_affine_combine, (a_re, a_im, x_re, x_im), reverse=reverse, axis=1)
    return h_re, h_im


def s5_mixer(u, lam_re, lam_im, log_dt, b_re, b_im, c_re, c_im, d_skip, w_glu, b_glu, h0):
    Bsz, L, _ = u.shape
    uf = _f(u)
    ug = uf.reshape(Bsz, L, S5_GROUPS, S5_GROUP)
    y = _f(d_skip) * uf
    finals = []
    for d in range(2):
        rev = d == 1
        init = None if h0 is None else _f(h0[:, d])
        h_re, h_im = s5_scan(ug, _f(lam_re[d]), _f(lam_im[d]), _f(log_dt[d]),
                             _f(b_re[d]), _f(b_im[d]), init, rev)
        y_d = (jnp.einsum('gpn,blgn->blgp', _f(c_re[d]), h_re)
               - jnp.einsum('gpn,blgn->blgp', _f(c_im[d]), h_im))
        y = y + y_d.reshape(Bsz, L, BRANCH)
        if h0 is None:
            end = 0 if rev else L - 1
            finals.append(jnp.stack([h_re[:, end], h_im[:, end]], axis=-1))
    y = jax.nn.gelu(y)
    y = y * jax.nn.sigmoid(y @ _f(w_glu) + _f(b_glu))
    final = jnp.stack(finals, axis=1) if h0 is None else None
    return y.astype(u.dtype), final


def sgu_mixer(uv, norm_g, w_s, b_s):
    Bsz, L, _ = uv.shape
    u, v = jnp.split(jax.nn.gelu(uv), 2, axis=-1)
    v = rmsnorm(v, norm_g).reshape(Bsz, L // SGU_CHUNK, SGU_CHUNK, SGU_HEADS, SGU_HEAD_DIM)
    z = jnp.einsum('hqk,bnkhd->bnqhd', w_s, v) + b_s.T[None, None, :, :, None]
    return u * z.reshape(Bsz, L, BRANCH)


def short_conv_centred(x, w, b):
    L = x.shape[1]
    xp = jnp.pad(x, ((0, 0), (1, 1), (0, 0)))
    return xp[:, :L] * w[0] + xp[:, 1:L + 1] * w[1] + xp[:, 2:] * w[2] + b


def hyena_filters(L, w1, b1, w2, b2, w3):
    pos = jnp.arange(L, dtype=jnp.float32)
    t01 = jnp.linspace(0.0, 1.0, L, dtype=jnp.float32)[:, None]
    w = (2.0 * math.pi / L) * pos[:, None]
    bands = jnp.linspace(1e-4, HY_BANDS - 1, HY_BANDS, dtype=jnp.float32)[None, :]
    feats = jnp.concatenate([t01, jnp.cos(bands * w), -jnp.sin(bands * w)], axis=-1)
    h = jnp.sin(feats @ _f(w1) + _f(b1))
    h = jnp.sin(h @ _f(w2) + _f(b2))
    h = h @ _f(w3)
    dist = jnp.abs(pos - (L // 2)) / L
    decay = jnp.abs(jnp.linspace(math.log(HY_DECAY_TARGET) / HY_SLOW_PCT,
                                 math.log(HY_DECAY_TARGET) / HY_FAST_PCT,
                                 HY_ORDER * BRANCH, dtype=jnp.float32))
    return (h * jnp.exp(-dist[:, None] * decay[None, :])).reshape(L, HY_ORDER, BRANCH)


def fft_conv_centred(u, filt, bias):
    L = u.shape[1]
    n = 2 * L
    uf = jnp.fft.rfft(u, n=n, axis=1)
    hf = jnp.fft.rfft(filt, n=n, axis=0)
    y = jnp.fft.irfft(uf * hf[None], n=n, axis=1)[:, L // 2:L // 2 + L]
    return y + u * bias


def hyena_mixer(z, conv_w, conv_b, w1, b1, w2, b2, w3, bias):
    L = z.shape[1]
    zc = _f(short_conv_centred(z, conv_w, conv_b))
    v, x1, x2 = jnp.split(zc, 3, axis=-1)
    filt = hyena_filters(L, w1, b1, w2, b2, w3)
    bias = _f(bias)
    y = x1 * fft_conv_centred(v, filt[:, 0], bias[0])
    y = x2 * fft_conv_centred(y, filt[:, 1], bias[1])
    return y.astype(z.dtype)


def axial_rope(L):
    n_rows = L // GRID_W
    row = jnp.repeat(jnp.arange(n_rows, dtype=jnp.float32), GRID_W)
    col = (jnp.arange(L) % GRID_W).astype(jnp.float32)
    inv = ROPE_BASE ** (-jnp.arange(0, ROPE_AXIS_DIM, 2, dtype=jnp.float32) / ROPE_AXIS_DIM)
    ar = row[:, None] * inv[None, :]
    ac = col[:, None] * inv[None, :]
    ang = jnp.concatenate([ar, ar, ac, ac], axis=-1)
    return jnp.cos(ang), jnp.sin(ang)


def apply_axial_rope(x, cos, sin):
    r1, r2, c1, c2 = jnp.split(x, 4, axis=-1)
    rot = jnp.concatenate([-r2, r1, -c2, c1], axis=-1)
    cs = cos[None, :, None, None, :]
    sn = sin[None, :, None, None, :]
    return (x * cs + rot * sn).astype(x.dtype)


def diff_attention(q, k, v, lam):
    Bsz, Lq = q.shape[0], q.shape[1]
    nblk = Lq // Q_BLOCK
    qb = jnp.moveaxis(q.reshape(Bsz, nblk, Q_BLOCK, DA_HEADS, 2, DA_HEAD_DIM), 1, 0)
    scale = DA_HEAD_DIM ** -0.5

    def one_block(qi):
        s = _f(jnp.einsum('bqhmd,bkhmd->bhmqk', qi, k)) * scale
        p = jax.nn.softmax(s, axis=-1)
        pd = p[:, :, 0] - lam * p[:, :, 1]
        return jnp.einsum('bhqk,bkhe->bqhe', pd.astype(v.dtype), v)

    o = lax.map(one_block, qb)
    return jnp.moveaxis(o, 0, 1).reshape(Bsz, Lq, DA_HEADS, DA_VALUE_DIM)


def diff_attn_mixer(q, k, v, lq1, lk1, lq2, lk2, subln, l, ctx_kv):
    Bsz, L, _ = q.shape
    q = q.reshape(Bsz, L, DA_HEADS, 2, DA_HEAD_DIM)
    k = k.reshape(Bsz, L, DA_HEADS, 2, DA_HEAD_DIM)
    v = v.reshape(Bsz, L, DA_HEADS, DA_VALUE_DIM)
    lam_init = 0.8 - 0.6 * math.exp(-0.3 * l)
    lam = (jnp.exp(jnp.sum(_f(lq1) * _f(lk1))) - jnp.exp(jnp.sum(_f(lq2) * _f(lk2))) + lam_init)
    if ctx_kv is None:
        k_all, v_all = k, v
        k_store = k.reshape(Bsz, L, DA_HEADS, 2 * DA_HEAD_DIM)
        v_store = v
    else:
        k_ctx, v_ctx = ctx_kv
        cos, sin = axial_rope(L)
        q = apply_axial_rope(q, cos, sin)
        k = apply_axial_rope(k, cos, sin)
        Lc = k_ctx.shape[1]
        k_all = jnp.concatenate(
            [k_ctx.reshape(Bsz, Lc, DA_HEADS, 2, DA_HEAD_DIM).astype(k.dtype), k], axis=1)
        v_all = jnp.concatenate([v_ctx.astype(v.dtype), v], axis=1)
        k_store, v_store = None, None
    o = diff_attention(q, k_all, v_all, lam)
    o = rmsnorm(o, subln) * (1.0 - lam_init)
    return o.reshape(Bsz, L, BRANCH), k_store, v_store


def swiglu(h, w1, w3, w2):
    return (jax.nn.silu(h @ w1) * (h @ w3)) @ w2


def moe_swiglu(h, router, w1, w3, w2):
    Bsz, L, D = h.shape
    t = h.reshape(Bsz * L, D)
    probs = jax.nn.softmax(_f(t @ router), axis=-1)
    top_v, top_i = lax.top_k(probs, TOP_K)
    top_v = top_v / jnp.sum(top_v, axis=-1, keepdims=True)
    combine = jnp.sum(jax.nn.one_hot(top_i, N_EXPERTS, dtype=jnp.float32) * top_v[..., None], axis=1)
    combine = combine.astype(t.dtype)
    out = jnp.zeros_like(t)
    for e in range(N_EXPERTS):
        out = out + combine[:, e:e + 1] * swiglu(t, w1[e], w3[e], w2[e])
    return out.reshape(Bsz, L, D)


def trunk_layer(x, cond, p, l, ctx):
    is_ctx = ctx is None
    mod = jax.nn.silu(cond) @ p['w_ada'][l] + p['b_ada'][l]
    sh1, sc1, g1, sh2, sc2, g2 = jnp.split(mod[:, None, :], 6, axis=-1)
    h = rmsnorm(x, p['norm_mix'][l]) * (1.0 + sc1) + sh1
    proj = h @ p['w_in'][l]
    u_s5, uv_sgu, z_hy, q, k, v = jnp.split(proj, PROJ_SPLITS, axis=-1)

    y_s5, ssm_final = s5_mixer(u_s5, p['s5_lam_re'][l], p['s5_lam_im'][l], p['s5_log_dt'][l],
                               p['s5_b_re'][l], p['s5_b_im'][l], p['s5_c_re'][l], p['s5_c_im'][l],
                               p['s5_d'][l], p['s5_w_glu'][l], p['s5_b_glu'][l],
                               None if is_ctx else ctx[2])
    y_sgu = sgu_mixer(uv_sgu, p['sgu_norm'][l], p['sgu_w_s'][l], p['sgu_b_s'][l])
    y_hy = hyena_mixer(z_hy, p['hy_conv_w'][l], p['hy_conv_b'][l], p['hy_w1'][l], p['hy_b1'][l],
                       p['hy_w2'][l], p['hy_b2'][l], p['hy_w3'][l], p['hy_bias'][l])
    y_da, k_store, v_store = diff_attn_mixer(q, k, v, p['da_lq1'][l], p['da_lk1'][l],
                                             p['da_lq2'][l], p['da_lk2'][l], p['da_subln'][l], l,
                                             None if is_ctx else (ctx[0], ctx[1]))
    bn = p['branch_norm'][l]
    mixed = jnp.concatenate([rmsnorm(y_s5, bn[0]), rmsnorm(y_sgu, bn[1]),
                             rmsnorm(y_hy, bn[2]), y_da], axis=-1)
    x = x + g1 * (mixed @ p['w_out'][l])

    h2 = rmsnorm(x, p['norm_ffn'][l]) * (1.0 + sc2) + sh2
    j = l // 2
    if l % 2 == 0:
        f = swiglu(h2, p['ffn_w1'][j], p['ffn_w3'][j], p['ffn_w2'][j])
    else:
        f = moe_swiglu(h2, p['moe_router'][j], p['moe_w1'][j], p['moe_w3'][j], p['moe_w2'][j])
    x = x + g2 * f
    if is_ctx:
        return x, k_store, v_store, ssm_final
    return x


def setup_inputs(seed: int = 0) -> dict:
    key = jax.random.key(seed)
    ks = iter(jax.random.split(key, 80))

    def nrm(shape, scale=1.0):
        return jax.random.normal(next(ks), shape, jnp.float32) * scale

    G, N, P = S5_GROUPS, S5_STATE, S5_GROUP
    D = D_MODEL
    return {
        'x_prompt': nrm((BATCH, SEQ, D)),
        'x_sample': nrm((DEC_BATCH, DEC_SEQ, D)),
        'cache_k': nrm((DEC_BATCH, DEPTH, PAST_LEN, DA_HEADS, 2 * DA_HEAD_DIM)),
        'cache_v': nrm((DEC_BATCH, DEPTH, PAST_LEN, DA_HEADS, DA_VALUE_DIM)),
        'state_ssm': nrm((DEC_BATCH, DEPTH, 2, G, N, 2)),
        'c': nrm((DEC_BATCH, D)),
        'c_ctx': nrm((D,)),
        'w_ada': nrm((DEPTH, D, 6 * D), 0.5 * D ** -0.5),
        'b_ada': nrm((DEPTH, 6 * D), 0.01),
        'norm_mix': 1.0 + nrm((DEPTH, D), 0.02),
        'norm_ffn': 1.0 + nrm((DEPTH, D), 0.02),
        'w_in': nrm((DEPTH, D, PROJ_DIM), D ** -0.5),
        'w_out': nrm((DEPTH, D_MIX, D), D_MIX ** -0.5),
        'branch_norm': 1.0 + nrm((DEPTH, 3, BRANCH), 0.02),
        's5_lam_re': -0.5 + nrm((DEPTH, 2, G, N), 0.01),
        's5_lam_im': math.pi * jnp.arange(N, dtype=jnp.float32) + nrm((DEPTH, 2, G, N), 0.01),
        's5_log_dt': jax.random.uniform(next(ks), (DEPTH, 2, G), jnp.float32,
                                        math.log(1e-3), math.log(1e-1)),
        's5_b_re': nrm((DEPTH, 2, G, N, P), (2.0 * P) ** -0.5),
        's5_b_im': nrm((DEPTH, 2, G, N, P), (2.0 * P) ** -0.5),
        's5_c_re': nrm((DEPTH, 2, G, P, N), (2.0 * N) ** -0.5),
        's5_c_im': nrm((DEPTH, 2, G, P, N), (2.0 * N) ** -0.5),
        's5_d': nrm((DEPTH, BRANCH)),
        's5_w_glu': nrm((DEPTH, BRANCH, BRANCH), BRANCH ** -0.5),
        's5_b_glu': nrm((DEPTH, BRANCH), 0.01),
        'sgu_norm': 1.0 + nrm((DEPTH, BRANCH), 0.02),
        'sgu_w_s': nrm((DEPTH, SGU_HEADS, SGU_CHUNK, SGU_CHUNK), SGU_CHUNK ** -0.5),
        'sgu_b_s': 1.0 + nrm((DEPTH, SGU_HEADS, SGU_CHUNK), 0.01),
        'hy_conv_w': nrm((DEPTH, 3, 3 * BRANCH), 3 ** -0.5),
        'hy_conv_b': nrm((DEPTH, 3 * BRANCH), 0.01),
        'hy_w1': nrm((DEPTH, HY_EMB, HY_HIDDEN), HY_EMB ** -0.5),
        'hy_b1': nrm((DEPTH, HY_HIDDEN), 0.01),
        'hy_w2': nrm((DEPTH, HY_HIDDEN, HY_HIDDEN), HY_HIDDEN ** -0.5),
        'hy_b2': nrm((DEPTH, HY_HIDDEN), 0.01),
        'hy_w3': nrm((DEPTH, HY_HIDDEN, HY_ORDER * BRANCH), HY_HIDDEN ** -0.5),
        'hy_bias': nrm((DEPTH, HY_ORDER, BRANCH)),
        'da_lq1': nrm((DEPTH, DA_HEAD_DIM), 0.1),
        'da_lk1': nrm((DEPTH, DA_HEAD_DIM), 0.1),
        'da_lq2': nrm((DEPTH, DA_HEAD_DIM), 0.1),
        'da_lk2': nrm((DEPTH, DA_HEAD_DIM), 0.1),
        'da_subln': 1.0 + nrm((DEPTH, DA_VALUE_DIM), 0.02),
        'ffn_w1': nrm((N_DENSE_LAYERS, D, D_FF_DENSE), D ** -0.5),
        'ffn_w3': nrm((N_DENSE_LAYERS, D, D_FF_DENSE), D ** -0.5),
        'ffn_w2': nrm((N_DENSE_LAYERS, D_FF_DENSE, D), D_FF_DENSE ** -0.5),
        'moe_router': nrm((N_MOE_LAYERS, D, N_EXPERTS), D ** -0.5),
        'moe_w1': nrm((N_MOE_LAYERS, N_EXPERTS, D, D_FF_EXPERT), D ** -0.5),
        'moe_w3': nrm((N_MOE_LAYERS, N_EXPERTS, D, D_FF_EXPERT), D ** -0.5),
        'moe_w2': nrm((N_MOE_LAYERS, N_EXPERTS, D_FF_EXPERT, D), D_FF_EXPERT ** -0.5),
        'norm_final': 1.0 + nrm((D,), 0.02),
    }


def reference(x_prompt, x_sample, cache_k, cache_v, state_ssm, c, c_ctx, w_ada, b_ada,
              norm_mix, norm_ffn, w_in, w_out, branch_norm, s5_lam_re, s5_lam_im, s5_log_dt,
              s5_b_re, s5_b_im, s5_c_re, s5_c_im, s5_d, s5_w_glu, s5_b_glu, sgu_norm, sgu_w_s,
              sgu_b_s, hy_conv_w, hy_conv_b, hy_w1, hy_b1, hy_w2, hy_b2, hy_w3, hy_bias,
              da_lq1, da_lk1, da_lq2, da_lk2, da_subln, ffn_w1, ffn_w3, ffn_w2, moe_router,
              moe_w1, moe_w3, moe_w2, norm_final):
    p = dict(w_ada=w_ada, b_ada=b_ada, norm_mix=norm_mix, norm_ffn=norm_ffn, w_in=w_in,
             w_out=w_out, branch_norm=branch_norm, s5_lam_re=s5_lam_re, s5_lam_im=s5_lam_im,
             s5_log_dt=s5_log_dt, s5_b_re=s5_b_re, s5_b_im=s5_b_im, s5_c_re=s5_c_re,
             s5_c_im=s5_c_im, s5_d=s5_d, s5_w_glu=s5_w_glu, s5_b_glu=s5_b_glu,
             sgu_norm=sgu_norm, sgu_w_s=sgu_w_s, sgu_b_s=sgu_b_s, hy_conv_w=hy_conv_w,
             hy_conv_b=hy_conv_b, hy_w1=hy_w1, hy_b1=hy_b1, hy_w2=hy_w2, hy_b2=hy_b2,
             hy_w3=hy_w3, hy_bias=hy_bias, da_lq1=da_lq1, da_lk1=da_lk1, da_lq2=da_lq2,
             da_lk2=da_lk2, da_subln=da_subln, ffn_w1=ffn_w1, ffn_w3=ffn_w3, ffn_w2=ffn_w2,
             moe_router=moe_router, moe_w1=moe_w1, moe_w3=moe_w3, moe_w2=moe_w2)

    xc = x_prompt
    ks, vs, ss = [], [], []
    for l in range(DEPTH):
        xc, k_l, v_l, s_l = trunk_layer(xc, c_ctx[None, :], p, l, None)
        ks.append(k_l)
        vs.append(v_l)
        ss.append(s_l)
    y_prompt = rmsnorm(xc, norm_final)
    new_cache_k = jnp.stack(ks, axis=1)
    new_cache_v = jnp.stack(vs, axis=1)
    new_state_ssm = jnp.stack(ss, axis=1)

    xs = x_sample
    for l in range(DEPTH):
        xs = trunk_layer(xs, c, p, l, (cache_k[:, l], cache_v[:, l], state_ssm[:, l]))
    y_sample = rmsnorm(xs, norm_final)
    return (y_prompt, y_sample, new_cache_k, new_cache_v, new_state_ssm)
```

```python
import functools
import math

import jax
import jax.numpy as jnp
import numpy as np
from jax import lax
from jax.experimental import pallas as pl
from jax.experimental.pallas import tpu as pltpu

F32 = jnp.float32
BF16 = jnp.bfloat16

D = 1024
BR = 256
PROJ = 9 * BR
S5_G, S5_N, S5_P = 16, 64, 16
S5_STATES = S5_G * S5_N
SGU_CHUNK, SGU_HEADS = 128, 4
HEADS, HEAD_DIM, VAL_DIM = 4, 32, 64
GRID_WIDTH = 64
ROPE_DIM = HEAD_DIM // 2
ROPE_BASE = 10000.0
HY_EMB, HY_BANDS = 33, 16
HY_DECAY_TARGET, HY_FAST, HY_SLOW = 1e-2, 0.3, 1.5
N_EXPERTS = 8
EPS = 1e-6
LOG2E = 1.4426950408889634

DFT_N2 = 128
DFT_ROWS = 32
VMEM_LIMIT = 56 * 1024 * 1024


def _cparams(*sem):
    return pltpu.CompilerParams(dimension_semantics=sem, vmem_limit_bytes=VMEM_LIMIT)


def _dot(a, b):
    return jnp.dot(a, b, preferred_element_type=F32)


def _dot3(a, b):
    a_hi = a.astype(BF16)
    b_hi = b.astype(BF16)
    a_lo = (a - a_hi.astype(F32)).astype(BF16)
    b_lo = (b - b_hi.astype(F32)).astype(BF16)
    return _dot(a_hi, b_hi) + (_dot(a_hi, b_lo) + _dot(a_lo, b_hi))


def _rms(x, g):
    return x * lax.rsqrt(jnp.mean(x * x, axis=-1, keepdims=True) + EPS) * g


def _mod_spec(modg, tm, L):
    nmod = modg.shape[0]
    return pl.BlockSpec((None, 1, 6 * D), lambda i, *_: ((i * tm // L) % nmod, 0, 0))


def _ada_kernel(c_ref, w_ref, b_ref, o_ref):
    c = c_ref[...]
    s = (c * jax.nn.sigmoid(c)).astype(BF16)
    o_ref[0] = _dot(s, w_ref[0].astype(BF16)) + b_ref[0]


def _ada(cond8, w_ada, b_ada):
    depth = w_ada.shape[0]
    tn = 1536
    return pl.pallas_call(
        _ada_kernel,
        grid=(depth, 6 * D // tn),
        in_specs=[pl.BlockSpec((8, D), lambda l, j: (0, 0)),
                  pl.BlockSpec((1, D, tn), lambda l, j: (l, 0, j)),
                  pl.BlockSpec((1, 1, tn), lambda l, j: (l, 0, j))],
        out_specs=pl.BlockSpec((1, 8, tn), lambda l, j: (l, 0, j)),
        out_shape=jax.ShapeDtypeStruct((depth, 8, 6 * D), F32),
        compiler_params=_cparams("parallel", "parallel"),
        name="ada",
    )(cond8, w_ada, b_ada.reshape(depth, 1, 6 * D))


def _inproj_kernel(*refs, rope):
    if rope:
        (x_ref, mod_ref, g_ref, w_ref, cos_ref, sin_ref,
         u_ref, uv_ref, z_ref, q_ref, k_ref, v_ref) = refs
    else:
        x_ref, mod_ref, g_ref, w_ref, u_ref, uv_ref, z_ref, q_ref, k_ref, v_ref = refs
    x = x_ref[...]
    h = _rms(x, g_ref[...]) * (1.0 + mod_ref[:, D:2 * D]) + mod_ref[:, 0:D]
    p = _dot(h.astype(BF16), w_ref[...])
    u_ref[...] = p[:, 0:BR]
    uv_ref[...] = p[:, BR:3 * BR]
    z_ref[...] = p[:, 3 * BR:6 * BR]
    q = p[:, 6 * BR:7 * BR]
    k = p[:, 7 * BR:8 * BR]
    if rope:
        cs = cos_ref[...]
        sn = sin_ref[...]
        lane = lax.broadcasted_iota(jnp.int32, q.shape, 1)
        first = (lane % (2 * (ROPE_DIM // 2))) < (ROPE_DIM // 2)
        half = ROPE_DIM // 2

        def rot(t):
            return jnp.where(first, pltpu.roll(t, BR - half, 1), pltpu.roll(t, half, 1))

        q = q * cs + rot(q) * sn
        k = k * cs + rot(k) * sn
    q_ref[...] = q
    k_ref[...] = k
    v_ref[...] = p[:, 8 * BR:9 * BR]


def _inproj(x, modg, g, w_bf, B, L, tm, rope_tabs):
    T = B * L
    nt = L // tm
    rope = rope_tabs is not None
    in_specs = [pl.BlockSpec((tm, D), lambda i: (i, 0)),
                _mod_spec(modg, tm, L),
                pl.BlockSpec((1, D), lambda i: (0, 0)),
                pl.BlockSpec((D, PROJ), lambda i: (0, 0))]
    args = [x, modg, g.reshape(1, D), w_bf]
    if rope:
        in_specs += [pl.BlockSpec((tm, BR), lambda i: (i % nt, 0))] * 2
        args += list(rope_tabs)
    tok = lambda n: pl.BlockSpec((tm, n), lambda i: (i, 0))
    out_specs = [pl.BlockSpec((tm, BR), lambda i: (i % nt, i // nt)),
                 tok(2 * BR), tok(3 * BR), tok(BR), tok(BR), tok(BR)]
    out_shape = [jax.ShapeDtypeStruct((L, B * BR), F32),
                 jax.ShapeDtypeStruct((T, 2 * BR), F32),
                 jax.ShapeDtypeStruct((T, 3 * BR), F32),
                 jax.ShapeDtypeStruct((T, BR), F32),
                 jax.ShapeDtypeStruct((T, BR), F32),
                 jax.ShapeDtypeStruct((T, BR), F32)]
    return pl.pallas_call(
        functools.partial(_inproj_kernel, rope=rope),
        grid=(T // tm,), in_specs=in_specs, out_specs=out_specs, out_shape=out_shape,
        compiler_params=_cparams("parallel"), name="inproj",
    )(*args)


def _s5_kernel(uf_ref, ub_ref, h0_ref, wb_ref, wc_ref, a_ref, yf_ref, yb_ref, hfin_ref,
               xs_ref, hc_ref, *, tt):
    i = pl.program_id(1)
    last = pl.num_programs(1) - 1
    half = S5_STATES // 2

    @pl.when(i == 0)
    def _():
        hc_ref[...] = h0_ref[...]

    for d in range(2):
        u_ref = uf_ref if d == 0 else ub_ref
        y_ref = yf_ref if d == 0 else yb_ref
        u2 = u_ref[...].reshape(tt * 8, BR).astype(BF16)
        xs_ref[...] = _dot(u2, wb_ref[d])
        for c in range(2):
            cr = slice(c * half, (c + 1) * half)
            ci = slice(S5_STATES + c * half, S5_STATES + (c + 1) * half)
            ar = jnp.broadcast_to(a_ref[d, 0:1, cr], (8, half))
            ai = jnp.broadcast_to(a_ref[d, 1:2, cr], (8, half))

            def body(s, carry, cr=cr, ci=ci, ar=ar, ai=ai, d=d):
                hr, hi = carry
                t = s if d == 0 else tt - 1 - s
                r0 = pl.multiple_of(t * 8, 8)
                nr = ar * hr - ai * hi + xs_ref[pl.ds(r0, 8), cr]
                ni = ar * hi + ai * hr + xs_ref[pl.ds(r0, 8), ci]
                xs_ref[pl.ds(r0, 8), cr] = nr
                xs_ref[pl.ds(r0, 8), ci] = ni
                return nr, ni

            hr, hi = lax.fori_loop(0, tt, body, (hc_ref[d, :, cr], hc_ref[d, :, ci]))
            hc_ref[d, :, cr] = hr
            hc_ref[d, :, ci] = hi
        y = _dot(xs_ref[...].astype(BF16), wc_ref[d])
        y_ref[...] = y.reshape(tt, 8, BR)

    @pl.when(i == last)
    def _():
        hfin_ref[...] = hc_ref[...]


def _s5(u_tm, h0, wb, wc, a, L, Bp, tt):
    nT = L // tt
    ng = Bp // 8
    blk = lambda f: pl.BlockSpec((tt, 8, BR), f)
    const = lambda shp: pl.BlockSpec(shp, lambda g, i: (0,) * len(shp))
    return pl.pallas_call(
        functools.partial(_s5_kernel, tt=tt),
        grid=(ng, nT),
        in_specs=[blk(lambda g, i: (i, g, 0)), blk(lambda g, i: (nT - 1 - i, g, 0)),
                  pl.BlockSpec((2, 8, 2 * S5_STATES), lambda g, i: (0, g, 0)),
                  const((2, BR, 2 * S5_STATES)), const((2, 2 * S5_STATES, BR)),
                  const((2, 2, S5_STATES))],
        out_specs=[blk(lambda g, i: (i, g, 0)), blk(lambda g, i: (nT - 1 - i, g, 0)),
                   pl.BlockSpec((2, 8, 2 * S5_STATES), lambda g, i: (0, g, 0))],
        out_shape=[jax.ShapeDtypeStruct((L, Bp, BR), F32),
                   jax.ShapeDtypeStruct((L, Bp, BR), F32),
                   jax.ShapeDtypeStruct((2, Bp, 2 * S5_STATES), F32)],
        scratch_shapes=[pltpu.VMEM((tt * 8, 2 * S5_STATES), F32),
                        pltpu.VMEM((2, 8, 2 * S5_STATES), F32)],
        compiler_params=_cparams("parallel", "arbitrary"), name="s5",
    )(u_tm, u_tm, h0, wb, wc, a)


def _s5_prep(lam_re, lam_im, log_dt, b_re, b_im, c_re, c_im):
    dt = jnp.exp(log_dt)[..., None]
    mag = jnp.exp(lam_re * dt)
    lb_re = mag * jnp.cos(lam_im * dt)
    lb_im = mag * jnp.sin(lam_im * dt)
    den = lam_re * lam_re + lam_im * lam_im
    nr = lb_re - 1.0
    coef_re = ((nr * lam_re + lb_im * lam_im) / den)[..., None]
    coef_im = ((lb_im * lam_re - nr * lam_im) / den)[..., None]
    bp_re = coef_re * b_re - coef_im * b_im
    bp_im = coef_re * b_im + coef_im * b_re
    eye = jnp.eye(S5_G, dtype=lam_re.dtype)

    def blockdiag_in(b):
        return jnp.einsum('dgnp,gh->dgphn', b, eye).reshape(2, BR, S5_STATES)

    def blockdiag_out(c):
        return jnp.einsum('dgpn,gh->dgnhp', c, eye).reshape(2, S5_STATES, BR)

    wb = jnp.concatenate([blockdiag_in(bp_re), blockdiag_in(bp_im)], axis=-1)
    wc = jnp.concatenate([blockdiag_out(c_re), -blockdiag_out(c_im)], axis=1)
    a = jnp.stack([lb_re.reshape(2, S5_STATES), lb_im.reshape(2, S5_STATES)], axis=1)
    return wb.astype(BF16), wc.astype(BF16), a


def _short_kernel(z_ref, zp_ref, zn_ref, w_ref, b_ref, v_ref, x1_ref, x2_ref, *, nt):
    j = pl.program_id(0) % nt
    z = z_ref[...]
    tm = z.shape[0]
    row = lax.broadcasted_iota(jnp.int32, z.shape, 0)
    prev_row = jnp.where(j > 0, zp_ref[7:8, :], 0.0)
    next_row = jnp.where(j < nt - 1, zn_ref[0:1, :], 0.0)
    zprev = jnp.where(row == 0, prev_row, pltpu.roll(z, 1, 0))
    znext = jnp.where(row == tm - 1, next_row, pltpu.roll(z, tm - 1, 0))
    y = zprev * w_ref[0:1, :] + z * w_ref[1:2, :] + znext * w_ref[2:3, :] + b_ref[...]
    v_ref[...] = y[:, 0:BR]
    x1_ref[...] = y[:, BR:2 * BR]
    x2_ref[...] = y[:, 2 * BR:3 * BR]


def _short_conv(z, w, b, L, tm):
    T = z.shape[0]
    nt = L // tm
    r8 = tm // 8
    nblk8 = T // 8
    out = jax.ShapeDtypeStruct((T, BR), F32)
    return pl.pallas_call(
        functools.partial(_short_kernel, nt=nt),
        grid=(T // tm,),
        in_specs=[pl.BlockSpec((tm, 3 * BR), lambda i: (i, 0)),
                  pl.BlockSpec((8, 3 * BR), lambda i: (jnp.maximum(i * r8 - 1, 0), 0)),
                  pl.BlockSpec((8, 3 * BR), lambda i: (jnp.minimum((i + 1) * r8, nblk8 - 1), 0)),
                  pl.BlockSpec((3, 3 * BR), lambda i: (0, 0)),
                  pl.BlockSpec((1, 3 * BR), lambda i: (0, 0))],
        out_specs=[pl.BlockSpec((tm, BR), lambda i: (i, 0))] * 3,
        out_shape=[out, out, out],
        compiler_params=_cparams("parallel"), name="short_conv",
    )(z, z, z, w, b.reshape(1, 3 * BR))


def _hy_consts(L):
    n = 2 * L
    n2 = DFT_N2
    n1 = n // n2
    r = L // n2
    s = DFT_ROWS // r

    def cs(num, den):
        th = (2.0 * math.pi / den) * (num % den).astype(F32)
        return jnp.cos(th), jnp.sin(th)

    ar = lambda m: jnp.arange(m, dtype=jnp.int32)
    eye = jnp.eye(s, dtype=F32)
    c, sn = cs(ar(n1)[:, None] * ar(r)[None, :], n1)
    f1 = jnp.stack([c, -sn], axis=1).reshape(2 * n1, r)
    m1 = jnp.kron(eye, f1)
    c, sn = cs((ar(r)[:, None] + r // 2) * ar(n1)[None, :], n1)
    g1 = jnp.stack([c, -sn], axis=2).reshape(r, 2 * n1) * (1.0 / n)
    gm = jnp.kron(eye, g1)
    kk1 = ar(n1)[:, None, None]
    kk2 = ar(n2)[None, :, None]
    jj2 = ar(n2)[None, None, :]
    c, sn = cs(jj2 * kk2 * n1 + jj2 * kk1, n)
    w2f = jnp.concatenate([jnp.concatenate([c, sn], axis=2),
                           jnp.concatenate([-sn, c], axis=2)], axis=1)
    ct, st = jnp.swapaxes(c, 1, 2), jnp.swapaxes(sn, 1, 2)
    w2i = jnp.concatenate([jnp.concatenate([ct, -st], axis=2),
                           jnp.concatenate([st, ct], axis=2)], axis=1)
    return m1.astype(BF16), gm.astype(BF16), w2f.astype(BF16), w2i.astype(BF16), n1


def _hyA_kernel(x_ref, m_ref, o_ref):
    o_ref[...] = _dot(m_ref[...], x_ref[...].astype(BF16)).astype(BF16)


def _hyB_kernel(a_ref, wf_ref, wi_ref, hf_ref, o_ref, *, rb):
    n2 = DFT_N2
    for j in range(rb):
        a = a_ref[j].reshape(2 * n2, BR)
        x = _dot(wf_ref[j], a)
        xr, xi = x[0:n2], x[n2:2 * n2]
        hr, hi = hf_ref[j, 0], hf_ref[j, 1]
        y = jnp.concatenate([xr * hr - xi * hi, xr * hi + xi * hr], axis=0).astype(BF16)
        o_ref[j] = _dot(wi_ref[j], y).astype(BF16).reshape(2, n2, BR)


def _hyC_kernel(a_ref, g_ref, v_ref, xg_ref, bias_ref, o_ref):
    y = _dot(g_ref[...], a_ref[...])
    o_ref[...] = xg_ref[...] * (y + v_ref[...] * bias_ref[...])


def _hy_conv(v, xg, bias, hf, consts, L):
    m1, gm, w2f, w2i, n1 = consts
    T = v.shape[0]
    n2 = DFT_N2
    rows = T // n2
    cols = n2 * BR
    tc = 4096
    nrb = rows // DFT_ROWS
    v2 = v.reshape(rows, cols)
    x2 = xg.reshape(rows, cols)
    a = pl.pallas_call(
        _hyA_kernel, grid=(nrb, cols // tc),
        in_specs=[pl.BlockSpec((DFT_ROWS, tc), lambda i, j: (i, j)),
                  pl.BlockSpec((4 * DFT_ROWS, DFT_ROWS), lambda i, j: (0, 0))],
        out_specs=pl.BlockSpec((4 * DFT_ROWS, tc), lambda i, j: (i, j)),
        out_shape=jax.ShapeDtypeStruct((4 * rows, cols), BF16),
        compiler_params=_cparams("parallel", "parallel"), name="hy_stage1",
    )(v2, m1)
    R = 2 * rows
    rb = min(8, n1)
    nkb = n1 // rb
    nseq = R // n1
    a5 = a.reshape(R, 2, n2, BR)
    b5 = pl.pallas_call(
        functools.partial(_hyB_kernel, rb=rb), grid=(nkb, nseq),
        in_specs=[pl.BlockSpec((rb, 2, n2, BR), lambda kb, s: (s * nkb + kb, 0, 0, 0)),
                  pl.BlockSpec((rb, 2 * n2, 2 * n2), lambda kb, s: (kb, 0, 0)),
                  pl.BlockSpec((rb, 2 * n2, 2 * n2), lambda kb, s: (kb, 0, 0)),
                  pl.BlockSpec((rb, 2, n2, BR), lambda kb, s: (kb, 0, 0, 0))],
        out_specs=pl.BlockSpec((rb, 2, n2, BR), lambda kb, s: (s * nkb + kb, 0, 0, 0)),
        out_shape=jax.ShapeDtypeStruct((R, 2, n2, BR), BF16),
        compiler_params=_cparams("parallel", "parallel"), name="hy_stage2",
    )(a5, w2f, w2i, hf)
    bt = jnp.tile(bias.reshape(1, BR), (1, tc // BR))
    y = pl.pallas_call(
        _hyC_kernel, grid=(nrb, cols // tc),
        in_specs=[pl.BlockSpec((4 * DFT_ROWS, tc), lambda i, j: (i, j)),
                  pl.BlockSpec((DFT_ROWS, 4 * DFT_ROWS), lambda i, j: (0, 0)),
                  pl.BlockSpec((DFT_ROWS, tc), lambda i, j: (i, j)),
                  pl.BlockSpec((DFT_ROWS, tc), lambda i, j: (i, j)),
                  pl.BlockSpec((1, tc), lambda i, j: (0, 0))],
        out_specs=pl.BlockSpec((DFT_ROWS, tc), lambda i, j: (i, j)),
        out_shape=jax.ShapeDtypeStruct((rows, cols), F32),
        compiler_params=_cparams("parallel", "parallel"), name="hy_stage3",
    )(b5.reshape(4 * rows, cols), gm, v2, x2, bt)
    return y.reshape(T, BR)


def _hy_filter_spectrum(L, w1, b1, w2, b2, w3, n1):
    dt = w1.dtype
    pos = jnp.arange(L, dtype=dt)
    t01 = jnp.linspace(0.0, 1.0, L, dtype=dt)[:, None]
    w = (2.0 * math.pi / L) * pos[:, None]
    bands = jnp.linspace(1e-4, HY_BANDS - 1, HY_BANDS, dtype=dt)[None, :]
    feats = jnp.concatenate([t01, jnp.cos(bands * w), -jnp.sin(bands * w)], axis=-1)
    h = jnp.sin(feats @ w1 + b1)
    h = jnp.sin(h @ w2 + b2)
    h = h @ w3
    dist = jnp.abs(pos - (L // 2)) / L
    decay = jnp.abs(jnp.linspace(math.log(HY_DECAY_TARGET) / HY_SLOW,
                                 math.log(HY_DECAY_TARGET) / HY_FAST, 2 * BR, dtype=dt))
    filt = h * jnp.exp(-dist[:, None] * decay[None, :])
    spec = jnp.fft.fft(filt, n=2 * L, axis=0)
    spec = spec.reshape(DFT_N2, n1, 2, BR).transpose(2, 1, 0, 3)
    return jnp.stack([spec.real, spec.imag], axis=2).astype(dt)


def _attn_kernel(lam_ref, q_ref, kt_ref, v_ref, sub_ref, o_ref, *, nk, qscale, post):
    q = q_ref[0] * qscale
    tq = q.shape[0]
    lam = lam_ref[0, 0]
    sub = sub_ref[...]
    for h in range(HEADS):
        res = []
        for m in range(2):
            c0 = h * 2 * HEAD_DIM + m * HEAD_DIM
            qhm = q[:, c0:c0 + HEAD_DIM].astype(BF16)

            def body(j, carry, c0=c0, h=h, qhm=qhm):
                mx, l, acc = carry
                kb = kt_ref[0, j, c0:c0 + HEAD_DIM, :].astype(BF16)
                s = _dot(qhm, kb)
                mn = jnp.maximum(mx, jnp.max(s, axis=-1, keepdims=True))
                alpha = jnp.exp2(mx - mn)
                p = jnp.exp2(s - mn)
                l = alpha * l + jnp.sum(p, axis=-1, keepdims=True)
                vb = v_ref[0, j, :, h * VAL_DIM:(h + 1) * VAL_DIM].astype(BF16)
                acc = alpha * acc + _dot(p.astype(BF16), vb)
                return mn, l, acc

            init = (jnp.full((tq, 1), -jnp.inf, F32), jnp.zeros((tq, 1), F32),
                    jnp.zeros((tq, VAL_DIM), F32))
            _, l, acc = lax.fori_loop(0, nk, body, init)
            res.append(acc / l)
        o = res[0] - lam * res[1]
        o = o * lax.rsqrt(jnp.mean(o * o, axis=-1, keepdims=True) + EPS) * sub * post
        o_ref[0, :, h * VAL_DIM:(h + 1) * VAL_DIM] = o


def _attn(lam, q, kt, v, subln, tq, post):
    B, L, _ = q.shape
    nk, tk = kt.shape[1], kt.shape[3]
    return pl.pallas_call(
        functools.partial(_attn_kernel, nk=nk, qscale=HEAD_DIM ** -0.5 * LOG2E, post=post),
        grid=(B, L // tq),
        in_specs=[pl.BlockSpec(memory_space=pltpu.SMEM),
                  pl.BlockSpec((1, tq, BR), lambda b, i: (b, i, 0)),
                  pl.BlockSpec((1, nk, BR, tk), lambda b, i: (b, 0, 0, 0)),
                  pl.BlockSpec((1, nk, tk, BR), lambda b, i: (b, 0, 0, 0)),
                  pl.BlockSpec((1, VAL_DIM), lambda b, i: (0, 0))],
        out_specs=pl.BlockSpec((1, tq, BR), lambda b, i: (b, i, 0)),
        out_shape=jax.ShapeDtypeStruct((B, L, BR), F32),
        compiler_params=_cparams("parallel", "parallel"), name="diff_attn",
    )(lam.reshape(1, 1), q, kt, v, subln.reshape(1, VAL_DIM))


def _rope_tables(L, dt):
    n_rows = L // GRID_WIDTH
    row = jnp.repeat(jnp.arange(n_rows, dtype=dt), GRID_WIDTH)
    col = (jnp.arange(L) % GRID_WIDTH).astype(dt)
    inv = ROPE_BASE ** (-jnp.arange(0, ROPE_DIM, 2, dtype=dt) / ROPE_DIM)
    ar = row[:, None] * inv[None, :]
    ac = col[:, None] * inv[None, :]
    ang = jnp.concatenate([ar, ar, ac, ac], axis=-1)
    sign = jnp.tile(jnp.concatenate([-jnp.ones(ROPE_DIM // 2, dt), jnp.ones(ROPE_DIM // 2, dt)]), 2)
    reps = BR // HEAD_DIM
    return jnp.tile(jnp.cos(ang), (1, reps)), jnp.tile(jnp.sin(ang) * sign, (1, reps))


def _mixout_kernel(x_ref, mod_ref, u_ref, yf_ref, yb_ref, uv_ref, hy_ref, da_ref,
                   sd_ref, wg_ref, bg_ref, sn_ref, ws_ref, bs_ref, bn_ref, wo_ref, gf_ref, rt_ref,
                   xo_ref, h2_ref, cb_ref):
    tm = x_ref.shape[0]
    u = u_ref[...]
    y = jax.nn.gelu(sd_ref[...] * u + yf_ref[...] + yb_ref[...])
    y = y * jax.nn.sigmoid(_dot(y.astype(BF16), wg_ref[...]) + bg_ref[...])
    acc = _dot(_rms(y, bn_ref[0:1, :]).astype(BF16), wo_ref[0:BR, :])
    guv = jax.nn.gelu(uv_ref[...])
    gu = guv[:, 0:BR]
    gv = _rms(guv[:, BR:2 * BR], sn_ref[...]).astype(BF16)
    lane = lax.broadcasted_iota(jnp.int32, (SGU_CHUNK, BR), 1)
    hd = BR // SGU_HEADS
    zs = []
    for c in range(tm // SGU_CHUNK):
        vc = gv[c * SGU_CHUNK:(c + 1) * SGU_CHUNK, :]
        z = bs_ref[...]
        for h in range(SGU_HEADS):
            z = z + jnp.where(lane // hd == h, _dot(ws_ref[h], vc), 0.0)
        zs.append(z)
    z = zs[0] if len(zs) == 1 else jnp.concatenate(zs, axis=0)
    acc = acc + _dot(_rms(gu * z, bn_ref[1:2, :]).astype(BF16), wo_ref[BR:2 * BR, :])
    acc = acc + _dot(_rms(hy_ref[...], bn_ref[2:3, :]).astype(BF16), wo_ref[2 * BR:3 * BR, :])
    acc = acc + _dot(da_ref[...].astype(BF16), wo_ref[3 * BR:4 * BR, :])
    xn = x_ref[...] + mod_ref[:, 2 * D:3 * D] * acc
    xo_ref[...] = xn
    h2 = _rms(xn, gf_ref[...]) * (1.0 + mod_ref[:, 4 * D:5 * D]) + mod_ref[:, 3 * D:4 * D]
    h2_ref[...] = h2.astype(BF16)
    logits = _dot3(h2, rt_ref[...])
    el = lax.broadcasted_iota(jnp.int32, logits.shape, 1)
    logits = jnp.where(el < N_EXPERTS, logits, -jnp.inf)
    e = jnp.exp(logits - jnp.max(logits, axis=-1, keepdims=True))
    probs = e / jnp.sum(e, axis=-1, keepdims=True)
    big = logits.shape[1]
    m1 = jnp.max(probs, axis=-1, keepdims=True)
    i1 = jnp.min(jnp.where(probs == m1, el, big), axis=-1, keepdims=True)
    p2 = jnp.where((el == i1) | (el >= N_EXPERTS), -1.0, probs)
    m2 = jnp.max(p2, axis=-1, keepdims=True)
    i2 = jnp.min(jnp.where(p2 == m2, el, big), axis=-1, keepdims=True)
    tot = m1 + m2
    cb_ref[...] = jnp.where(el == i1, m1 / tot, 0.0) + jnp.where(el == i2, m2 / tot, 0.0)


def _mixout(x, modg, u_tm2, yf2, yb2, uv, hy, da, lp, B, L, tm):
    T = B * L
    nt = L // tm
    tok = lambda n: pl.BlockSpec((tm, n), lambda i: (i, 0))
    tmaj = pl.BlockSpec((tm, BR), lambda i: (i % nt, i // nt))
    const = lambda shp: pl.BlockSpec(shp, lambda i: (0,) * len(shp))
    return pl.pallas_call(
        _mixout_kernel, grid=(T // tm,),
        in_specs=[tok(D), _mod_spec(modg, tm, L),
                  tmaj, tmaj, tmaj, tok(2 * BR), tok(BR), tok(BR),
                  const((1, BR)), const((BR, BR)), const((1, BR)), const((1, BR)),
                  const((SGU_HEADS, SGU_CHUNK, SGU_CHUNK)), const((SGU_CHUNK, BR)),
                  const((3, BR)), const((D, D)), const((1, D)), const((D, 128))],
        out_specs=[tok(D), tok(D), tok(128)],
        out_shape=[jax.ShapeDtypeStruct((T, D), F32), jax.ShapeDtypeStruct((T, D), BF16),
                   jax.ShapeDtypeStruct((T, 128), F32)],
        compiler_params=_cparams("parallel"), name="mix_out",
    )(x, modg, u_tm2, yf2, yb2, uv, hy, da,
      lp['s5_d'], lp['s5_w_glu'], lp['s5_b_glu'], lp['sgu_norm'], lp['sgu_w_s'], lp['sgu_b'],
      lp['branch_norm'], lp['w_out'], lp['norm_ffn'], lp['router'])


def _ffn_kernel(h_ref, x_ref, mod_ref, w1_ref, w3_ref, w2_ref, o_ref, acc_ref):
    j = pl.program_id(1)
    h = h_ref[...]
    a = _dot(h, w1_ref[...])
    t = (a * jax.nn.sigmoid(a) * _dot(h, w3_ref[...])).astype(BF16)
    part = _dot(t, w2_ref[...])

    @pl.when(j == 0)
    def _():
        acc_ref[...] = part

    @pl.when(j > 0)
    def _():
        acc_ref[...] += part

    @pl.when(j == pl.num_programs(1) - 1)
    def _():
        o_ref[...] = x_ref[...] + mod_ref[:, 5 * D:6 * D] * acc_ref[...]


def _ffn(h2, x, modg, w1, w3, w2, L, tm, tf):
    T = x.shape[0]
    nt = L // tm
    dff = w1.shape[1]
    return pl.pallas_call(
        _ffn_kernel, grid=(T // tm, dff // tf),
        in_specs=[pl.BlockSpec((tm, D), lambda i, j: (i, 0)),
                  pl.BlockSpec((tm, D), lambda i, j: (i, 0)),
                  _mod_spec(modg, tm, L),
                  pl.BlockSpec((D, tf), lambda i, j: (0, j)),
                  pl.BlockSpec((D, tf), lambda i, j: (0, j)),
                  pl.BlockSpec((tf, D), lambda i, j: (j, 0))],
        out_specs=pl.BlockSpec((tm, D), lambda i, j: (i, 0)),
        out_shape=jax.ShapeDtypeStruct((T, D), F32),
        scratch_shapes=[pltpu.VMEM((tm, D), F32)],
        compiler_params=_cparams("parallel", "arbitrary"), name="ffn",
    )(h2, x, modg, w1, w3, w2)


def _moe_kernel(h_ref, x_ref, mod_ref, cb_ref, w1_ref, w3_ref, w2_ref, o_ref, acc_ref):
    e = pl.program_id(1)
    j = pl.program_id(2)
    h = h_ref[...]
    a = _dot(h, w1_ref[0])
    t = (a * jax.nn.sigmoid(a) * _dot(h, w3_ref[0])).astype(BF16)
    cb = cb_ref[...]
    el = lax.broadcasted_iota(jnp.int32, cb.shape, 1)
    ce = jnp.sum(jnp.where(el == e, cb, 0.0), axis=-1, keepdims=True)
    part = ce * _dot(t, w2_ref[0])
    first = (e == 0) & (j == 0)

    @pl.when(first)
    def _():
        acc_ref[...] = part

    @pl.when(jnp.logical_not(first))
    def _():
        acc_ref[...] += part

    @pl.when((e == pl.num_programs(1) - 1) & (j == pl.num_programs(2) - 1))
    def _():
        o_ref[...] = x_ref[...] + mod_ref[:, 5 * D:6 * D] * acc_ref[...]


def _moe(h2, x, modg, cb, w1, w3, w2, L, tm, tf):
    T = x.shape[0]
    nt = L // tm
    ne, _, dff = w1.shape
    return pl.pallas_call(
        _moe_kernel, grid=(T // tm, ne, dff // tf),
        in_specs=[pl.BlockSpec((tm, D), lambda i, e, j: (i, 0)),
                  pl.BlockSpec((tm, D), lambda i, e, j: (i, 0)),
                  _mod_spec(modg, tm, L),
                  pl.BlockSpec((tm, 128), lambda i, e, j: (i, 0)),
                  pl.BlockSpec((1, D, tf), lambda i, e, j: (e, 0, j)),
                  pl.BlockSpec((1, D, tf), lambda i, e, j: (e, 0, j)),
                  pl.BlockSpec((1, tf, D), lambda i, e, j: (e, j, 0))],
        out_specs=pl.BlockSpec((tm, D), lambda i, e, j: (i, 0)),
        out_shape=jax.ShapeDtypeStruct((T, D), F32),
        scratch_shapes=[pltpu.VMEM((tm, D), F32)],
        compiler_params=_cparams("parallel", "arbitrary", "arbitrary"), name="moe",
    )(h2, x, modg, cb, w1, w3, w2)


def _final_kernel(x_ref, g_ref, o_ref):
    o_ref[...] = _rms(x_ref[...], g_ref[...])


def _final_norm(x, g, tm):
    T = x.shape[0]
    return pl.pallas_call(
        _final_kernel, grid=(T // tm,),
        in_specs=[pl.BlockSpec((tm, D), lambda i: (i, 0)), pl.BlockSpec((1, D), lambda i: (0, 0))],
        out_specs=pl.BlockSpec((tm, D), lambda i: (i, 0)),
        out_shape=jax.ShapeDtypeStruct((T, D), F32),
        compiler_params=_cparams("parallel"), name="final_norm",
    )(x, g.reshape(1, D))


def _layer(x, modg, lp, l, B, L, ctx):
    is_ctx = ctx is None
    T = B * L
    Bp = -(-B // 8) * 8
    tm = min(L, 512)
    rope_tabs = None if is_ctx else _rope_tables(L, F32)
    u_tm, uv, z, q, k, v = _inproj(x, modg, lp['norm_mix'], lp['w_in'], B, L, tm, rope_tabs)

    u3 = jnp.pad(u_tm.reshape(L, B, BR), ((0, 0), (0, Bp - B), (0, 0)))
    if is_ctx:
        h0 = jnp.zeros((2, Bp, 2 * S5_STATES), F32)
    else:
        h0 = ctx[2]
        h0 = jnp.transpose(h0, (1, 0, 4, 2, 3)).reshape(2, B, 2 * S5_STATES)
        h0 = jnp.pad(h0, ((0, 0), (0, Bp - B), (0, 0)))
    yf, yb, hfin = _s5(u3, h0, lp['s5_wb'], lp['s5_wc'], lp['s5_a'], L, Bp, min(L, 128))

    hv, hx1, hx2 = _short_conv(z, lp['hy_conv_w'], lp['hy_conv_b'], L, tm)
    consts = lp['hy_consts_ctx'] if is_ctx else lp['hy_consts_lat']
    hf = lp['hy_spec_ctx'] if is_ctx else lp['hy_spec_lat']
    y1 = _hy_conv(hv, hx1, lp['hy_bias'][0], hf[0], consts, L)
    y_hy = _hy_conv(y1, hx2, lp['hy_bias'][1], hf[1], consts, L)

    tk = 256
    q3 = q.reshape(B, L, BR)
    k3 = k.reshape(B, L, BR)
    v3 = v.reshape(B, L, BR)
    if is_ctx:
        k_all, v_all = k3, v3
    else:
        k_all = jnp.concatenate([ctx[0].reshape(B, -1, BR), k3], axis=1)
        v_all = jnp.concatenate([ctx[1].reshape(B, -1, BR), v3], axis=1)
    nk = k_all.shape[1] // tk
    kt = jnp.transpose(k_all.reshape(B, nk, tk, BR), (0, 1, 3, 2))
    da = _attn(lp['da_lam'], q3, kt, v_all.reshape(B, nk, tk, BR), lp['da_subln'],
               min(L, 256), 1.0 - lp['lam_init'])

    x, h2, cb = _mixout(x, modg, u_tm, yf.reshape(L, Bp * BR), yb.reshape(L, Bp * BR), uv,
                        y_hy, da.reshape(T, BR), lp, B, L, tm)
    tmf = min(T, 1024)
    if l % 2 == 0:
        x = _ffn(h2, x, modg, lp['ffn_w1'], lp['ffn_w3'], lp['ffn_w2'], L, tmf,
                 lp['ffn_w1'].shape[1] // 2)
    else:
        x = _moe(h2, x, modg, cb, lp['moe_w1'], lp['moe_w3'], lp['moe_w2'], L, tmf, 512)
    if is_ctx:
        fin = hfin[:, :B].reshape(2, B, 2, S5_G, S5_N)
        fin = jnp.transpose(fin, (1, 0, 3, 4, 2))
        return x, k3.reshape(B, L, HEADS, 2 * HEAD_DIM), v3.reshape(B, L, HEADS, VAL_DIM), fin
    return x


def kernel(x_prompt, x_sample, cache_k, cache_v, state_ssm, c, c_ctx, w_ada, b_ada, norm_mix, norm_ffn, w_in, w_out, branch_norm, s5_lam_re, s5_lam_im, s5_log_dt, s5_b_re, s5_b_im, s5_c_re, s5_c_im, s5_d, s5_w_glu, s5_b_glu, sgu_norm, sgu_w_s, sgu_b_s, hy_conv_w, hy_conv_b, hy_w1, hy_b1, hy_w2, hy_b2, hy_w3, hy_bias, da_lq1, da_lk1, da_lq2, da_lk2, da_subln, ffn_w1, ffn_w3, ffn_w2, moe_router, moe_w1, moe_w3, moe_w2, norm_final):
    depth = w_in.shape[0]
    Bc, Lc, _ = x_prompt.shape
    Bs, Ls, _ = x_sample.shape

    cond = jnp.concatenate([c_ctx[None, :], c], axis=0)
    cond8 = jnp.pad(cond, ((0, 8 - cond.shape[0]), (0, 0)))
    mod = _ada(cond8, w_ada, b_ada)

    cc = _hy_consts(Lc)
    cl = _hy_consts(Ls)

    layers = []
    for l in range(depth):
        j = l // 2
        wb, wc, a = _s5_prep(s5_lam_re[l], s5_lam_im[l], s5_log_dt[l], s5_b_re[l], s5_b_im[l],
                             s5_c_re[l], s5_c_im[l])
        lam_init = 0.8 - 0.6 * math.exp(-0.3 * l)
        lam = (jnp.exp(jnp.sum(da_lq1[l] * da_lk1[l])) - jnp.exp(jnp.sum(da_lq2[l] * da_lk2[l]))
               + lam_init)
        hy_args = (hy_w1[l], hy_b1[l], hy_w2[l], hy_b2[l], hy_w3[l])
        lp = dict(
            norm_mix=norm_mix[l], w_in=w_in[l].astype(BF16), w_out=w_out[l].astype(BF16),
            norm_ffn=norm_ffn[l].reshape(1, D), branch_norm=branch_norm[l],
            s5_wb=wb, s5_wc=wc, s5_a=a, s5_d=s5_d[l].reshape(1, BR),
            s5_w_glu=s5_w_glu[l].astype(BF16), s5_b_glu=s5_b_glu[l].reshape(1, BR),
            sgu_norm=sgu_norm[l].reshape(1, BR), sgu_w_s=sgu_w_s[l].astype(BF16),
            sgu_b=jnp.repeat(sgu_b_s[l].T, BR // SGU_HEADS, axis=1),
            hy_conv_w=hy_conv_w[l], hy_conv_b=hy_conv_b[l], hy_bias=hy_bias[l],
            hy_consts_ctx=cc, hy_consts_lat=cl,
            hy_spec_ctx=_hy_filter_spectrum(Lc, *hy_args, cc[4]),
            hy_spec_lat=_hy_filter_spectrum(Ls, *hy_args, cl[4]),
            da_lam=lam, lam_init=lam_init, da_subln=da_subln[l],
        )
        if l % 2 == 0:
            lp.update(ffn_w1=ffn_w1[j].astype(BF16), ffn_w3=ffn_w3[j].astype(BF16),
                      ffn_w2=ffn_w2[j].astype(BF16),
                      router=jnp.zeros((D, 128), F32))
        else:
            lp.update(moe_w1=moe_w1[j].astype(BF16), moe_w3=moe_w3[j].astype(BF16),
                      moe_w2=moe_w2[j].astype(BF16),
                      router=jnp.pad(moe_router[j], ((0, 0), (0, 128 - N_EXPERTS))))
        layers.append(lp)

    xc = x_prompt.reshape(Bc * Lc, D)
    ks, vs, ss = [], [], []
    for l in range(depth):
        xc, k_l, v_l, s_l = _layer(xc, mod[l, 0:1].reshape(1, 1, 6 * D), layers[l], l, Bc, Lc, None)
        ks.append(k_l)
        vs.append(v_l)
        ss.append(s_l)
    y_prompt = _final_norm(xc, norm_final, 512).reshape(Bc, Lc, D)

    xs = x_sample.reshape(Bs * Ls, D)
    for l in range(depth):
        xs = _layer(xs, mod[l, 1:1 + Bs].reshape(Bs, 1, 6 * D), layers[l], l, Bs, Ls,
                    (cache_k[:, l], cache_v[:, l], state_ssm[:, l]))
    y_sample = _final_norm(xs, norm_final, 512).reshape(Bs, Ls, D)
    return (y_prompt, y_sample, jnp.stack(ks, axis=1), jnp.stack(vs, axis=1), jnp.stack(ss, axis=1))
```

```python
import functools
import math

import jax
import jax.numpy as jnp
import numpy as np
from jax import lax
from jax.experimental import pallas as pl
from jax.experimental.pallas import tpu as pltpu

F32 = jnp.float32
BF16 = jnp.bfloat16

D = 1024
BR = 256
PROJ = 9 * BR
S5_G, S5_N, S5_P = 16, 64, 16
S5_STATES = S5_G * S5_N
SGU_CHUNK, SGU_HEADS = 128, 4
HEADS, HEAD_DIM, VAL_DIM = 4, 32, 64
GRID_WIDTH = 64
ROPE_DIM = HEAD_DIM // 2
ROPE_BASE = 10000.0
HY_EMB, HY_BANDS = 33, 16
HY_DECAY_TARGET, HY_FAST, HY_SLOW = 1e-2, 0.3, 1.5
N_EXPERTS = 8
EPS = 1e-6
LOG2E = 1.4426950408889634

DFT_N2 = 128
DFT_ROWS = 32
VMEM_LIMIT = 56 * 1024 * 1024


def _cparams(*sem):
    return pltpu.CompilerParams(dimension_semantics=sem, vmem_limit_bytes=VMEM_LIMIT)


def _dot(a, b):
    return jnp.dot(a, b, preferred_element_type=F32)


def _dot3(a, b):
    a_hi = a.astype(BF16)
    b_hi = b.astype(BF16)
    a_lo = (a - a_hi.astype(F32)).astype(BF16)
    b_lo = (b - b_hi.astype(F32)).astype(BF16)
    return _dot(a_hi, b_hi) + (_dot(a_hi, b_lo) + _dot(a_lo, b_hi))


def _rms(x, g):
    return x * lax.rsqrt(jnp.mean(x * x, axis=-1, keepdims=True) + EPS) * g


def _mod_spec(modg, tm, L):
    nmod = modg.shape[0]
    return pl.BlockSpec((None, 1, 6 * D), lambda i, *_: ((i * tm // L) % nmod, 0, 0))


def _ada_kernel(c_ref, w_ref, b_ref, o_ref):
    c = c_ref[...]
    s = (c * jax.nn.sigmoid(c)).astype(BF16)
    o_ref[0] = _dot(s, w_ref[0].astype(BF16)) + b_ref[0]


def _ada(cond8, w_ada, b_ada):
    depth = w_ada.shape[0]
    tn = 1536
    return pl.pallas_call(
        _ada_kernel,
        grid=(depth, 6 * D // tn),
        in_specs=[pl.BlockSpec((8, D), lambda l, j: (0, 0)),
                  pl.BlockSpec((1, D, tn), lambda l, j: (l, 0, j)),
                  pl.BlockSpec((1, 1, tn), lambda l, j: (l, 0, j))],
        out_specs=pl.BlockSpec((1, 8, tn), lambda l, j: (l, 0, j)),
        out_shape=jax.ShapeDtypeStruct((depth, 8, 6 * D), F32),
        compiler_params=_cparams("parallel", "parallel"),
        name="ada",
    )(cond8, w_ada, b_ada.reshape(depth, 1, 6 * D))


def _inproj_kernel(*refs, rope):
    if rope:
        (x_ref, mod_ref, g_ref, w_ref, cos_ref, sin_ref,
         u_ref, uv_ref, z_ref, q_ref, k_ref, v_ref) = refs
    else:
        x_ref, mod_ref, g_ref, w_ref, u_ref, uv_ref, z_ref, q_ref, k_ref, v_ref = refs
    x = x_ref[...]
    h = _rms(x, g_ref[...]) * (1.0 + mod_ref[:, D:2 * D]) + mod_ref[:, 0:D]
    p = _dot(h.astype(BF16), w_ref[...])
    u_ref[...] = p[:, 0:BR]
    uv_ref[...] = p[:, BR:3 * BR]
    z_ref[...] = p[:, 3 * BR:6 * BR]
    q = p[:, 6 * BR:7 * BR]
    k = p[:, 7 * BR:8 * BR]
    if rope:
        cs = cos_ref[...]
        sn = sin_ref[...]
        lane = lax.broadcasted_iota(jnp.int32, q.shape, 1)
        first = (lane % (2 * (ROPE_DIM // 2))) < (ROPE_DIM // 2)
        half = ROPE_DIM // 2

        def rot(t):
            return jnp.where(first, pltpu.roll(t, BR - half, 1), pltpu.roll(t, half, 1))

        q = q * cs + rot(q) * sn
        k = k * cs + rot(k) * sn
    q_ref[...] = q
    k_ref[...] = k
    v_ref[...] = p[:, 8 * BR:9 * BR]


def _inproj(x, modg, g, w_bf, B, L, tm, rope_tabs):
    T = B * L
    nt = L // tm
    rope = rope_tabs is not None
    in_specs = [pl.BlockSpec((tm, D), lambda i: (i, 0)),
                _mod_spec(modg, tm, L),
                pl.BlockSpec((1, D), lambda i: (0, 0)),
                pl.BlockSpec((D, PROJ), lambda i: (0, 0))]
    args = [x, modg, g.reshape(1, D), w_bf]
    if rope:
        in_specs += [pl.BlockSpec((tm, BR), lambda i: (i % nt, 0))] * 2
        args += list(rope_tabs)
    tok = lambda n: pl.BlockSpec((tm, n), lambda i: (i, 0))
    out_specs = [pl.BlockSpec((tm, BR), lambda i: (i % nt, i // nt)),
                 tok(2 * BR), tok(3 * BR), tok(BR), tok(BR), tok(BR)]
    out_shape = [jax.ShapeDtypeStruct((L, B * BR), F32),
                 jax.ShapeDtypeStruct((T, 2 * BR), F32),
                 jax.ShapeDtypeStruct((T, 3 * BR), F32),
                 jax.ShapeDtypeStruct((T, BR), F32),
                 jax.ShapeDtypeStruct((T, BR), F32),
                 jax.ShapeDtypeStruct((T, BR), F32)]
    return pl.pallas_call(
        functools.partial(_inproj_kernel, rope=rope),
        grid=(T // tm,), in_specs=in_specs, out_specs=out_specs, out_shape=out_shape,
        compiler_params=_cparams("parallel"), name="inproj",
    )(*args)


def _s5_kernel(uf_ref, ub_ref, h0_ref, wb_ref, wc_ref, a_ref, yf_ref, yb_ref, hfin_ref,
               xs_ref, hc_ref, *, tt):
    i = pl.program_id(1)
    last = pl.num_programs(1) - 1
    half = S5_STATES // 2

    @pl.when(i == 0)
    def _():
        hc_ref[...] = h0_ref[...]

    for d in range(2):
        u_ref = uf_ref if d == 0 else ub_ref
        y_ref = yf_ref if d == 0 else yb_ref
        u2 = u_ref[...].reshape(tt * 8, BR).astype(BF16)
        xs_ref[...] = _dot(u2, wb_ref[d])
        for c in range(2):
            cr = slice(c * half, (c + 1) * half)
            ci = slice(S5_STATES + c * half, S5_STATES + (c + 1) * half)
            ar = jnp.broadcast_to(a_ref[d, 0:1, cr], (8, half))
            ai = jnp.broadcast_to(a_ref[d, 1:2, cr], (8, half))

            def body(s, carry, cr=cr, ci=ci, ar=ar, ai=ai, d=d):
                hr, hi = carry
                t = s if d == 0 else tt - 1 - s
                r0 = pl.multiple_of(t * 8, 8)
                nr = ar * hr - ai * hi + xs_ref[pl.ds(r0, 8), cr]
                ni = ar * hi + ai * hr + xs_ref[pl.ds(r0, 8), ci]
                xs_ref[pl.ds(r0, 8), cr] = nr
                xs_ref[pl.ds(r0, 8), ci] = ni
                return nr, ni

            hr, hi = lax.fori_loop(0, tt, body, (hc_ref[d, :, cr], hc_ref[d, :, ci]))
            hc_ref[d, :, cr] = hr
            hc_ref[d, :, ci] = hi
        y = _dot(xs_ref[...].astype(BF16), wc_ref[d])
        y_ref[...] = y.reshape(tt, 8, BR)

    @pl.when(i == last)
    def _():
        hfin_ref[...] = hc_ref[...]


def _s5(u_tm, h0, wb, wc, a, L, Bp, tt):
    nT = L // tt
    ng = Bp // 8
    blk = lambda f: pl.BlockSpec((tt, 8, BR), f)
    const = lambda shp: pl.BlockSpec(shp, lambda g, i: (0,) * len(shp))
    return pl.pallas_call(
        functools.partial(_s5_kernel, tt=tt),
        grid=(ng, nT),
        in_specs=[blk(lambda g, i: (i, g, 0)), blk(lambda g, i: (nT - 1 - i, g, 0)),
                  pl.BlockSpec((2, 8, 2 * S5_STATES), lambda g, i: (0, g, 0)),
                  const((2, BR, 2 * S5_STATES)), const((2, 2 * S5_STATES, BR)),
                  const((2, 2, S5_STATES))],
        out_specs=[blk(lambda g, i: (i, g, 0)), blk(lambda g, i: (nT - 1 - i, g, 0)),
                   pl.BlockSpec((2, 8, 2 * S5_STATES), lambda g, i: (0, g, 0))],
        out_shape=[jax.ShapeDtypeStruct((L, Bp, BR), F32),
                   jax.ShapeDtypeStruct((L, Bp, BR), F32),
                   jax.ShapeDtypeStruct((2, Bp, 2 * S5_STATES), F32)],
        scratch_shapes=[pltpu.VMEM((tt * 8, 2 * S5_STATES), F32),
                        pltpu.VMEM((2, 8, 2 * S5_STATES), F32)],
        compiler_params=_cparams("parallel", "arbitrary"), name="s5",
    )(u_tm, u_tm, h0, wb, wc, a)


def _s5_prep(lam_re, lam_im, log_dt, b_re, b_im, c_re, c_im):
    dt = jnp.exp(log_dt)[..., None]
    mag = jnp.exp(lam_re * dt)
    lb_re = mag * jnp.cos(lam_im * dt)
    lb_im = mag * jnp.sin(lam_im * dt)
    den = lam_re * lam_re + lam_im * lam_im
    nr = lb_re - 1.0
    coef_re = ((nr * lam_re + lb_im * lam_im) / den)[..., None]
    coef_im = ((lb_im * lam_re - nr * lam_im) / den)[..., None]
    bp_re = coef_re * b_re - coef_im * b_im
    bp_im = coef_re * b_im + coef_im * b_re
    eye = jnp.eye(S5_G, dtype=lam_re.dtype)

    def blockdiag_in(b):
        return jnp.einsum('dgnp,gh->dgphn', b, eye).reshape(2, BR, S5_STATES)

    def blockdiag_out(c):
        return jnp.einsum('dgpn,gh->dgnhp', c, eye).reshape(2, S5_STATES, BR)

    wb = jnp.concatenate([blockdiag_in(bp_re), blockdiag_in(bp_im)], axis=-1)
    wc = jnp.concatenate([blockdiag_out(c_re), -blockdiag_out(c_im)], axis=1)
    a = jnp.stack([lb_re.reshape(2, S5_STATES), lb_im.reshape(2, S5_STATES)], axis=1)
    return wb.astype(BF16), wc.astype(BF16), a


def _short_kernel(z_ref, zp_ref, zn_ref, w_ref, b_ref, v_ref, x1_ref, x2_ref, *, nt):
    j = pl.program_id(0) % nt
    z = z_ref[...]
    tm = z.shape[0]
    row = lax.broadcasted_iota(jnp.int32, z.shape, 0)
    prev_row = jnp.where(j > 0, zp_ref[7:8, :], 0.0)
    next_row = jnp.where(j < nt - 1, zn_ref[0:1, :], 0.0)
    zprev = jnp.where(row == 0, prev_row, pltpu.roll(z, 1, 0))
    znext = jnp.where(row == tm - 1, next_row, pltpu.roll(z, tm - 1, 0))
    y = zprev * w_ref[0:1, :] + z * w_ref[1:2, :] + znext * w_ref[2:3, :] + b_ref[...]
    v_ref[...] = y[:, 0:BR]
    x1_ref[...] = y[:, BR:2 * BR]
    x2_ref[...] = y[:, 2 * BR:3 * BR]


def _short_conv(z, w, b, L, tm):
    T = z.shape[0]
    nt = L // tm
    r8 = tm // 8
    nblk8 = T // 8
    out = jax.ShapeDtypeStruct((T, BR), F32)
    return pl.pallas_call(
        functools.partial(_short_kernel, nt=nt),
        grid=(T // tm,),
        in_specs=[pl.BlockSpec((tm, 3 * BR), lambda i: (i, 0)),
                  pl.BlockSpec((8, 3 * BR), lambda i: (jnp.maximum(i * r8 - 1, 0), 0)),
                  pl.BlockSpec((8, 3 * BR), lambda i: (jnp.minimum((i + 1) * r8, nblk8 - 1), 0)),
                  pl.BlockSpec((3, 3 * BR), lambda i: (0, 0)),
                  pl.BlockSpec((1, 3 * BR), lambda i: (0, 0))],
        out_specs=[pl.BlockSpec((tm, BR), lambda i: (i, 0))] * 3,
        out_shape=[out, out, out],
        compiler_params=_cparams("parallel"), name="short_conv",
    )(z, z, z, w, b.reshape(1, 3 * BR))


def _hy_consts(L):
    n = 2 * L
    n2 = DFT_N2
    n1 = n // n2
    r = L // n2
    s = DFT_ROWS // r

    def cs(num, den):
        th = (2.0 * math.pi / den) * (num % den).astype(F32)
        return jnp.cos(th), jnp.sin(th)

    ar = lambda m: jnp.arange(m, dtype=jnp.int32)
    eye = jnp.eye(s, dtype=F32)
    c, sn = cs(ar(n1)[:, None] * ar(r)[None, :], n1)
    f1 = jnp.stack([c, -sn], axis=1).reshape(2 * n1, r)
    m1 = jnp.kron(eye, f1)
    c, sn = cs((ar(r)[:, None] + r // 2) * ar(n1)[None, :], n1)
    g1 = jnp.stack([c, -sn], axis=2).reshape(r, 2 * n1) * (1.0 / n)
    gm = jnp.kron(eye, g1)
    kk1 = ar(n1)[:, None, None]
    kk2 = ar(n2)[None, :, None]
    jj2 = ar(n2)[None, None, :]
    c, sn = cs(jj2 * kk2 * n1 + jj2 * kk1, n)
    w2f = jnp.concatenate([jnp.concatenate([c, sn], axis=2),
                           jnp.concatenate([-sn, c], axis=2)], axis=1)
    ct, st = jnp.swapaxes(c, 1, 2), jnp.swapaxes(sn, 1, 2)
    w2i = jnp.concatenate([jnp.concatenate([ct, -st], axis=2),
                           jnp.concatenate([st, ct], axis=2)], axis=1)
    return m1.astype(BF16), gm.astype(BF16), w2f.astype(BF16), w2i.astype(BF16), n1


def _hyA_kernel(x_ref, m_ref, o_ref):
    o_ref[...] = _dot(m_ref[...], x_ref[...].astype(BF16)).astype(BF16)


def _hyB_kernel(a_ref, wf_ref, wi_ref, hf_ref, o_ref, *, rb):
    n2 = DFT_N2
    for j in range(rb):
        a = a_ref[j].reshape(2 * n2, BR)
        x = _dot(wf_ref[j], a)
        xr, xi = x[0:n2], x[n2:2 * n2]
        hr, hi = hf_ref[j, 0], hf_ref[j, 1]
        y = jnp.concatenate([xr * hr - xi * hi, xr * hi + xi * hr], axis=0).astype(BF16)
        o_ref[j] = _dot(wi_ref[j], y).astype(BF16).reshape(2, n2, BR)


def _hyC_kernel(a_ref, g_ref, v_ref, xg_ref, bias_ref, o_ref):
    y = _dot(g_ref[...], a_ref[...])
    o_ref[...] = xg_ref[...] * (y + v_ref[...] * bias_ref[...])


def _hy_conv(v, xg, bias, hf, consts, L):
    m1, gm, w2f, w2i, n1 = consts
    T = v.shape[0]
    n2 = DFT_N2
    rows = T // n2
    cols = n2 * BR
    tc = 4096
    nrb = rows // DFT_ROWS
    v2 = v.reshape(rows, cols)
    x2 = xg.reshape(rows, cols)
    a = pl.pallas_call(
        _hyA_kernel, grid=(nrb, cols // tc),
        in_specs=[pl.BlockSpec((DFT_ROWS, tc), lambda i, j: (i, j)),
                  pl.BlockSpec((4 * DFT_ROWS, DFT_ROWS), lambda i, j: (0, 0))],
        out_specs=pl.BlockSpec((4 * DFT_ROWS, tc), lambda i, j: (i, j)),
        out_shape=jax.ShapeDtypeStruct((4 * rows, cols), BF16),
        compiler_params=_cparams("parallel", "parallel"), name="hy_stage1",
    )(v2, m1)
    R = 2 * rows
    rb = min(8, n1)
    nkb = n1 // rb
    nseq = R // n1
    a5 = a.reshape(R, 2, n2, BR)
    b5 = pl.pallas_call(
        functools.partial(_hyB_kernel, rb=rb), grid=(nkb, nseq),
        in_specs=[pl.BlockSpec((rb, 2, n2, BR), lambda kb, s: (s * nkb + kb, 0, 0, 0)),
                  pl.BlockSpec((rb, 2 * n2, 2 * n2), lambda kb, s: (kb, 0, 0)),
                  pl.BlockSpec((rb, 2 * n2, 2 * n2), lambda kb, s: (kb, 0, 0)),
                  pl.BlockSpec((rb, 2, n2, BR), lambda kb, s: (kb, 0, 0, 0))],
        out_specs=pl.BlockSpec((rb, 2, n2, BR), lambda kb, s: (s * nkb + kb, 0, 0, 0)),
        out_shape=jax.ShapeDtypeStruct((R, 2, n2, BR), BF16),
        compiler_params=_cparams("parallel", "parallel"), name="hy_stage2",
    )(a5, w2f, w2i, hf)
    bt = jnp.tile(bias.reshape(1, BR), (1, tc // BR))
    y = pl.pallas_call(
        _hyC_kernel, grid=(nrb, cols // tc),
        in_specs=[pl.BlockSpec((4 * DFT_ROWS, tc), lambda i, j: (i, j)),
                  pl.BlockSpec((DFT_ROWS, 4 * DFT_ROWS), lambda i, j: (0, 0)),
                  pl.BlockSpec((DFT_ROWS, tc), lambda i, j: (i, j)),
                  pl.BlockSpec((DFT_ROWS, tc), lambda i, j: (i, j)),
                  pl.BlockSpec((1, tc), lambda i, j: (0, 0))],
        out_specs=pl.BlockSpec((DFT_ROWS, tc), lambda i, j: (i, j)),
        out_shape=jax.ShapeDtypeStruct((rows, cols), F32),
        compiler_params=_cparams("parallel", "parallel"), name="hy_stage3",
    )(b5.reshape(4 * rows, cols), gm, v2, x2, bt)
    return y.reshape(T, BR)


def _hy_filter_spectrum(L, w1, b1, w2, b2, w3, n1):
    dt = w1.dtype
    pos = jnp.arange(L, dtype=dt)
    t01 = jnp.linspace(0.0, 1.0, L, dtype=dt)[:, None]
    w = (2.0 * math.pi / L) * pos[:, None]
    bands = jnp.linspace(1e-4, HY_BANDS - 1, HY_BANDS, dtype=dt)[None, :]
    feats = jnp.concatenate([t01, jnp.cos(bands * w), -jnp.sin(bands * w)], axis=-1)
    h = jnp.sin(feats @ w1 + b1)
    h = jnp.sin(h @ w2 + b2)
    h = h @ w3
    dist = jnp.abs(pos - (L // 2)) / L
    decay = jnp.abs(jnp.linspace(math.log(HY_DECAY_TARGET) / HY_SLOW,
                                 math.log(HY_DECAY_TARGET) / HY_FAST, 2 * BR, dtype=dt))
    filt = h * jnp.exp(-dist[:, None] * decay[None, :])
    spec = jnp.fft.fft(filt, n=2 * L, axis=0)
    spec = spec.reshape(DFT_N2, n1, 2, BR).transpose(2, 1, 0, 3)
    return jnp.stack([spec.real, spec.imag], axis=2).astype(dt)


def _attn_kernel(lam_ref, q_ref, kt_ref, v_ref, sub_ref, o_ref, *, qscale, post):
    q = (q_ref[0] * qscale).astype(BF16)
    lam = lam_ref[0, 0]
    sub = sub_ref[...]
    for h in range(HEADS):
        vh = v_ref[0, :, h * VAL_DIM:(h + 1) * VAL_DIM]
        res = []
        for m in range(2):
            c0 = h * 2 * HEAD_DIM + m * HEAD_DIM
            s = _dot(q[:, c0:c0 + HEAD_DIM], kt_ref[0, c0:c0 + HEAD_DIM, :])
            p = jnp.exp2(s - jnp.max(s, axis=-1, keepdims=True))
            l = jnp.sum(p, axis=-1, keepdims=True)
            res.append(_dot(p.astype(BF16), vh) / l)
        o = res[0] - lam * res[1]
        o = o * lax.rsqrt(jnp.mean(o * o, axis=-1, keepdims=True) + EPS) * sub * post
        o_ref[0, :, h * VAL_DIM:(h + 1) * VAL_DIM] = o


def _attn(lam, q, kt, v, subln, tq, post):
    B, L, _ = q.shape
    Lk = kt.shape[2]
    return pl.pallas_call(
        functools.partial(_attn_kernel, qscale=HEAD_DIM ** -0.5 * LOG2E, post=post),
        grid=(B, L // tq),
        in_specs=[pl.BlockSpec(memory_space=pltpu.SMEM),
                  pl.BlockSpec((1, tq, BR), lambda b, i: (b, i, 0)),
                  pl.BlockSpec((1, BR, Lk), lambda b, i: (b, 0, 0)),
                  pl.BlockSpec((1, Lk, BR), lambda b, i: (b, 0, 0)),
                  pl.BlockSpec((1, VAL_DIM), lambda b, i: (0, 0))],
        out_specs=pl.BlockSpec((1, tq, BR), lambda b, i: (b, i, 0)),
        out_shape=jax.ShapeDtypeStruct((B, L, BR), F32),
        compiler_params=_cparams("parallel", "parallel"), name="diff_attn",
    )(lam.reshape(1, 1), q, kt, v, subln.reshape(1, VAL_DIM))


def _attn_inputs(k3, v3, ctx):
    B = k3.shape[0]
    if ctx is not None:
        k3 = jnp.concatenate([ctx[0].reshape(B, -1, BR), k3], axis=1)
        v3 = jnp.concatenate([ctx[1].reshape(B, -1, BR), v3], axis=1)
    return jnp.swapaxes(k3, 1, 2).astype(BF16), v3.astype(BF16)


def _attn_latent_test(q, k, v, ck, cv):
    kt, vb = _attn_inputs(k, v, (ck, cv))
    return _attn(jnp.float32(0.5), q, kt, vb, jnp.ones((VAL_DIM,), F32), 256, 0.5)


def _rope_tables(L, dt):
    n_rows = L // GRID_WIDTH
    row = jnp.repeat(jnp.arange(n_rows, dtype=dt), GRID_WIDTH)
    col = (jnp.arange(L) % GRID_WIDTH).astype(dt)
    inv = ROPE_BASE ** (-jnp.arange(0, ROPE_DIM, 2, dtype=dt) / ROPE_DIM)
    ar = row[:, None] * inv[None, :]
    ac = col[:, None] * inv[None, :]
    ang = jnp.concatenate([ar, ar, ac, ac], axis=-1)
    sign = jnp.tile(jnp.concatenate([-jnp.ones(ROPE_DIM // 2, dt), jnp.ones(ROPE_DIM // 2, dt)]), 2)
    reps = BR // HEAD_DIM
    return jnp.tile(jnp.cos(ang), (1, reps)), jnp.tile(jnp.sin(ang) * sign, (1, reps))


def _mixout_kernel(x_ref, mod_ref, u_ref, yf_ref, yb_ref, uv_ref, hy_ref, da_ref,
                   sd_ref, wg_ref, bg_ref, sn_ref, ws_ref, bs_ref, bn_ref, wo_ref, gf_ref, rt_ref,
                   xo_ref, h2_ref, cb_ref):
    tm = x_ref.shape[0]
    u = u_ref[...]
    y = jax.nn.gelu(sd_ref[...] * u + yf_ref[...] + yb_ref[...])
    y = y * jax.nn.sigmoid(_dot(y.astype(BF16), wg_ref[...]) + bg_ref[...])
    acc = _dot(_rms(y, bn_ref[0:1, :]).astype(BF16), wo_ref[0:BR, :])
    guv = jax.nn.gelu(uv_ref[...])
    gu = guv[:, 0:BR]
    gv = _rms(guv[:, BR:2 * BR], sn_ref[...]).astype(BF16)
    lane = lax.broadcasted_iota(jnp.int32, (SGU_CHUNK, BR), 1)
    hd = BR // SGU_HEADS
    zs = []
    for c in range(tm // SGU_CHUNK):
        vc = gv[c * SGU_CHUNK:(c + 1) * SGU_CHUNK, :]
        z = bs_ref[...]
        for h in range(SGU_HEADS):
            z = z + jnp.where(lane // hd == h, _dot(ws_ref[h], vc), 0.0)
        zs.append(z)
    z = zs[0] if len(zs) == 1 else jnp.concatenate(zs, axis=0)
    acc = acc + _dot(_rms(gu * z, bn_ref[1:2, :]).astype(BF16), wo_ref[BR:2 * BR, :])
    acc = acc + _dot(_rms(hy_ref[...], bn_ref[2:3, :]).astype(BF16), wo_ref[2 * BR:3 * BR, :])
    acc = acc + _dot(da_ref[...].astype(BF16), wo_ref[3 * BR:4 * BR, :])
    xn = x_ref[...] + mod_ref[:, 2 * D:3 * D] * acc
    xo_ref[...] = xn
    h2 = _rms(xn, gf_ref[...]) * (1.0 + mod_ref[:, 4 * D:5 * D]) + mod_ref[:, 3 * D:4 * D]
    h2_ref[...] = h2.astype(BF16)
    logits = _dot3(h2, rt_ref[...])
    el = lax.broadcasted_iota(jnp.int32, logits.shape, 1)
    logits = jnp.where(el < N_EXPERTS, logits, -jnp.inf)
    e = jnp.exp(logits - jnp.max(logits, axis=-1, keepdims=True))
    probs = e / jnp.sum(e, axis=-1, keepdims=True)
    big = logits.shape[1]
    m1 = jnp.max(probs, axis=-1, keepdims=True)
    i1 = jnp.min(jnp.where(probs == m1, el, big), axis=-1, keepdims=True)
    p2 = jnp.where((el == i1) | (el >= N_EXPERTS), -1.0, probs)
    m2 = jnp.max(p2, axis=-1, keepdims=True)
    i2 = jnp.min(jnp.where(p2 == m2, el, big), axis=-1, keepdims=True)
    tot = m1 + m2
    cb_ref[...] = jnp.where(el == i1, m1 / tot, 0.0) + jnp.where(el == i2, m2 / tot, 0.0)


def _mixout(x, modg, u_tm2, yf2, yb2, uv, hy, da, lp, B, L, tm):
    T = B * L
    nt = L // tm
    tok = lambda n: pl.BlockSpec((tm, n), lambda i: (i, 0))
    tmaj = pl.BlockSpec((tm, BR), lambda i: (i % nt, i // nt))
    const = lambda shp: pl.BlockSpec(shp, lambda i: (0,) * len(shp))
    return pl.pallas_call(
        _mixout_kernel, grid=(T // tm,),
        in_specs=[tok(D), _mod_spec(modg, tm, L),
                  tmaj, tmaj, tmaj, tok(2 * BR), tok(BR), tok(BR),
                  const((1, BR)), const((BR, BR)), const((1, BR)), const((1, BR)),
                  const((SGU_HEADS, SGU_CHUNK, SGU_CHUNK)), const((SGU_CHUNK, BR)),
                  const((3, BR)), const((D, D)), const((1, D)), const((D, 128))],
        out_specs=[tok(D), tok(D), tok(128)],
        out_shape=[jax.ShapeDtypeStruct((T, D), F32), jax.ShapeDtypeStruct((T, D), BF16),
                   jax.ShapeDtypeStruct((T, 128), F32)],
        compiler_params=_cparams("parallel"), name="mix_out",
    )(x, modg, u_tm2, yf2, yb2, uv, hy, da,
      lp['s5_d'], lp['s5_w_glu'], lp['s5_b_glu'], lp['sgu_norm'], lp['sgu_w_s'], lp['sgu_b'],
      lp['branch_norm'], lp['w_out'], lp['norm_ffn'], lp['router'])


def _ffn_kernel(h_ref, x_ref, mod_ref, w1_ref, w3_ref, w2_ref, o_ref, acc_ref):
    j = pl.program_id(1)
    h = h_ref[...]
    a = _dot(h, w1_ref[...])
    t = (a * jax.nn.sigmoid(a) * _dot(h, w3_ref[...])).astype(BF16)
    part = _dot(t, w2_ref[...])

    @pl.when(j == 0)
    def _():
        acc_ref[...] = part

    @pl.when(j > 0)
    def _():
        acc_ref[...] += part

    @pl.when(j == pl.num_programs(1) - 1)
    def _():
        o_ref[...] = x_ref[...] + mod_ref[:, 5 * D:6 * D] * acc_ref[...]


def _ffn(h2, x, modg, w1, w3, w2, L, tm, tf):
    T = x.shape[0]
    nt = L // tm
    dff = w1.shape[1]
    return pl.pallas_call(
        _ffn_kernel, grid=(T // tm, dff // tf),
        in_specs=[pl.BlockSpec((tm, D), lambda i, j: (i, 0)),
                  pl.BlockSpec((tm, D), lambda i, j: (i, 0)),
                  _mod_spec(modg, tm, L),
                  pl.BlockSpec((D, tf), lambda i, j: (0, j)),
                  pl.BlockSpec((D, tf), lambda i, j: (0, j)),
                  pl.BlockSpec((tf, D), lambda i, j: (j, 0))],
        out_specs=pl.BlockSpec((tm, D), lambda i, j: (i, 0)),
        out_shape=jax.ShapeDtypeStruct((T, D), F32),
        scratch_shapes=[pltpu.VMEM((tm, D), F32)],
        compiler_params=_cparams("parallel", "arbitrary"), name="ffn",
    )(h2, x, modg, w1, w3, w2)


def _route_kernel(cb_ref, rk_ref, rkt_ref, cnt_ref):
    tm = cb_ref.shape[0]
    mask = cb_ref[...] > 0.0
    mf = jnp.where(mask, 1.0, 0.0)
    r = lax.broadcasted_iota(jnp.int32, (tm, tm), 0)
    c = lax.broadcasted_iota(jnp.int32, (tm, tm), 1)
    before = jnp.where(c < r, 1.0, 0.0).astype(BF16)
    rank = jnp.where(mask, _dot(before, mf.astype(BF16)), -1.0)
    rk_ref[...] = rank
    rkt_ref[...] = rank.T[0:N_EXPERTS, :]
    cnt_ref[...] = jnp.sum(mf, axis=0, keepdims=True)


def _route(cb, tm):
    T = cb.shape[0]
    nt = T // tm
    rk, rkt, cnt = pl.pallas_call(
        _route_kernel, grid=(nt,),
        in_specs=[pl.BlockSpec((tm, 128), lambda i: (i, 0))],
        out_specs=[pl.BlockSpec((tm, 128), lambda i: (i, 0)),
                   pl.BlockSpec((None, N_EXPERTS, tm), lambda i: (i, 0, 0)),
                   pl.BlockSpec((None, 1, 128), lambda i: (i, 0, 0))],
        out_shape=[jax.ShapeDtypeStruct((T, 128), F32),
                   jax.ShapeDtypeStruct((nt, N_EXPERTS, tm), F32),
                   jax.ShapeDtypeStruct((nt, 1, 128), F32)],
        compiler_params=_cparams("parallel"), name="moe_route",
    )(cb)
    return rk, rkt, cnt[:, 0, :N_EXPERTS].astype(jnp.int32).reshape(-1)


MOE_BUCKETS = (128, 256, 320, 384, 512, 768, 1024)


def _moe_kernel(cnt_ref, h_ref, x_ref, mod_ref, cb_ref, rk_ref, rkt_ref, w1_ref, w3_ref, w2_ref,
                o_ref, xg_ref, y_ref):
    i, e, j = pl.program_id(0), pl.program_id(1), pl.program_id(2)
    ne, nj = pl.num_programs(1), pl.num_programs(2)
    tm = h_ref.shape[0]
    n = cnt_ref[i * ne + e]

    @pl.when((e == 0) & (j == 0))
    def _():
        o_ref[...] = jnp.zeros_like(o_ref)

    def step(M):
        @pl.when(j == 0)
        def _():
            rid = lax.broadcasted_iota(jnp.int32, (M, tm), 0).astype(F32)
            sel = jnp.where(rid == rkt_ref[pl.ds(e, 1), :], 1.0, 0.0).astype(BF16)
            xg_ref[0:M, :] = _dot(sel, h_ref[...]).astype(BF16)

        xg = xg_ref[0:M, :]
        a = _dot(xg, w1_ref[0])
        t = (a * jax.nn.sigmoid(a) * _dot(xg, w3_ref[0])).astype(BF16)
        part = _dot(t, w2_ref[0])

        @pl.when(j == 0)
        def _():
            y_ref[0:M, :] = part

        @pl.when(j > 0)
        def _():
            y_ref[0:M, :] += part

        @pl.when(j == nj - 1)
        def _():
            el = lax.broadcasted_iota(jnp.int32, (tm, 128), 1)
            rcol = jnp.sum(jnp.where(el == e, rk_ref[...], 0.0), axis=-1, keepdims=True)
            wcol = jnp.sum(jnp.where(el == e, cb_ref[...], 0.0), axis=-1, keepdims=True)
            cid = lax.broadcasted_iota(jnp.int32, (tm, M), 1).astype(F32)
            selt = jnp.where(cid == rcol, 1.0, 0.0).astype(BF16)
            y = y_ref[0:M, :]
            y_hi = y.astype(BF16)
            y_lo = (y - y_hi.astype(F32)).astype(BF16)
            o_ref[...] += wcol * (_dot(selt, y_hi) + _dot(selt, y_lo))

    lo = 0
    for M in MOE_BUCKETS:
        if M <= tm:
            pl.when((n > lo) & (n <= M))(functools.partial(step, M))
            lo = M

    @pl.when((e == ne - 1) & (j == nj - 1))
    def _():
        o_ref[...] = x_ref[...] + mod_ref[:, 5 * D:6 * D] * o_ref[...]


def _moe(h2, x, modg, cb, w1, w3, w2, L, tm, tf):
    T = x.shape[0]
    ne, _, dff = w1.shape
    rk, rkt, cnt = _route(cb, tm)
    nmod = modg.shape[0]
    grid_spec = pltpu.PrefetchScalarGridSpec(
        num_scalar_prefetch=1, grid=(T // tm, ne, dff // tf),
        in_specs=[pl.BlockSpec((tm, D), lambda i, e, j, c: (i, 0)),
                  pl.BlockSpec((tm, D), lambda i, e, j, c: (i, 0)),
                  pl.BlockSpec((None, 1, 6 * D), lambda i, e, j, c: ((i * tm // L) % nmod, 0, 0)),
                  pl.BlockSpec((tm, 128), lambda i, e, j, c: (i, 0)),
                  pl.BlockSpec((tm, 128), lambda i, e, j, c: (i, 0)),
                  pl.BlockSpec((None, N_EXPERTS, tm), lambda i, e, j, c: (i, 0, 0)),
                  pl.BlockSpec((1, D, tf), lambda i, e, j, c: (e, 0, j)),
                  pl.BlockSpec((1, D, tf), lambda i, e, j, c: (e, 0, j)),
                  pl.BlockSpec((1, tf, D), lambda i, e, j, c: (e, j, 0))],
        out_specs=pl.BlockSpec((tm, D), lambda i, e, j, c: (i, 0)),
        scratch_shapes=[pltpu.VMEM((tm, D), BF16), pltpu.VMEM((tm, D), F32)])
    return pl.pallas_call(
        _moe_kernel, grid_spec=grid_spec,
        out_shape=jax.ShapeDtypeStruct((T, D), F32),
        compiler_params=_cparams("parallel", "arbitrary", "arbitrary"), name="moe",
    )(cnt, h2, x, modg, cb, rk, rkt, w1, w3, w2)


def _final_kernel(x_ref, g_ref, o_ref):
    o_ref[...] = _rms(x_ref[...], g_ref[...])


def _final_norm(x, g, tm):
    T = x.shape[0]
    return pl.pallas_call(
        _final_kernel, grid=(T // tm,),
        in_specs=[pl.BlockSpec((tm, D), lambda i: (i, 0)), pl.BlockSpec((1, D), lambda i: (0, 0))],
        out_specs=pl.BlockSpec((tm, D), lambda i: (i, 0)),
        out_shape=jax.ShapeDtypeStruct((T, D), F32),
        compiler_params=_cparams("parallel"), name="final_norm",
    )(x, g.reshape(1, D))


def _layer(x, modg, lp, l, B, L, ctx):
    is_ctx = ctx is None
    T = B * L
    Bp = -(-B // 8) * 8
    tm = min(L, 512)
    rope_tabs = None if is_ctx else _rope_tables(L, F32)
    u_tm, uv, z, q, k, v = _inproj(x, modg, lp['norm_mix'], lp['w_in'], B, L, tm, rope_tabs)

    u3 = jnp.pad(u_tm.reshape(L, B, BR), ((0, 0), (0, Bp - B), (0, 0)))
    if is_ctx:
        h0 = jnp.zeros((2, Bp, 2 * S5_STATES), F32)
    else:
        h0 = ctx[2]
        h0 = jnp.transpose(h0, (1, 0, 4, 2, 3)).reshape(2, B, 2 * S5_STATES)
        h0 = jnp.pad(h0, ((0, 0), (0, Bp - B), (0, 0)))
    yf, yb, hfin = _s5(u3, h0, lp['s5_wb'], lp['s5_wc'], lp['s5_a'], L, Bp, min(L, 128))

    hv, hx1, hx2 = _short_conv(z, lp['hy_conv_w'], lp['hy_conv_b'], L, tm)
    consts = lp['hy_consts_ctx'] if is_ctx else lp['hy_consts_lat']
    hf = lp['hy_spec_ctx'] if is_ctx else lp['hy_spec_lat']
    y1 = _hy_conv(hv, hx1, lp['hy_bias'][0], hf[0], consts, L)
    y_hy = _hy_conv(y1, hx2, lp['hy_bias'][1], hf[1], consts, L)

    q3 = q.reshape(B, L, BR)
    k3 = k.reshape(B, L, BR)
    v3 = v.reshape(B, L, BR)
    kt, vb = _attn_inputs(k3, v3, None if is_ctx else ctx)
    da = _attn(lp['da_lam'], q3, kt, vb, lp['da_subln'], min(L, 256), 1.0 - lp['lam_init'])

    x, h2, cb = _mixout(x, modg, u_tm, yf.reshape(L, Bp * BR), yb.reshape(L, Bp * BR), uv,
                        y_hy, da.reshape(T, BR), lp, B, L, tm)
    tmf = min(T, 1024)
    if l % 2 == 0:
        x = _ffn(h2, x, modg, lp['ffn_w1'], lp['ffn_w3'], lp['ffn_w2'], L, tmf,
                 lp['ffn_w1'].shape[1] // 2)
    else:
        x = _moe(h2, x, modg, cb, lp['moe_w1'], lp['moe_w3'], lp['moe_w2'], L, tmf, 896)
    if is_ctx:
        fin = hfin[:, :B].reshape(2, B, 2, S5_G, S5_N)
        fin = jnp.transpose(fin, (1, 0, 3, 4, 2))
        return x, k3.reshape(B, L, HEADS, 2 * HEAD_DIM), v3.reshape(B, L, HEADS, VAL_DIM), fin
    return x


def kernel(x_prompt, x_sample, cache_k, cache_v, state_ssm, c, c_ctx, w_ada, b_ada, norm_mix, norm_ffn, w_in, w_out, branch_norm, s5_lam_re, s5_lam_im, s5_log_dt, s5_b_re, s5_b_im, s5_c_re, s5_c_im, s5_d, s5_w_glu, s5_b_glu, sgu_norm, sgu_w_s, sgu_b_s, hy_conv_w, hy_conv_b, hy_w1, hy_b1, hy_w2, hy_b2, hy_w3, hy_bias, da_lq1, da_lk1, da_lq2, da_lk2, da_subln, ffn_w1, ffn_w3, ffn_w2, moe_router, moe_w1, moe_w3, moe_w2, norm_final):
    depth = w_in.shape[0]
    Bc, Lc, _ = x_prompt.shape
    Bs, Ls, _ = x_sample.shape

    cond = jnp.concatenate([c_ctx[None, :], c], axis=0)
    cond8 = jnp.pad(cond, ((0, 8 - cond.shape[0]), (0, 0)))
    mod = _ada(cond8, w_ada, b_ada)

    cc = _hy_consts(Lc)
    cl = _hy_consts(Ls)

    layers = []
    for l in range(depth):
        j = l // 2
        wb, wc, a = _s5_prep(s5_lam_re[l], s5_lam_im[l], s5_log_dt[l], s5_b_re[l], s5_b_im[l],
                             s5_c_re[l], s5_c_im[l])
        lam_init = 0.8 - 0.6 * math.exp(-0.3 * l)
        lam = (jnp.exp(jnp.sum(da_lq1[l] * da_lk1[l])) - jnp.exp(jnp.sum(da_lq2[l] * da_lk2[l]))
               + lam_init)
        hy_args = (hy_w1[l], hy_b1[l], hy_w2[l], hy_b2[l], hy_w3[l])
        lp = dict(
            norm_mix=norm_mix[l], w_in=w_in[l].astype(BF16), w_out=w_out[l].astype(BF16),
            norm_ffn=norm_ffn[l].reshape(1, D), branch_norm=branch_norm[l],
            s5_wb=wb, s5_wc=wc, s5_a=a, s5_d=s5_d[l].reshape(1, BR),
            s5_w_glu=s5_w_glu[l].astype(BF16), s5_b_glu=s5_b_glu[l].reshape(1, BR),
            sgu_norm=sgu_norm[l].reshape(1, BR), sgu_w_s=sgu_w_s[l].astype(BF16),
            sgu_b=jnp.repeat(sgu_b_s[l].T, BR // SGU_HEADS, axis=1),
            hy_conv_w=hy_conv_w[l], hy_conv_b=hy_conv_b[l], hy_bias=hy_bias[l],
            hy_consts_ctx=cc, hy_consts_lat=cl,
            hy_spec_ctx=_hy_filter_spectrum(Lc, *hy_args, cc[4]),
            hy_spec_lat=_hy_filter_spectrum(Ls, *hy_args, cl[4]),
            da_lam=lam, lam_init=lam_init, da_subln=da_subln[l],
        )
        if l % 2 == 0:
            lp.update(ffn_w1=ffn_w1[j].astype(BF16), ffn_w3=ffn_w3[j].astype(BF16),
                      ffn_w2=ffn_w2[j].astype(BF16),
                      router=jnp.zeros((D, 128), F32))
        else:
            lp.update(moe_w1=moe_w1[j].astype(BF16), moe_w3=moe_w3[j].astype(BF16),
                      moe_w2=moe_w2[j].astype(BF16),
                      router=jnp.pad(moe_router[j], ((0, 0), (0, 128 - N_EXPERTS))))
        layers.append(lp)

    xc = x_prompt.reshape(Bc * Lc, D)
    ks, vs, ss = [], [], []
    for l in range(depth):
        xc, k_l, v_l, s_l = _layer(xc, mod[l, 0:1].reshape(1, 1, 6 * D), layers[l], l, Bc, Lc, None)
        ks.append(k_l)
        vs.append(v_l)
        ss.append(s_l)
    y_prompt = _final_norm(xc, norm_final, 512).reshape(Bc, Lc, D)

    xs = x_sample.reshape(Bs * Ls, D)
    for l in range(depth):
        xs = _layer(xs, mod[l, 1:1 + Bs].reshape(Bs, 1, 6 * D), layers[l], l, Bs, Ls,
                    (cache_k[:, l], cache_v[:, l], state_ssm[:, l]))
    y_sample = _final_norm(xs, norm_final, 512).reshape(Bs, Ls, D)
    return (y_prompt, y_sample, jnp.stack(ks, axis=1), jnp.stack(vs, axis=1), jnp.stack(ss, axis=1))
```

```python
import functools
import math

import jax
import jax.numpy as jnp
import numpy as np
from jax import lax
from jax.experimental import pallas as pl
from jax.experimental.pallas import tpu as pltpu

F32 = jnp.float32
BF16 = jnp.bfloat16

D = 1024
BR = 256
PROJ = 9 * BR
S5_G, S5_N, S5_P = 16, 64, 16
S5_STATES = S5_G * S5_N
SGU_CHUNK, SGU_HEADS = 128, 4
HEADS, HEAD_DIM, VAL_DIM = 4, 32, 64
GRID_WIDTH = 64
ROPE_DIM = HEAD_DIM // 2
ROPE_BASE = 10000.0
HY_EMB, HY_BANDS = 33, 16
HY_DECAY_TARGET, HY_FAST, HY_SLOW = 1e-2, 0.3, 1.5
N_EXPERTS = 8
EPS = 1e-6
LOG2E = 1.4426950408889634

DFT_N2 = 128
DFT_ROWS = 32
VMEM_LIMIT = 56 * 1024 * 1024


def _cparams(*sem):
    return pltpu.CompilerParams(dimension_semantics=sem, vmem_limit_bytes=VMEM_LIMIT)


def _dot(a, b):
    return jnp.dot(a, b, preferred_element_type=F32)


def _dot3(a, b):
    a_hi = a.astype(BF16)
    b_hi = b.astype(BF16)
    a_lo = (a - a_hi.astype(F32)).astype(BF16)
    b_lo = (b - b_hi.astype(F32)).astype(BF16)
    return _dot(a_hi, b_hi) + (_dot(a_hi, b_lo) + _dot(a_lo, b_hi))


def _rms(x, g):
    return x * lax.rsqrt(jnp.mean(x * x, axis=-1, keepdims=True) + EPS) * g


def _mod_spec(modg, tm, L):
    nmod = modg.shape[0]
    return pl.BlockSpec((None, 1, 6 * D), lambda i, *_: ((i * tm // L) % nmod, 0, 0))


def _ada_kernel(c_ref, w_ref, b_ref, o_ref):
    c = c_ref[...]
    s = (c * jax.nn.sigmoid(c)).astype(BF16)
    o_ref[0] = _dot(s, w_ref[0].astype(BF16)) + b_ref[0]


def _ada(cond8, w_ada, b_ada):
    depth = w_ada.shape[0]
    tn = 1536
    return pl.pallas_call(
        _ada_kernel,
        grid=(depth, 6 * D // tn),
        in_specs=[pl.BlockSpec((8, D), lambda l, j: (0, 0)),
                  pl.BlockSpec((1, D, tn), lambda l, j: (l, 0, j)),
                  pl.BlockSpec((1, 1, tn), lambda l, j: (l, 0, j))],
        out_specs=pl.BlockSpec((1, 8, tn), lambda l, j: (l, 0, j)),
        out_shape=jax.ShapeDtypeStruct((depth, 8, 6 * D), F32),
        compiler_params=_cparams("parallel", "parallel"),
        name="ada",
    )(cond8, w_ada, b_ada.reshape(depth, 1, 6 * D))


def _inproj_kernel(*refs, rope):
    if rope:
        (x_ref, mod_ref, g_ref, w_ref, cos_ref, sin_ref,
         u_ref, uv_ref, z_ref, q_ref, k_ref, v_ref) = refs
    else:
        x_ref, mod_ref, g_ref, w_ref, u_ref, uv_ref, z_ref, q_ref, k_ref, v_ref = refs
    x = x_ref[...]
    h = _rms(x, g_ref[...]) * (1.0 + mod_ref[:, D:2 * D]) + mod_ref[:, 0:D]
    p = _dot(h.astype(BF16), w_ref[...])
    u_ref[...] = p[:, 0:BR]
    uv_ref[...] = p[:, BR:3 * BR]
    z_ref[...] = p[:, 3 * BR:6 * BR]
    q = p[:, 6 * BR:7 * BR]
    k = p[:, 7 * BR:8 * BR]
    if rope:
        cs = cos_ref[...]
        sn = sin_ref[...]
        lane = lax.broadcasted_iota(jnp.int32, q.shape, 1)
        first = (lane % (2 * (ROPE_DIM // 2))) < (ROPE_DIM // 2)
        half = ROPE_DIM // 2

        def rot(t):
            return jnp.where(first, pltpu.roll(t, BR - half, 1), pltpu.roll(t, half, 1))

        q = q * cs + rot(q) * sn
        k = k * cs + rot(k) * sn
    q_ref[...] = q
    k_ref[...] = k
    v_ref[...] = p[:, 8 * BR:9 * BR]


def _inproj(x, modg, g, w_bf, B, L, tm, rope_tabs):
    T = B * L
    nt = L // tm
    rope = rope_tabs is not None
    in_specs = [pl.BlockSpec((tm, D), lambda i: (i, 0)),
                _mod_spec(modg, tm, L),
                pl.BlockSpec((1, D), lambda i: (0, 0)),
                pl.BlockSpec((D, PROJ), lambda i: (0, 0))]
    args = [x, modg, g.reshape(1, D), w_bf]
    if rope:
        in_specs += [pl.BlockSpec((tm, BR), lambda i: (i % nt, 0))] * 2
        args += list(rope_tabs)
    tok = lambda n: pl.BlockSpec((tm, n), lambda i: (i, 0))
    out_specs = [pl.BlockSpec((tm, BR), lambda i: (i % nt, i // nt)),
                 tok(2 * BR), tok(3 * BR), tok(BR), tok(BR), tok(BR)]
    out_shape = [jax.ShapeDtypeStruct((L, B * BR), F32),
                 jax.ShapeDtypeStruct((T, 2 * BR), F32),
                 jax.ShapeDtypeStruct((T, 3 * BR), F32),
                 jax.ShapeDtypeStruct((T, BR), F32),
                 jax.ShapeDtypeStruct((T, BR), F32),
                 jax.ShapeDtypeStruct((T, BR), F32)]
    return pl.pallas_call(
        functools.partial(_inproj_kernel, rope=rope),
        grid=(T // tm,), in_specs=in_specs, out_specs=out_specs, out_shape=out_shape,
        compiler_params=_cparams("parallel"), name="inproj",
    )(*args)


def _s5_kernel(uf_ref, ub_ref, h0_ref, wb_ref, wc_ref, a_ref, yf_ref, yb_ref, hfin_ref,
               xs_ref, hc_ref, *, tt):
    i = pl.program_id(1)
    last = pl.num_programs(1) - 1
    half = S5_STATES // 2

    @pl.when(i == 0)
    def _():
        hc_ref[...] = h0_ref[...]

    for d in range(2):
        u_ref = uf_ref if d == 0 else ub_ref
        y_ref = yf_ref if d == 0 else yb_ref
        u2 = u_ref[...].reshape(tt * 8, BR).astype(BF16)
        xs_ref[...] = _dot(u2, wb_ref[d])
        for c in range(2):
            cr = slice(c * half, (c + 1) * half)
            ci = slice(S5_STATES + c * half, S5_STATES + (c + 1) * half)
            ar = jnp.broadcast_to(a_ref[d, 0:1, cr], (8, half))
            ai = jnp.broadcast_to(a_ref[d, 1:2, cr], (8, half))

            def body(s, carry, cr=cr, ci=ci, ar=ar, ai=ai, d=d):
                hr, hi = carry
                t = s if d == 0 else tt - 1 - s
                r0 = pl.multiple_of(t * 8, 8)
                nr = ar * hr - ai * hi + xs_ref[pl.ds(r0, 8), cr]
                ni = ar * hi + ai * hr + xs_ref[pl.ds(r0, 8), ci]
                xs_ref[pl.ds(r0, 8), cr] = nr
                xs_ref[pl.ds(r0, 8), ci] = ni
                return nr, ni

            hr, hi = lax.fori_loop(0, tt, body, (hc_ref[d, :, cr], hc_ref[d, :, ci]))
            hc_ref[d, :, cr] = hr
            hc_ref[d, :, ci] = hi
        y = _dot(xs_ref[...].astype(BF16), wc_ref[d])
        y_ref[...] = y.reshape(tt, 8, BR)

    @pl.when(i == last)
    def _():
        hfin_ref[...] = hc_ref[...]


def _s5(u_tm, h0, wb, wc, a, L, Bp, tt):
    nT = L // tt
    ng = Bp // 8
    blk = lambda f: pl.BlockSpec((tt, 8, BR), f)
    const = lambda shp: pl.BlockSpec(shp, lambda g, i: (0,) * len(shp))
    return pl.pallas_call(
        functools.partial(_s5_kernel, tt=tt),
        grid=(ng, nT),
        in_specs=[blk(lambda g, i: (i, g, 0)), blk(lambda g, i: (nT - 1 - i, g, 0)),
                  pl.BlockSpec((2, 8, 2 * S5_STATES), lambda g, i: (0, g, 0)),
                  const((2, BR, 2 * S5_STATES)), const((2, 2 * S5_STATES, BR)),
                  const((2, 2, S5_STATES))],
        out_specs=[blk(lambda g, i: (i, g, 0)), blk(lambda g, i: (nT - 1 - i, g, 0)),
                   pl.BlockSpec((2, 8, 2 * S5_STATES), lambda g, i: (0, g, 0))],
        out_shape=[jax.ShapeDtypeStruct((L, Bp, BR), F32),
                   jax.ShapeDtypeStruct((L, Bp, BR), F32),
                   jax.ShapeDtypeStruct((2, Bp, 2 * S5_STATES), F32)],
        scratch_shapes=[pltpu.VMEM((tt * 8, 2 * S5_STATES), F32),
                        pltpu.VMEM((2, 8, 2 * S5_STATES), F32)],
        compiler_params=_cparams("parallel", "arbitrary"), name="s5",
    )(u_tm, u_tm, h0, wb, wc, a)


def _s5_prep(lam_re, lam_im, log_dt, b_re, b_im, c_re, c_im):
    dt = jnp.exp(log_dt)[..., None]
    mag = jnp.exp(lam_re * dt)
    lb_re = mag * jnp.cos(lam_im * dt)
    lb_im = mag * jnp.sin(lam_im * dt)
    den = lam_re * lam_re + lam_im * lam_im
    nr = lb_re - 1.0
    coef_re = ((nr * lam_re + lb_im * lam_im) / den)[..., None]
    coef_im = ((lb_im * lam_re - nr * lam_im) / den)[..., None]
    bp_re = coef_re * b_re - coef_im * b_im
    bp_im = coef_re * b_im + coef_im * b_re
    eye = jnp.eye(S5_G, dtype=lam_re.dtype)

    def blockdiag_in(b):
        return jnp.einsum('dgnp,gh->dgphn', b, eye).reshape(2, BR, S5_STATES)

    def blockdiag_out(c):
        return jnp.einsum('dgpn,gh->dgnhp', c, eye).reshape(2, S5_STATES, BR)

    wb = jnp.concatenate([blockdiag_in(bp_re), blockdiag_in(bp_im)], axis=-1)
    wc = jnp.concatenate([blockdiag_out(c_re), -blockdiag_out(c_im)], axis=1)
    a = jnp.stack([lb_re.reshape(2, S5_STATES), lb_im.reshape(2, S5_STATES)], axis=1)
    return wb.astype(BF16), wc.astype(BF16), a


def _short_kernel(z_ref, zp_ref, zn_ref, w_ref, b_ref, v_ref, x1_ref, x2_ref, *, nt):
    j = pl.program_id(0) % nt
    z = z_ref[...]
    tm = z.shape[0]
    row = lax.broadcasted_iota(jnp.int32, z.shape, 0)
    prev_row = jnp.where(j > 0, zp_ref[7:8, :], 0.0)
    next_row = jnp.where(j < nt - 1, zn_ref[0:1, :], 0.0)
    zprev = jnp.where(row == 0, prev_row, pltpu.roll(z, 1, 0))
    znext = jnp.where(row == tm - 1, next_row, pltpu.roll(z, tm - 1, 0))
    y = zprev * w_ref[0:1, :] + z * w_ref[1:2, :] + znext * w_ref[2:3, :] + b_ref[...]
    v_ref[...] = y[:, 0:BR]
    x1_ref[...] = y[:, BR:2 * BR]
    x2_ref[...] = y[:, 2 * BR:3 * BR]


def _short_conv(z, w, b, L, tm):
    T = z.shape[0]
    nt = L // tm
    r8 = tm // 8
    nblk8 = T // 8
    out = jax.ShapeDtypeStruct((T, BR), F32)
    return pl.pallas_call(
        functools.partial(_short_kernel, nt=nt),
        grid=(T // tm,),
        in_specs=[pl.BlockSpec((tm, 3 * BR), lambda i: (i, 0)),
                  pl.BlockSpec((8, 3 * BR), lambda i: (jnp.maximum(i * r8 - 1, 0), 0)),
                  pl.BlockSpec((8, 3 * BR), lambda i: (jnp.minimum((i + 1) * r8, nblk8 - 1), 0)),
                  pl.BlockSpec((3, 3 * BR), lambda i: (0, 0)),
                  pl.BlockSpec((1, 3 * BR), lambda i: (0, 0))],
        out_specs=[pl.BlockSpec((tm, BR), lambda i: (i, 0))] * 3,
        out_shape=[out, out, out],
        compiler_params=_cparams("parallel"), name="short_conv",
    )(z, z, z, w, b.reshape(1, 3 * BR))


def _hy_consts(L):
    n = 2 * L
    n2 = DFT_N2
    n1 = n // n2
    r = L // n2
    s = DFT_ROWS // r

    def cs(num, den):
        th = (2.0 * math.pi / den) * (num % den).astype(F32)
        return jnp.cos(th), jnp.sin(th)

    ar = lambda m: jnp.arange(m, dtype=jnp.int32)
    eye = jnp.eye(s, dtype=F32)
    c, sn = cs(ar(n1)[:, None] * ar(r)[None, :], n1)
    f1 = jnp.stack([c, -sn], axis=1).reshape(2 * n1, r)
    m1 = jnp.kron(eye, f1)
    c, sn = cs((ar(r)[:, None] + r // 2) * ar(n1)[None, :], n1)
    g1 = jnp.stack([c, -sn], axis=2).reshape(r, 2 * n1) * (1.0 / n)
    gm = jnp.kron(eye, g1)
    kk1 = ar(n1)[:, None, None]
    kk2 = ar(n2)[None, :, None]
    jj2 = ar(n2)[None, None, :]
    c, sn = cs(jj2 * kk2 * n1 + jj2 * kk1, n)
    w2f = jnp.concatenate([jnp.concatenate([c, sn], axis=2),
                           jnp.concatenate([-sn, c], axis=2)], axis=1)
    ct, st = jnp.swapaxes(c, 1, 2), jnp.swapaxes(sn, 1, 2)
    w2i = jnp.concatenate([jnp.concatenate([ct, -st], axis=2),
                           jnp.concatenate([st, ct], axis=2)], axis=1)
    return m1.astype(BF16), gm.astype(BF16), w2f.astype(BF16), w2i.astype(BF16), n1


def _hyA_kernel(x_ref, m_ref, o_ref):
    o_ref[...] = _dot(m_ref[...], x_ref[...].astype(BF16)).astype(BF16)


def _hyB_kernel(a_ref, wf_ref, wi_ref, hf_ref, o_ref, *, rb):
    n2 = DFT_N2
    for j in range(rb):
        a = a_ref[j].reshape(2 * n2, BR)
        x = _dot(wf_ref[j], a)
        xr, xi = x[0:n2], x[n2:2 * n2]
        hr, hi = hf_ref[j, 0], hf_ref[j, 1]
        y = jnp.concatenate([xr * hr - xi * hi, xr * hi + xi * hr], axis=0).astype(BF16)
        o_ref[j] = _dot(wi_ref[j], y).astype(BF16).reshape(2, n2, BR)


def _hyC_kernel(a_ref, g_ref, v_ref, xg_ref, bias_ref, o_ref):
    y = _dot(g_ref[...], a_ref[...])
    o_ref[...] = xg_ref[...] * (y + v_ref[...] * bias_ref[...])


def _hy_conv(v, xg, bias, hf, consts, L):
    m1, gm, w2f, w2i, n1 = consts
    T = v.shape[0]
    n2 = DFT_N2
    rows = T // n2
    cols = n2 * BR
    tc = 4096
    nrb = rows // DFT_ROWS
    v2 = v.reshape(rows, cols)
    x2 = xg.reshape(rows, cols)
    a = pl.pallas_call(
        _hyA_kernel, grid=(nrb, cols // tc),
        in_specs=[pl.BlockSpec((DFT_ROWS, tc), lambda i, j: (i, j)),
                  pl.BlockSpec((4 * DFT_ROWS, DFT_ROWS), lambda i, j: (0, 0))],
        out_specs=pl.BlockSpec((4 * DFT_ROWS, tc), lambda i, j: (i, j)),
        out_shape=jax.ShapeDtypeStruct((4 * rows, cols), BF16),
        compiler_params=_cparams("parallel", "parallel"), name="hy_stage1",
    )(v2, m1)
    R = 2 * rows
    rb = min(8, n1)
    nkb = n1 // rb
    nseq = R // n1
    a5 = a.reshape(R, 2, n2, BR)
    b5 = pl.pallas_call(
        functools.partial(_hyB_kernel, rb=rb), grid=(nkb, nseq),
        in_specs=[pl.BlockSpec((rb, 2, n2, BR), lambda kb, s: (s * nkb + kb, 0, 0, 0)),
                  pl.BlockSpec((rb, 2 * n2, 2 * n2), lambda kb, s: (kb, 0, 0)),
                  pl.BlockSpec((rb, 2 * n2, 2 * n2), lambda kb, s: (kb, 0, 0)),
                  pl.BlockSpec((rb, 2, n2, BR), lambda kb, s: (kb, 0, 0, 0))],
        out_specs=pl.BlockSpec((rb, 2, n2, BR), lambda kb, s: (s * nkb + kb, 0, 0, 0)),
        out_shape=jax.ShapeDtypeStruct((R, 2, n2, BR), BF16),
        compiler_params=_cparams("parallel", "parallel"), name="hy_stage2",
    )(a5, w2f, w2i, hf)
    bt = jnp.tile(bias.reshape(1, BR), (1, tc // BR))
    y = pl.pallas_call(
        _hyC_kernel, grid=(nrb, cols // tc),
        in_specs=[pl.BlockSpec((4 * DFT_ROWS, tc), lambda i, j: (i, j)),
                  pl.BlockSpec((DFT_ROWS, 4 * DFT_ROWS), lambda i, j: (0, 0)),
                  pl.BlockSpec((DFT_ROWS, tc), lambda i, j: (i, j)),
                  pl.BlockSpec((DFT_ROWS, tc), lambda i, j: (i, j)),
                  pl.BlockSpec((1, tc), lambda i, j: (0, 0))],
        out_specs=pl.BlockSpec((DFT_ROWS, tc), lambda i, j: (i, j)),
        out_shape=jax.ShapeDtypeStruct((rows, cols), F32),
        compiler_params=_cparams("parallel", "parallel"), name="hy_stage3",
    )(b5.reshape(4 * rows, cols), gm, v2, x2, bt)
    return y.reshape(T, BR)


def _hy_filter_spectrum(L, w1, b1, w2, b2, w3, n1):
    dt = w1.dtype
    pos = jnp.arange(L, dtype=dt)
    t01 = jnp.linspace(0.0, 1.0, L, dtype=dt)[:, None]
    w = (2.0 * math.pi / L) * pos[:, None]
    bands = jnp.linspace(1e-4, HY_BANDS - 1, HY_BANDS, dtype=dt)[None, :]
    feats = jnp.concatenate([t01, jnp.cos(bands * w), -jnp.sin(bands * w)], axis=-1)
    h = jnp.sin(feats @ w1 + b1)
    h = jnp.sin(h @ w2 + b2)
    h = h @ w3
    dist = jnp.abs(pos - (L // 2)) / L
    decay = jnp.abs(jnp.linspace(math.log(HY_DECAY_TARGET) / HY_SLOW,
                                 math.log(HY_DECAY_TARGET) / HY_FAST, 2 * BR, dtype=dt))
    filt = h * jnp.exp(-dist[:, None] * decay[None, :])
    spec = jnp.fft.fft(filt, n=2 * L, axis=0)
    spec = spec.reshape(DFT_N2, n1, 2, BR).transpose(2, 1, 0, 3)
    return jnp.stack([spec.real, spec.imag], axis=2).astype(dt)


def _attn_kernel(lam_ref, q_ref, kt_ref, v_ref, sub_ref, o_ref, *, qscale, post):
    q = (q_ref[0] * qscale).astype(BF16)
    lam = lam_ref[0, 0]
    sub = sub_ref[...]
    for h in range(HEADS):
        vh = v_ref[0, :, h * VAL_DIM:(h + 1) * VAL_DIM]
        res = []
        for m in range(2):
            c0 = h * 2 * HEAD_DIM + m * HEAD_DIM
            s = _dot(q[:, c0:c0 + HEAD_DIM], kt_ref[0, c0:c0 + HEAD_DIM, :])
            p = jnp.exp2(s - jnp.max(s, axis=-1, keepdims=True))
            l = jnp.sum(p, axis=-1, keepdims=True)
            res.append(_dot(p.astype(BF16), vh) / l)
        o = res[0] - lam * res[1]
        o = o * lax.rsqrt(jnp.mean(o * o, axis=-1, keepdims=True) + EPS) * sub * post
        o_ref[0, :, h * VAL_DIM:(h + 1) * VAL_DIM] = o


def _attn(lam, q, kt, v, subln, tq, post):
    B, L, _ = q.shape
    Lk = kt.shape[2]
    return pl.pallas_call(
        functools.partial(_attn_kernel, qscale=HEAD_DIM ** -0.5 * LOG2E, post=post),
        grid=(B, L // tq),
        in_specs=[pl.BlockSpec(memory_space=pltpu.SMEM),
                  pl.BlockSpec((1, tq, BR), lambda b, i: (b, i, 0)),
                  pl.BlockSpec((1, BR, Lk), lambda b, i: (b, 0, 0)),
                  pl.BlockSpec((1, Lk, BR), lambda b, i: (b, 0, 0)),
                  pl.BlockSpec((1, VAL_DIM), lambda b, i: (0, 0))],
        out_specs=pl.BlockSpec((1, tq, BR), lambda b, i: (b, i, 0)),
        out_shape=jax.ShapeDtypeStruct((B, L, BR), F32),
        compiler_params=_cparams("parallel", "parallel"), name="diff_attn",
    )(lam.reshape(1, 1), q, kt, v, subln.reshape(1, VAL_DIM))


def _attn_inputs(k3, v3, ctx):
    B = k3.shape[0]
    if ctx is not None:
        k3 = jnp.concatenate([ctx[0].reshape(B, -1, BR), k3], axis=1)
        v3 = jnp.concatenate([ctx[1].reshape(B, -1, BR), v3], axis=1)
    return jnp.swapaxes(k3, 1, 2).astype(BF16), v3.astype(BF16)


def _attn_latent_test(q, k, v, ck, cv):
    kt, vb = _attn_inputs(k, v, (ck, cv))
    return _attn(jnp.float32(0.5), q, kt, vb, jnp.ones((VAL_DIM,), F32), 256, 0.5)


def _rope_tables(L, dt):
    n_rows = L // GRID_WIDTH
    row = jnp.repeat(jnp.arange(n_rows, dtype=dt), GRID_WIDTH)
    col = (jnp.arange(L) % GRID_WIDTH).astype(dt)
    inv = ROPE_BASE ** (-jnp.arange(0, ROPE_DIM, 2, dtype=dt) / ROPE_DIM)
    ar = row[:, None] * inv[None, :]
    ac = col[:, None] * inv[None, :]
    ang = jnp.concatenate([ar, ar, ac, ac], axis=-1)
    sign = jnp.tile(jnp.concatenate([-jnp.ones(ROPE_DIM // 2, dt), jnp.ones(ROPE_DIM // 2, dt)]), 2)
    reps = BR // HEAD_DIM
    return jnp.tile(jnp.cos(ang), (1, reps)), jnp.tile(jnp.sin(ang) * sign, (1, reps))


def _mixout_kernel(x_ref, mod_ref, u_ref, yf_ref, yb_ref, uv_ref, hy_ref, da_ref,
                   sd_ref, wg_ref, bg_ref, sn_ref, ws_ref, bs_ref, bn_ref, wo_ref, gf_ref, rt_ref,
                   xo_ref, h2_ref, cb_ref):
    tm = x_ref.shape[0]
    u = u_ref[...]
    y = jax.nn.gelu(sd_ref[...] * u + yf_ref[...] + yb_ref[...])
    y = y * jax.nn.sigmoid(_dot(y.astype(BF16), wg_ref[...]) + bg_ref[...])
    acc = _dot(_rms(y, bn_ref[0:1, :]).astype(BF16), wo_ref[0:BR, :])
    guv = jax.nn.gelu(uv_ref[...])
    gu = guv[:, 0:BR]
    gv = _rms(guv[:, BR:2 * BR], sn_ref[...]).astype(BF16)
    lane = lax.broadcasted_iota(jnp.int32, (SGU_CHUNK, BR), 1)
    hd = BR // SGU_HEADS
    zs = []
    for c in range(tm // SGU_CHUNK):
        vc = gv[c * SGU_CHUNK:(c + 1) * SGU_CHUNK, :]
        z = bs_ref[...]
        for h in range(SGU_HEADS):
            z = z + jnp.where(lane // hd == h, _dot(ws_ref[h], vc), 0.0)
        zs.append(z)
    z = zs[0] if len(zs) == 1 else jnp.concatenate(zs, axis=0)
    acc = acc + _dot(_rms(gu * z, bn_ref[1:2, :]).astype(BF16), wo_ref[BR:2 * BR, :])
    acc = acc + _dot(_rms(hy_ref[...], bn_ref[2:3, :]).astype(BF16), wo_ref[2 * BR:3 * BR, :])
    acc = acc + _dot(da_ref[...].astype(BF16), wo_ref[3 * BR:4 * BR, :])
    xn = x_ref[...] + mod_ref[:, 2 * D:3 * D] * acc
    xo_ref[...] = xn
    h2 = _rms(xn, gf_ref[...]) * (1.0 + mod_ref[:, 4 * D:5 * D]) + mod_ref[:, 3 * D:4 * D]
    h2_ref[...] = h2.astype(BF16)
    logits = _dot3(h2, rt_ref[...])
    el = lax.broadcasted_iota(jnp.int32, logits.shape, 1)
    logits = jnp.where(el < N_EXPERTS, logits, -jnp.inf)
    e = jnp.exp(logits - jnp.max(logits, axis=-1, keepdims=True))
    probs = e / jnp.sum(e, axis=-1, keepdims=True)
    big = logits.shape[1]
    m1 = jnp.max(probs, axis=-1, keepdims=True)
    i1 = jnp.min(jnp.where(probs == m1, el, big), axis=-1, keepdims=True)
    p2 = jnp.where((el == i1) | (el >= N_EXPERTS), -1.0, probs)
    m2 = jnp.max(p2, axis=-1, keepdims=True)
    i2 = jnp.min(jnp.where(p2 == m2, el, big), axis=-1, keepdims=True)
    tot = m1 + m2
    cb_ref[...] = jnp.where(el == i1, m1 / tot, 0.0) + jnp.where(el == i2, m2 / tot, 0.0)


def _mixout(x, modg, u_tm2, yf2, yb2, uv, hy, da, lp, B, L, tm):
    T = B * L
    nt = L // tm
    tok = lambda n: pl.BlockSpec((tm, n), lambda i: (i, 0))
    tmaj = pl.BlockSpec((tm, BR), lambda i: (i % nt, i // nt))
    const = lambda shp: pl.BlockSpec(shp, lambda i: (0,) * len(shp))
    return pl.pallas_call(
        _mixout_kernel, grid=(T // tm,),
        in_specs=[tok(D), _mod_spec(modg, tm, L),
                  tmaj, tmaj, tmaj, tok(2 * BR), tok(BR), tok(BR),
                  const((1, BR)), const((BR, BR)), const((1, BR)), const((1, BR)),
                  const((SGU_HEADS, SGU_CHUNK, SGU_CHUNK)), const((SGU_CHUNK, BR)),
                  const((3, BR)), const((D, D)), const((1, D)), const((D, 128))],
        out_specs=[tok(D), tok(D), tok(128)],
        out_shape=[jax.ShapeDtypeStruct((T, D), F32), jax.ShapeDtypeStruct((T, D), BF16),
                   jax.ShapeDtypeStruct((T, 128), F32)],
        compiler_params=_cparams("parallel"), name="mix_out",
    )(x, modg, u_tm2, yf2, yb2, uv, hy, da,
      lp['s5_d'], lp['s5_w_glu'], lp['s5_b_glu'], lp['sgu_norm'], lp['sgu_w_s'], lp['sgu_b'],
      lp['branch_norm'], lp['w_out'], lp['norm_ffn'], lp['router'])


def _ffn_kernel(h_ref, x_ref, mod_ref, w1_ref, w3_ref, w2_ref, o_ref, acc_ref):
    j = pl.program_id(1)
    h = h_ref[...]
    a = _dot(h, w1_ref[...])
    t = (a * jax.nn.sigmoid(a) * _dot(h, w3_ref[...])).astype(BF16)
    part = _dot(t, w2_ref[...])

    @pl.when(j == 0)
    def _():
        acc_ref[...] = part

    @pl.when(j > 0)
    def _():
        acc_ref[...] += part

    @pl.when(j == pl.num_programs(1) - 1)
    def _():
        o_ref[...] = x_ref[...] + mod_ref[:, 5 * D:6 * D] * acc_ref[...]


def _ffn(h2, x, modg, w1, w3, w2, L, tm, tf):
    T = x.shape[0]
    nt = L // tm
    dff = w1.shape[1]
    return pl.pallas_call(
        _ffn_kernel, grid=(T // tm, dff // tf),
        in_specs=[pl.BlockSpec((tm, D), lambda i, j: (i, 0)),
                  pl.BlockSpec((tm, D), lambda i, j: (i, 0)),
                  _mod_spec(modg, tm, L),
                  pl.BlockSpec((D, tf), lambda i, j: (0, j)),
                  pl.BlockSpec((D, tf), lambda i, j: (0, j)),
                  pl.BlockSpec((tf, D), lambda i, j: (j, 0))],
        out_specs=pl.BlockSpec((tm, D), lambda i, j: (i, 0)),
        out_shape=jax.ShapeDtypeStruct((T, D), F32),
        scratch_shapes=[pltpu.VMEM((tm, D), F32)],
        compiler_params=_cparams("parallel", "arbitrary"), name="ffn",
    )(h2, x, modg, w1, w3, w2)


def _route_kernel(cb_ref, rk_ref, rkt_ref, cnt_ref):
    tm = cb_ref.shape[0]
    mask = cb_ref[...] > 0.0
    mf = jnp.where(mask, 1.0, 0.0)
    r = lax.broadcasted_iota(jnp.int32, (tm, tm), 0)
    c = lax.broadcasted_iota(jnp.int32, (tm, tm), 1)
    before = jnp.where(c < r, 1.0, 0.0).astype(BF16)
    rank = jnp.where(mask, _dot(before, mf.astype(BF16)), -1.0)
    rk_ref[...] = rank
    rkt_ref[...] = rank.T[0:N_EXPERTS, :]
    cnt_ref[...] = jnp.sum(mf, axis=0, keepdims=True)


def _route(cb, tm):
    T = cb.shape[0]
    nt = T // tm
    rk, rkt, cnt = pl.pallas_call(
        _route_kernel, grid=(nt,),
        in_specs=[pl.BlockSpec((tm, 128), lambda i: (i, 0))],
        out_specs=[pl.BlockSpec((tm, 128), lambda i: (i, 0)),
                   pl.BlockSpec((None, N_EXPERTS, tm), lambda i: (i, 0, 0)),
                   pl.BlockSpec((None, 1, 128), lambda i: (i, 0, 0))],
        out_shape=[jax.ShapeDtypeStruct((T, 128), F32),
                   jax.ShapeDtypeStruct((nt, N_EXPERTS, tm), F32),
                   jax.ShapeDtypeStruct((nt, 1, 128), F32)],
        compiler_params=_cparams("parallel"), name="moe_route",
    )(cb)
    return rk, rkt, cnt[:, 0, :N_EXPERTS].astype(jnp.int32).reshape(-1)


MOE_CHUNK = 128


def _moe_kernel(cnt_ref, h_ref, x_ref, mod_ref, cb_ref, rk_ref, rkt_ref, w1_ref, w3_ref, w2_ref,
                o_ref, xg_ref, y_ref):
    i, e, j = pl.program_id(0), pl.program_id(1), pl.program_id(2)
    ne, nj = pl.num_programs(1), pl.num_programs(2)
    tm = h_ref.shape[0]
    ch = MOE_CHUNK
    nch = (cnt_ref[i * ne + e] + (ch - 1)) // ch

    def rows(c):
        return pl.ds(pl.multiple_of(c * ch, ch), ch)

    @pl.when((e == 0) & (j == 0))
    def _():
        o_ref[...] = jnp.zeros_like(o_ref)
        y_ref[...] = jnp.zeros_like(y_ref)

    @pl.when(j == 0)
    def _():
        rid = lax.broadcasted_iota(jnp.int32, (ch, tm), 0).astype(F32)
        rrow = rkt_ref[pl.ds(e, 1), :]

        def gather(c, _):
            sel = jnp.where(rid == rrow - (c * ch).astype(F32), 1.0, 0.0).astype(BF16)
            xg_ref[rows(c), :] = _dot(sel, h_ref[...]).astype(BF16)
            return 0

        lax.fori_loop(0, nch, gather, 0)

    def expert(c, _):
        xg = xg_ref[rows(c), :]
        a = _dot(xg, w1_ref[0])
        t = (a * jax.nn.sigmoid(a) * _dot(xg, w3_ref[0])).astype(BF16)
        part = _dot(t, w2_ref[0])

        @pl.when(j == 0)
        def _():
            y_ref[rows(c), :] = part

        @pl.when(j > 0)
        def _():
            y_ref[rows(c), :] += part

        return 0

    lax.fori_loop(0, nch, expert, 0)

    @pl.when(j == nj - 1)
    def _():
        el = lax.broadcasted_iota(jnp.int32, (tm, 128), 1)
        rcol = jnp.sum(jnp.where(el == e, rk_ref[...], 0.0), axis=-1, keepdims=True)
        wcol = jnp.sum(jnp.where(el == e, cb_ref[...], 0.0), axis=-1, keepdims=True)
        cid = lax.broadcasted_iota(jnp.int32, (tm, 2 * ch), 1).astype(F32)

        def scatter(c, _):
            r0 = pl.multiple_of(c * (2 * ch), 2 * ch)
            selt = jnp.where(cid == rcol - r0.astype(F32), 1.0, 0.0).astype(BF16)
            o_ref[...] += wcol * _dot(selt, y_ref[pl.ds(r0, 2 * ch), :].astype(BF16))
            return 0

        lax.fori_loop(0, (nch + 1) // 2, scatter, 0)

    @pl.when((e == ne - 1) & (j == nj - 1))
    def _():
        o_ref[...] = x_ref[...] + mod_ref[:, 5 * D:6 * D] * o_ref[...]


def _moe(h2, x, modg, cb, w1, w3, w2, L, tm, tf):
    T = x.shape[0]
    ne, _, dff = w1.shape
    rk, rkt, cnt = _route(cb, tm)
    nmod = modg.shape[0]
    grid_spec = pltpu.PrefetchScalarGridSpec(
        num_scalar_prefetch=1, grid=(T // tm, ne, dff // tf),
        in_specs=[pl.BlockSpec((tm, D), lambda i, e, j, c: (i, 0)),
                  pl.BlockSpec((tm, D), lambda i, e, j, c: (i, 0)),
                  pl.BlockSpec((None, 1, 6 * D), lambda i, e, j, c: ((i * tm // L) % nmod, 0, 0)),
                  pl.BlockSpec((tm, 128), lambda i, e, j, c: (i, 0)),
                  pl.BlockSpec((tm, 128), lambda i, e, j, c: (i, 0)),
                  pl.BlockSpec((None, N_EXPERTS, tm), lambda i, e, j, c: (i, 0, 0)),
                  pl.BlockSpec((1, D, tf), lambda i, e, j, c: (e, 0, j)),
                  pl.BlockSpec((1, D, tf), lambda i, e, j, c: (e, 0, j)),
                  pl.BlockSpec((1, tf, D), lambda i, e, j, c: (e, j, 0))],
        out_specs=pl.BlockSpec((tm, D), lambda i, e, j, c: (i, 0)),
        scratch_shapes=[pltpu.VMEM((tm, D), BF16), pltpu.VMEM((tm, D), F32)])
    return pl.pallas_call(
        _moe_kernel, grid_spec=grid_spec,
        out_shape=jax.ShapeDtypeStruct((T, D), F32),
        compiler_params=_cparams("parallel", "arbitrary", "arbitrary"), name="moe",
    )(cnt, h2, x, modg, cb, rk, rkt, w1, w3, w2)


def _final_kernel(x_ref, g_ref, o_ref):
    o_ref[...] = _rms(x_ref[...], g_ref[...])


def _final_norm(x, g, tm):
    T = x.shape[0]
    return pl.pallas_call(
        _final_kernel, grid=(T // tm,),
        in_specs=[pl.BlockSpec((tm, D), lambda i: (i, 0)), pl.BlockSpec((1, D), lambda i: (0, 0))],
        out_specs=pl.BlockSpec((tm, D), lambda i: (i, 0)),
        out_shape=jax.ShapeDtypeStruct((T, D), F32),
        compiler_params=_cparams("parallel"), name="final_norm",
    )(x, g.reshape(1, D))


def _layer(x, modg, lp, l, B, L, ctx):
    is_ctx = ctx is None
    T = B * L
    Bp = -(-B // 8) * 8
    tm = min(L, 512)
    rope_tabs = None if is_ctx else _rope_tables(L, F32)
    u_tm, uv, z, q, k, v = _inproj(x, modg, lp['norm_mix'], lp['w_in'], B, L, tm, rope_tabs)

    u3 = jnp.pad(u_tm.reshape(L, B, BR), ((0, 0), (0, Bp - B), (0, 0)))
    if is_ctx:
        h0 = jnp.zeros((2, Bp, 2 * S5_STATES), F32)
    else:
        h0 = ctx[2]
        h0 = jnp.transpose(h0, (1, 0, 4, 2, 3)).reshape(2, B, 2 * S5_STATES)
        h0 = jnp.pad(h0, ((0, 0), (0, Bp - B), (0, 0)))
    yf, yb, hfin = _s5(u3, h0, lp['s5_wb'], lp['s5_wc'], lp['s5_a'], L, Bp, min(L, 128))

    hv, hx1, hx2 = _short_conv(z, lp['hy_conv_w'], lp['hy_conv_b'], L, tm)
    consts = lp['hy_consts_ctx'] if is_ctx else lp['hy_consts_lat']
    hf = lp['hy_spec_ctx'] if is_ctx else lp['hy_spec_lat']
    y1 = _hy_conv(hv, hx1, lp['hy_bias'][0], hf[0], consts, L)
    y_hy = _hy_conv(y1, hx2, lp['hy_bias'][1], hf[1], consts, L)

    q3 = q.reshape(B, L, BR)
    k3 = k.reshape(B, L, BR)
    v3 = v.reshape(B, L, BR)
    kt, vb = _attn_inputs(k3, v3, None if is_ctx else ctx)
    da = _attn(lp['da_lam'], q3, kt, vb, lp['da_subln'], min(L, 256), 1.0 - lp['lam_init'])

    x, h2, cb = _mixout(x, modg, u_tm, yf.reshape(L, Bp * BR), yb.reshape(L, Bp * BR), uv,
                        y_hy, da.reshape(T, BR), lp, B, L, tm)
    tmf = min(T, 1024)
    if l % 2 == 0:
        x = _ffn(h2, x, modg, lp['ffn_w1'], lp['ffn_w3'], lp['ffn_w2'], L, tmf,
                 lp['ffn_w1'].shape[1] // 2)
    else:
        x = _moe(h2, x, modg, cb, lp['moe_w1'], lp['moe_w3'], lp['moe_w2'], L, tmf, 896)
    if is_ctx:
        fin = hfin[:, :B].reshape(2, B, 2, S5_G, S5_N)
        fin = jnp.transpose(fin, (1, 0, 3, 4, 2))
        return x, k3.reshape(B, L, HEADS, 2 * HEAD_DIM), v3.reshape(B, L, HEADS, VAL_DIM), fin
    return x


def kernel(x_prompt, x_sample, cache_k, cache_v, state_ssm, c, c_ctx, w_ada, b_ada, norm_mix, norm_ffn, w_in, w_out, branch_norm, s5_lam_re, s5_lam_im, s5_log_dt, s5_b_re, s5_b_im, s5_c_re, s5_c_im, s5_d, s5_w_glu, s5_b_glu, sgu_norm, sgu_w_s, sgu_b_s, hy_conv_w, hy_conv_b, hy_w1, hy_b1, hy_w2, hy_b2, hy_w3, hy_bias, da_lq1, da_lk1, da_lq2, da_lk2, da_subln, ffn_w1, ffn_w3, ffn_w2, moe_router, moe_w1, moe_w3, moe_w2, norm_final):
    depth = w_in.shape[0]
    Bc, Lc, _ = x_prompt.shape
    Bs, Ls, _ = x_sample.shape

    cond = jnp.concatenate([c_ctx[None, :], c], axis=0)
    cond8 = jnp.pad(cond, ((0, 8 - cond.shape[0]), (0, 0)))
    mod = _ada(cond8, w_ada, b_ada)

    cc = _hy_consts(Lc)
    cl = _hy_consts(Ls)

    layers = []
    for l in range(depth):
        j = l // 2
        wb, wc, a = _s5_prep(s5_lam_re[l], s5_lam_im[l], s5_log_dt[l], s5_b_re[l], s5_b_im[l],
                             s5_c_re[l], s5_c_im[l])
        lam_init = 0.8 - 0.6 * math.exp(-0.3 * l)
        lam = (jnp.exp(jnp.sum(da_lq1[l] * da_lk1[l])) - jnp.exp(jnp.sum(da_lq2[l] * da_lk2[l]))
               + lam_init)
        hy_args = (hy_w1[l], hy_b1[l], hy_w2[l], hy_b2[l], hy_w3[l])
        lp = dict(
            norm_mix=norm_mix[l], w_in=w_in[l].astype(BF16), w_out=w_out[l].astype(BF16),
            norm_ffn=norm_ffn[l].reshape(1, D), branch_norm=branch_norm[l],
            s5_wb=wb, s5_wc=wc, s5_a=a, s5_d=s5_d[l].reshape(1, BR),
            s5_w_glu=s5_w_glu[l].astype(BF16), s5_b_glu=s5_b_glu[l].reshape(1, BR),
            sgu_norm=sgu_norm[l].reshape(1, BR), sgu_w_s=sgu_w_s[l].astype(BF16),
            sgu_b=jnp.repeat(sgu_b_s[l].T, BR // SGU_HEADS, axis=1),
            hy_conv_w=hy_conv_w[l], hy_conv_b=hy_conv_b[l], hy_bias=hy_bias[l],
            hy_consts_ctx=cc, hy_consts_lat=cl,
            hy_spec_ctx=_hy_filter_spectrum(Lc, *hy_args, cc[4]),
            hy_spec_lat=_hy_filter_spectrum(Ls, *hy_args, cl[4]),
            da_lam=lam, lam_init=lam_init, da_subln=da_subln[l],
        )
        if l % 2 == 0:
            lp.update(ffn_w1=ffn_w1[j].astype(BF16), ffn_w3=ffn_w3[j].astype(BF16),
                      ffn_w2=ffn_w2[j].astype(BF16),
                      router=jnp.zeros((D, 128), F32))
        else:
            lp.update(moe_w1=moe_w1[j].astype(BF16), moe_w3=moe_w3[j].astype(BF16),
                      moe_w2=moe_w2[j].astype(BF16),
                      router=jnp.pad(moe_router[j], ((0, 0), (0, 128 - N_EXPERTS))))
        layers.append(lp)

    xc = x_prompt.reshape(Bc * Lc, D)
    ks, vs, ss = [], [], []
    for l in range(depth):
        xc, k_l, v_l, s_l = _layer(xc, mod[l, 0:1].reshape(1, 1, 6 * D), layers[l], l, Bc, Lc, None)
        ks.append(k_l)
        vs.append(v_l)
        ss.append(s_l)
    y_prompt = _final_norm(xc, norm_final, 512).reshape(Bc, Lc, D)

    xs = x_sample.reshape(Bs * Ls, D)
    for l in range(depth):
        xs = _layer(xs, mod[l, 1:1 + Bs].reshape(Bs, 1, 6 * D), layers[l], l, Bs, Ls,
                    (cache_k[:, l], cache_v[:, l], state_ssm[:, l]))
    y_sample = _final_norm(xs, norm_final, 512).reshape(Bs, Ls, D)
    return (y_prompt, y_sample, jnp.stack(ks, axis=1), jnp.stack(vs, axis=1), jnp.stack(ss, axis=1))
```

```python
import functools
import math

import jax
import jax.numpy as jnp
import numpy as np
from jax import lax
from jax.experimental import pallas as pl
from jax.experimental.pallas import tpu as pltpu

F32 = jnp.float32
BF16 = jnp.bfloat16

D = 1024
BR = 256
PROJ = 9 * BR
S5_G, S5_N, S5_P = 16, 64, 16
S5_STATES = S5_G * S5_N
SGU_CHUNK, SGU_HEADS = 128, 4
HEADS, HEAD_DIM, VAL_DIM = 4, 32, 64
GRID_WIDTH = 64
ROPE_DIM = HEAD_DIM // 2
ROPE_BASE = 10000.0
HY_EMB, HY_BANDS = 33, 16
HY_DECAY_TARGET, HY_FAST, HY_SLOW = 1e-2, 0.3, 1.5
N_EXPERTS = 8
EPS = 1e-6
LOG2E = 1.4426950408889634

HY_DIRECT_MAX = 512
VMEM_LIMIT = 56 * 1024 * 1024


def _cparams(*sem):
    return pltpu.CompilerParams(dimension_semantics=sem, vmem_limit_bytes=VMEM_LIMIT)


def _dot(a, b):
    return jnp.dot(a, b, preferred_element_type=F32)


def _dot3(a, b):
    a_hi = a.astype(BF16)
    b_hi = b.astype(BF16)
    a_lo = (a - a_hi.astype(F32)).astype(BF16)
    b_lo = (b - b_hi.astype(F32)).astype(BF16)
    return _dot(a_hi, b_hi) + (_dot(a_hi, b_lo) + _dot(a_lo, b_hi))


def _rms(x, g):
    return x * lax.rsqrt(jnp.mean(x * x, axis=-1, keepdims=True) + EPS) * g


def _mod_spec(modg, tm, L):
    nmod = modg.shape[0]
    return pl.BlockSpec((None, 1, 6 * D), lambda i, *_: ((i * tm // L) % nmod, 0, 0))


def _ada_kernel(c_ref, w_ref, b_ref, o_ref):
    c = c_ref[...]
    s = (c * jax.nn.sigmoid(c)).astype(BF16)
    o_ref[0] = _dot(s, w_ref[0].astype(BF16)) + b_ref[0]


def _ada(cond8, w_ada, b_ada):
    depth = w_ada.shape[0]
    tn = 1536
    return pl.pallas_call(
        _ada_kernel,
        grid=(depth, 6 * D // tn),
        in_specs=[pl.BlockSpec((8, D), lambda l, j: (0, 0)),
                  pl.BlockSpec((1, D, tn), lambda l, j: (l, 0, j)),
                  pl.BlockSpec((1, 1, tn), lambda l, j: (l, 0, j))],
        out_specs=pl.BlockSpec((1, 8, tn), lambda l, j: (l, 0, j)),
        out_shape=jax.ShapeDtypeStruct((depth, 8, 6 * D), F32),
        compiler_params=_cparams("parallel", "parallel"),
        name="ada",
    )(cond8, w_ada, b_ada.reshape(depth, 1, 6 * D))


def _inproj_kernel(*refs, rope):
    if rope:
        (x_ref, mod_ref, g_ref, w_ref, cos_ref, sin_ref,
         u_ref, uv_ref, z_ref, q_ref, k_ref, v_ref) = refs
    else:
        x_ref, mod_ref, g_ref, w_ref, u_ref, uv_ref, z_ref, q_ref, k_ref, v_ref = refs
    x = x_ref[...]
    h = _rms(x, g_ref[...]) * (1.0 + mod_ref[:, D:2 * D]) + mod_ref[:, 0:D]
    p = _dot(h.astype(BF16), w_ref[...])
    u_ref[...] = p[:, 0:BR]
    uv_ref[...] = p[:, BR:3 * BR]
    z_ref[...] = p[:, 3 * BR:6 * BR]
    q = p[:, 6 * BR:7 * BR]
    k = p[:, 7 * BR:8 * BR]
    if rope:
        cs = cos_ref[...]
        sn = sin_ref[...]
        lane = lax.broadcasted_iota(jnp.int32, q.shape, 1)
        first = (lane % (2 * (ROPE_DIM // 2))) < (ROPE_DIM // 2)
        half = ROPE_DIM // 2

        def rot(t):
            return jnp.where(first, pltpu.roll(t, BR - half, 1), pltpu.roll(t, half, 1))

        q = q * cs + rot(q) * sn
        k = k * cs + rot(k) * sn
    q_ref[...] = q
    k_ref[...] = k
    v_ref[...] = p[:, 8 * BR:9 * BR]


def _inproj(x, modg, g, w_bf, B, L, tm, rope_tabs):
    T = B * L
    nt = L // tm
    rope = rope_tabs is not None
    in_specs = [pl.BlockSpec((tm, D), lambda i: (i, 0)),
                _mod_spec(modg, tm, L),
                pl.BlockSpec((1, D), lambda i: (0, 0)),
                pl.BlockSpec((D, PROJ), lambda i: (0, 0))]
    args = [x, modg, g.reshape(1, D), w_bf]
    if rope:
        in_specs += [pl.BlockSpec((tm, BR), lambda i: (i % nt, 0))] * 2
        args += list(rope_tabs)
    tok = lambda n: pl.BlockSpec((tm, n), lambda i: (i, 0))
    out_specs = [pl.BlockSpec((tm, BR), lambda i: (i % nt, i // nt)),
                 tok(2 * BR), tok(3 * BR), tok(BR), tok(BR), tok(BR)]
    out_shape = [jax.ShapeDtypeStruct((L, B * BR), F32),
                 jax.ShapeDtypeStruct((T, 2 * BR), F32),
                 jax.ShapeDtypeStruct((T, 3 * BR), F32),
                 jax.ShapeDtypeStruct((T, BR), F32),
                 jax.ShapeDtypeStruct((T, BR), F32),
                 jax.ShapeDtypeStruct((T, BR), F32)]
    return pl.pallas_call(
        functools.partial(_inproj_kernel, rope=rope),
        grid=(T // tm,), in_specs=in_specs, out_specs=out_specs, out_shape=out_shape,
        compiler_params=_cparams("parallel"), name="inproj",
    )(*args)


def _s5_kernel(uf_ref, ub_ref, h0_ref, wb_ref, wc_ref, a_ref, yf_ref, yb_ref, hfin_ref,
               xs_ref, hc_ref, *, tt):
    i = pl.program_id(1)
    last = pl.num_programs(1) - 1
    half = S5_STATES // 2

    @pl.when(i == 0)
    def _():
        hc_ref[...] = h0_ref[...]

    for d in range(2):
        u_ref = uf_ref if d == 0 else ub_ref
        y_ref = yf_ref if d == 0 else yb_ref
        u2 = u_ref[...].reshape(tt * 8, BR).astype(BF16)
        xs_ref[...] = _dot(u2, wb_ref[d])
        for c in range(2):
            cr = slice(c * half, (c + 1) * half)
            ci = slice(S5_STATES + c * half, S5_STATES + (c + 1) * half)
            ar = jnp.broadcast_to(a_ref[d, 0:1, cr], (8, half))
            ai = jnp.broadcast_to(a_ref[d, 1:2, cr], (8, half))

            def body(s, carry, cr=cr, ci=ci, ar=ar, ai=ai, d=d):
                hr, hi = carry
                t = s if d == 0 else tt - 1 - s
                r0 = pl.multiple_of(t * 8, 8)
                nr = ar * hr - ai * hi + xs_ref[pl.ds(r0, 8), cr]
                ni = ar * hi + ai * hr + xs_ref[pl.ds(r0, 8), ci]
                xs_ref[pl.ds(r0, 8), cr] = nr
                xs_ref[pl.ds(r0, 8), ci] = ni
                return nr, ni

            hr, hi = lax.fori_loop(0, tt, body, (hc_ref[d, :, cr], hc_ref[d, :, ci]))
            hc_ref[d, :, cr] = hr
            hc_ref[d, :, ci] = hi
        y = _dot(xs_ref[...].astype(BF16), wc_ref[d])
        y_ref[...] = y.reshape(tt, 8, BR)

    @pl.when(i == last)
    def _():
        hfin_ref[...] = hc_ref[...]


def _s5(u_tm, h0, wb, wc, a, L, Bp, tt):
    nT = L // tt
    ng = Bp // 8
    blk = lambda f: pl.BlockSpec((tt, 8, BR), f)
    const = lambda shp: pl.BlockSpec(shp, lambda g, i: (0,) * len(shp))
    return pl.pallas_call(
        functools.partial(_s5_kernel, tt=tt),
        grid=(ng, nT),
        in_specs=[blk(lambda g, i: (i, g, 0)), blk(lambda g, i: (nT - 1 - i, g, 0)),
                  pl.BlockSpec((2, 8, 2 * S5_STATES), lambda g, i: (0, g, 0)),
                  const((2, BR, 2 * S5_STATES)), const((2, 2 * S5_STATES, BR)),
                  const((2, 2, S5_STATES))],
        out_specs=[blk(lambda g, i: (i, g, 0)), blk(lambda g, i: (nT - 1 - i, g, 0)),
                   pl.BlockSpec((2, 8, 2 * S5_STATES), lambda g, i: (0, g, 0))],
        out_shape=[jax.ShapeDtypeStruct((L, Bp, BR), F32),
                   jax.ShapeDtypeStruct((L, Bp, BR), F32),
                   jax.ShapeDtypeStruct((2, Bp, 2 * S5_STATES), F32)],
        scratch_shapes=[pltpu.VMEM((tt * 8, 2 * S5_STATES), F32),
                        pltpu.VMEM((2, 8, 2 * S5_STATES), F32)],
        compiler_params=_cparams("parallel", "arbitrary"), name="s5",
    )(u_tm, u_tm, h0, wb, wc, a)


def _s5_prep(lam_re, lam_im, log_dt, b_re, b_im, c_re, c_im):
    dt = jnp.exp(log_dt)[..., None]
    mag = jnp.exp(lam_re * dt)
    lb_re = mag * jnp.cos(lam_im * dt)
    lb_im = mag * jnp.sin(lam_im * dt)
    den = lam_re * lam_re + lam_im * lam_im
    nr = lb_re - 1.0
    coef_re = ((nr * lam_re + lb_im * lam_im) / den)[..., None]
    coef_im = ((lb_im * lam_re - nr * lam_im) / den)[..., None]
    bp_re = coef_re * b_re - coef_im * b_im
    bp_im = coef_re * b_im + coef_im * b_re
    eye = jnp.eye(S5_G, dtype=lam_re.dtype)

    def blockdiag_in(b):
        return jnp.einsum('dgnp,gh->dgphn', b, eye).reshape(2, BR, S5_STATES)

    def blockdiag_out(c):
        return jnp.einsum('dgpn,gh->dgnhp', c, eye).reshape(2, S5_STATES, BR)

    wb = jnp.concatenate([blockdiag_in(bp_re), blockdiag_in(bp_im)], axis=-1)
    wc = jnp.concatenate([blockdiag_out(c_re), -blockdiag_out(c_im)], axis=1)
    a = jnp.stack([lb_re.reshape(2, S5_STATES), lb_im.reshape(2, S5_STATES)], axis=1)
    return wb.astype(BF16), wc.astype(BF16), a


def _short_kernel(z_ref, zp_ref, zn_ref, w_ref, b_ref, v_ref, x1_ref, x2_ref, *, nt):
    j = pl.program_id(0) % nt
    z = z_ref[...]
    tm = z.shape[0]
    row = lax.broadcasted_iota(jnp.int32, z.shape, 0)
    prev_row = jnp.where(j > 0, zp_ref[7:8, :], 0.0)
    next_row = jnp.where(j < nt - 1, zn_ref[0:1, :], 0.0)
    zprev = jnp.where(row == 0, prev_row, pltpu.roll(z, 1, 0))
    znext = jnp.where(row == tm - 1, next_row, pltpu.roll(z, tm - 1, 0))
    y = zprev * w_ref[0:1, :] + z * w_ref[1:2, :] + znext * w_ref[2:3, :] + b_ref[...]
    v_ref[...] = y[:, 0:BR]
    x1_ref[...] = y[:, BR:2 * BR]
    x2_ref[...] = y[:, 2 * BR:3 * BR]


def _short_conv(z, w, b, L, tm):
    T = z.shape[0]
    nt = L // tm
    r8 = tm // 8
    nblk8 = T // 8
    out = jax.ShapeDtypeStruct((T, BR), F32)
    return pl.pallas_call(
        functools.partial(_short_kernel, nt=nt),
        grid=(T // tm,),
        in_specs=[pl.BlockSpec((tm, 3 * BR), lambda i: (i, 0)),
                  pl.BlockSpec((8, 3 * BR), lambda i: (jnp.maximum(i * r8 - 1, 0), 0)),
                  pl.BlockSpec((8, 3 * BR), lambda i: (jnp.minimum((i + 1) * r8, nblk8 - 1), 0)),
                  pl.BlockSpec((3, 3 * BR), lambda i: (0, 0)),
                  pl.BlockSpec((1, 3 * BR), lambda i: (0, 0))],
        out_specs=[pl.BlockSpec((tm, BR), lambda i: (i, 0))] * 3,
        out_shape=[out, out, out],
        compiler_params=_cparams("parallel"), name="short_conv",
    )(z, z, z, w, b.reshape(1, 3 * BR))


def _cs(num, den):
    th = (2.0 * math.pi / den) * (num % den).astype(F32)
    return jnp.cos(th), jnp.sin(th)


def _iota(m):
    return jnp.arange(m, dtype=jnp.int32)


def _hy_direct_tables(L):
    n = 2 * L
    c, s = _cs(_iota(n)[:, None] * _iota(L)[None, :], n)
    wf = jnp.concatenate([c, -s], axis=0)
    c, s = _cs((_iota(L)[:, None] + L // 2) * _iota(n)[None, :], n)
    wi = jnp.concatenate([c, -s], axis=1) * (1.0 / n)
    return wf.astype(BF16), wi.astype(BF16)


def _hy_spec_direct_kernel(f_ref, wf_ref, o_ref):
    o_ref[...] = _dot(wf_ref[...], f_ref[...].astype(BF16))


def _hy_spec_direct(filt, wf):
    L, C = filt.shape
    spec = pl.pallas_call(
        _hy_spec_direct_kernel,
        out_shape=jax.ShapeDtypeStruct((4 * L, C), F32), name="hy_spec_direct",
    )(filt, wf)
    return spec.reshape(2, 2 * L, 2, BR).transpose(2, 0, 1, 3)


def _hy_direct_kernel(z_ref, cw_ref, cb_ref, wf_ref, wi_ref, hf_ref, bias_ref, o_ref, *, L, nseq):
    n = 2 * L
    row = lax.broadcasted_iota(jnp.int32, (L, 3 * BR), 0)
    for s in range(nseq):
        z = z_ref[s * L:(s + 1) * L, :]
        zprev = jnp.where(row == 0, 0.0, pltpu.roll(z, 1, 0))
        znext = jnp.where(row == L - 1, 0.0, pltpu.roll(z, L - 1, 0))
        zc = zprev * cw_ref[0:1, :] + z * cw_ref[1:2, :] + znext * cw_ref[2:3, :] + cb_ref[...]
        y = zc[:, 0:BR]
        for o in range(2):
            x = _dot(wf_ref[...], y.astype(BF16))
            xr, xi = x[0:n], x[n:2 * n]
            hr, hi = hf_ref[o, 0], hf_ref[o, 1]
            yc = jnp.concatenate([xr * hr - xi * hi, xr * hi + xi * hr], axis=0).astype(BF16)
            y = zc[:, (o + 1) * BR:(o + 2) * BR] * (_dot(wi_ref[...], yc) + y * bias_ref[o:o + 1, :])
        o_ref[s * L:(s + 1) * L, :] = y


def _hy_direct(z, cw, cb, bias, tabs, hf, L):
    T = z.shape[0]
    wf, wi = tabs
    nseq = 4
    const = lambda shp: pl.BlockSpec(shp, lambda i: (0,) * len(shp))
    return pl.pallas_call(
        functools.partial(_hy_direct_kernel, L=L, nseq=nseq), grid=(T // (nseq * L),),
        in_specs=[pl.BlockSpec((nseq * L, 3 * BR), lambda i: (i, 0)),
                  const((3, 3 * BR)), const((1, 3 * BR)), const((4 * L, L)), const((L, 4 * L)),
                  const((2, 2, 2 * L, BR)), const((2, BR))],
        out_specs=pl.BlockSpec((nseq * L, BR), lambda i: (i, 0)),
        out_shape=jax.ShapeDtypeStruct((T, BR), F32),
        compiler_params=_cparams("parallel"), name="hy_direct",
    )(z, cw, cb.reshape(1, 3 * BR), wf, wi, hf, bias)


HY_N1 = 8
HY_K1 = HY_N1 // 2 + 1


def _hy_split_tables(L):
    n = 2 * L
    n2 = n // HY_N1
    hk = n2 // 2
    k1 = _iota(HY_K1)[:, None, None]
    k2 = _iota(n2)[None, :, None]
    j2 = _iota(n2)[None, None, :]
    c, s = _cs(j2 * k2 * HY_N1 + j2 * k1, n)
    top = jnp.concatenate([c, s], axis=2).reshape(HY_K1, 2, hk, 2 * n2)
    bot = jnp.concatenate([-s, c], axis=2).reshape(HY_K1, 2, hk, 2 * n2)
    wf = jnp.concatenate([top, bot], axis=2)
    ct = jnp.swapaxes(c, 1, 2).reshape(HY_K1, n2, 2, hk).transpose(0, 2, 1, 3)
    st = jnp.swapaxes(s, 1, 2).reshape(HY_K1, n2, 2, hk).transpose(0, 2, 1, 3)
    wi = jnp.concatenate([jnp.concatenate([ct, -st], axis=3),
                          jnp.concatenate([st, ct], axis=3)], axis=2)
    kk = _iota(HY_K1)[:, None]
    c1, s1 = _cs(kk * _iota(HY_N1 // 2)[None, :], HY_N1)
    wgt = jnp.where((kk == 0) | (kk == HY_N1 // 2), 1.0, 2.0) / n
    co, so = _cs(kk * (_iota(HY_N1 // 2)[None, :] + HY_N1 // 4), HY_N1)
    coef = jnp.concatenate([c1, -s1, wgt * co, -wgt * so], axis=1)
    return wf.astype(BF16), wi.astype(BF16), coef.astype(F32)


def _hy_split_stage1(coef_ref, v_ref, a_ref, k1, n2):
    nin = HY_N1 // 2
    ar = ai = None
    for j in range(nin):
        xj = v_ref[0, j * n2:(j + 1) * n2, :]
        tr, ti = coef_ref[k1, j] * xj, coef_ref[k1, nin + j] * xj
        ar, ai = (tr, ti) if ar is None else (ar + tr, ai + ti)
    a_ref[0:n2, :] = ar.astype(BF16)
    a_ref[n2:2 * n2, :] = ai.astype(BF16)


def _hy_spec_split_kernel(coef_ref, f_ref, wf_ref, o_ref, a_ref, *, L):
    n2 = 2 * L // HY_N1

    @pl.when(pl.program_id(2) == 0)
    def _():
        _hy_split_stage1(coef_ref, f_ref, a_ref, pl.program_id(1), n2)

    o_ref[0, 0, 0] = _dot(wf_ref[0, 0], a_ref[...])


def _hy_split_kernel(coef_ref, v_ref, xg_ref, bias_ref, wf_ref, wi_ref, hf_ref, o_ref, a_ref, *, L):
    k1, hh = pl.program_id(1), pl.program_id(2)
    n2 = 2 * L // HY_N1
    hk = n2 // 2
    nin = HY_N1 // 2

    @pl.when(hh == 0)
    def _():
        _hy_split_stage1(coef_ref, v_ref, a_ref, k1, n2)

    @pl.when((k1 == 0) & (hh == 0))
    def _():
        o_ref[...] = jnp.zeros_like(o_ref)

    x = _dot(wf_ref[0, 0], a_ref[...])
    xr, xi = x[0:hk], x[hk:n2]
    hr, hi = hf_ref[0, 0, 0:hk, :], hf_ref[0, 0, hk:n2, :]
    yc = jnp.concatenate([xr * hr - xi * hi, xr * hi + xi * hr], axis=0).astype(BF16)
    b = _dot(wi_ref[0, 0], yc)
    for o in range(nin):
        o_ref[0, o * n2:(o + 1) * n2, :] += (coef_ref[k1, 2 * nin + o] * b[0:n2]
                                             + coef_ref[k1, 3 * nin + o] * b[n2:2 * n2])

    @pl.when((k1 == HY_K1 - 1) & (hh == 1))
    def _():
        o_ref[0] = xg_ref[0] * (o_ref[0] + v_ref[0] * bias_ref[...])


def _hy_spec_split(filt2, tabs, L):
    wf, _, coef = tabs
    n2 = 2 * L // HY_N1
    return pl.pallas_call(
        functools.partial(_hy_spec_split_kernel, L=L), grid=(2, HY_K1, 2),
        in_specs=[pl.BlockSpec(memory_space=pltpu.SMEM),
                  pl.BlockSpec((1, L, BR), lambda o, k, h: (o, 0, 0)),
                  pl.BlockSpec((1, 1, n2, 2 * n2), lambda o, k, h: (k, h, 0, 0))],
        out_specs=pl.BlockSpec((1, 1, 1, n2, BR), lambda o, k, h: (o, k, h, 0, 0)),
        out_shape=jax.ShapeDtypeStruct((2, HY_K1, 2, n2, BR), F32),
        scratch_shapes=[pltpu.VMEM((2 * n2, BR), BF16)],
        compiler_params=_cparams("parallel", "arbitrary", "arbitrary"), name="hy_spec_split",
    )(coef, filt2, wf)


def _hy_split(v, xg, bias, hf, tabs, L):
    wf, wi, coef = tabs
    B = v.shape[0]
    n2 = 2 * L // HY_N1
    seq = pl.BlockSpec((1, L, BR), lambda b, k, h: (b, 0, 0))
    return pl.pallas_call(
        functools.partial(_hy_split_kernel, L=L), grid=(B, HY_K1, 2),
        in_specs=[pl.BlockSpec(memory_space=pltpu.SMEM), seq, seq,
                  pl.BlockSpec((1, BR), lambda b, k, h: (0, 0)),
                  pl.BlockSpec((1, 1, n2, 2 * n2), lambda b, k, h: (k, h, 0, 0)),
                  pl.BlockSpec((1, 1, 2 * n2, n2), lambda b, k, h: (k, h, 0, 0)),
                  pl.BlockSpec((1, 1, n2, BR), lambda b, k, h: (k, h, 0, 0))],
        out_specs=seq,
        out_shape=jax.ShapeDtypeStruct((B, L, BR), F32),
        scratch_shapes=[pltpu.VMEM((2 * n2, BR), BF16)],
        compiler_params=_cparams("parallel", "arbitrary", "arbitrary"), name="hy_split",
    )(coef, v, xg, bias.reshape(1, BR), wf, wi, hf)


def _hy_filters(L, w1, b1, w2, b2, w3):
    dt = w1.dtype
    pos = jnp.arange(L, dtype=dt)
    t01 = jnp.linspace(0.0, 1.0, L, dtype=dt)[:, None]
    w = (2.0 * math.pi / L) * pos[:, None]
    bands = jnp.linspace(1e-4, HY_BANDS - 1, HY_BANDS, dtype=dt)[None, :]
    feats = jnp.concatenate([t01, jnp.cos(bands * w), -jnp.sin(bands * w)], axis=-1)
    h = jnp.sin(feats @ w1 + b1)
    h = jnp.sin(h @ w2 + b2)
    h = h @ w3
    dist = jnp.abs(pos - (L // 2)) / L
    decay = jnp.abs(jnp.linspace(math.log(HY_DECAY_TARGET) / HY_SLOW,
                                 math.log(HY_DECAY_TARGET) / HY_FAST, 2 * BR, dtype=dt))
    return h * jnp.exp(-dist[:, None] * decay[None, :])


def _attn_kernel(lam_ref, q_ref, kt_ref, v_ref, sub_ref, o_ref, *, qscale, post):
    q = (q_ref[0] * qscale).astype(BF16)
    lam = lam_ref[0, 0]
    sub = sub_ref[...]
    for h in range(HEADS):
        vh = v_ref[0, :, h * VAL_DIM:(h + 1) * VAL_DIM]
        res = []
        for m in range(2):
            c0 = h * 2 * HEAD_DIM + m * HEAD_DIM
            s = _dot(q[:, c0:c0 + HEAD_DIM], kt_ref[0, c0:c0 + HEAD_DIM, :])
            p = jnp.exp2(s - jnp.max(s, axis=-1, keepdims=True))
            l = jnp.sum(p, axis=-1, keepdims=True)
            res.append(_dot(p.astype(BF16), vh) / l)
        o = res[0] - lam * res[1]
        o = o * lax.rsqrt(jnp.mean(o * o, axis=-1, keepdims=True) + EPS) * sub * post
        o_ref[0, :, h * VAL_DIM:(h + 1) * VAL_DIM] = o


def _attn(lam, q, kt, v, subln, tq, post):
    B, L, _ = q.shape
    Lk = kt.shape[2]
    return pl.pallas_call(
        functools.partial(_attn_kernel, qscale=HEAD_DIM ** -0.5 * LOG2E, post=post),
        grid=(B, L // tq),
        in_specs=[pl.BlockSpec(memory_space=pltpu.SMEM),
                  pl.BlockSpec((1, tq, BR), lambda b, i: (b, i, 0)),
                  pl.BlockSpec((1, BR, Lk), lambda b, i: (b, 0, 0)),
                  pl.BlockSpec((1, Lk, BR), lambda b, i: (b, 0, 0)),
                  pl.BlockSpec((1, VAL_DIM), lambda b, i: (0, 0))],
        out_specs=pl.BlockSpec((1, tq, BR), lambda b, i: (b, i, 0)),
        out_shape=jax.ShapeDtypeStruct((B, L, BR), F32),
        compiler_params=_cparams("parallel", "parallel"), name="diff_attn",
    )(lam.reshape(1, 1), q, kt, v, subln.reshape(1, VAL_DIM))


def _attn_inputs(k3, v3, ctx):
    B = k3.shape[0]
    if ctx is not None:
        k3 = jnp.concatenate([ctx[0].reshape(B, -1, BR), k3], axis=1)
        v3 = jnp.concatenate([ctx[1].reshape(B, -1, BR), v3], axis=1)
    return jnp.swapaxes(k3, 1, 2).astype(BF16), v3.astype(BF16)


def _attn_latent_test(q, k, v, ck, cv):
    kt, vb = _attn_inputs(k, v, (ck, cv))
    return _attn(jnp.float32(0.5), q, kt, vb, jnp.ones((VAL_DIM,), F32), 256, 0.5)


def _rope_tables(L, dt):
    n_rows = L // GRID_WIDTH
    row = jnp.repeat(jnp.arange(n_rows, dtype=dt), GRID_WIDTH)
    col = (jnp.arange(L) % GRID_WIDTH).astype(dt)
    inv = ROPE_BASE ** (-jnp.arange(0, ROPE_DIM, 2, dtype=dt) / ROPE_DIM)
    ar = row[:, None] * inv[None, :]
    ac = col[:, None] * inv[None, :]
    ang = jnp.concatenate([ar, ar, ac, ac], axis=-1)
    sign = jnp.tile(jnp.concatenate([-jnp.ones(ROPE_DIM // 2, dt), jnp.ones(ROPE_DIM // 2, dt)]), 2)
    reps = BR // HEAD_DIM
    return jnp.tile(jnp.cos(ang), (1, reps)), jnp.tile(jnp.sin(ang) * sign, (1, reps))


def _mixout_kernel(x_ref, mod_ref, u_ref, yf_ref, yb_ref, uv_ref, hy_ref, da_ref,
                   sd_ref, wg_ref, bg_ref, sn_ref, ws_ref, bs_ref, bn_ref, wo_ref, gf_ref, rt_ref,
                   xo_ref, h2_ref, cb_ref, *, route):
    tm = x_ref.shape[0]
    u = u_ref[...]
    y = jax.nn.gelu(sd_ref[...] * u + yf_ref[...] + yb_ref[...])
    y = y * jax.nn.sigmoid(_dot(y.astype(BF16), wg_ref[...]) + bg_ref[...])
    acc = _dot(_rms(y, bn_ref[0:1, :]).astype(BF16), wo_ref[0:BR, :])
    guv = jax.nn.gelu(uv_ref[...])
    gu = guv[:, 0:BR]
    gv = _rms(guv[:, BR:2 * BR], sn_ref[...]).astype(BF16)
    lane = lax.broadcasted_iota(jnp.int32, (SGU_CHUNK, BR), 1)
    hd = BR // SGU_HEADS
    zs = []
    for c in range(tm // SGU_CHUNK):
        vc = gv[c * SGU_CHUNK:(c + 1) * SGU_CHUNK, :]
        z = bs_ref[...]
        for h in range(SGU_HEADS):
            z = z + jnp.where(lane // hd == h, _dot(ws_ref[h], vc), 0.0)
        zs.append(z)
    z = zs[0] if len(zs) == 1 else jnp.concatenate(zs, axis=0)
    acc = acc + _dot(_rms(gu * z, bn_ref[1:2, :]).astype(BF16), wo_ref[BR:2 * BR, :])
    acc = acc + _dot(_rms(hy_ref[...], bn_ref[2:3, :]).astype(BF16), wo_ref[2 * BR:3 * BR, :])
    acc = acc + _dot(da_ref[...].astype(BF16), wo_ref[3 * BR:4 * BR, :])
    xn = x_ref[...] + mod_ref[:, 2 * D:3 * D] * acc
    xo_ref[...] = xn
    h2 = _rms(xn, gf_ref[...]) * (1.0 + mod_ref[:, 4 * D:5 * D]) + mod_ref[:, 3 * D:4 * D]
    h2_ref[...] = h2.astype(BF16)
    if not route:
        cb_ref[...] = jnp.zeros_like(cb_ref)
        return
    logits = _dot3(h2, rt_ref[...])
    el = lax.broadcasted_iota(jnp.int32, logits.shape, 1)
    logits = jnp.where(el < N_EXPERTS, logits, -jnp.inf)
    e = jnp.exp(logits - jnp.max(logits, axis=-1, keepdims=True))
    probs = e / jnp.sum(e, axis=-1, keepdims=True)
    big = logits.shape[1]
    m1 = jnp.max(probs, axis=-1, keepdims=True)
    i1 = jnp.min(jnp.where(probs == m1, el, big), axis=-1, keepdims=True)
    p2 = jnp.where((el == i1) | (el >= N_EXPERTS), -1.0, probs)
    m2 = jnp.max(p2, axis=-1, keepdims=True)
    i2 = jnp.min(jnp.where(p2 == m2, el, big), axis=-1, keepdims=True)
    tot = m1 + m2
    cb_ref[...] = jnp.where(el == i1, m1 / tot, 0.0) + jnp.where(el == i2, m2 / tot, 0.0)


def _mixout(x, modg, u_tm2, yf2, yb2, uv, hy, da, lp, B, L, tm, route):
    T = B * L
    nt = L // tm
    tok = lambda n: pl.BlockSpec((tm, n), lambda i: (i, 0))
    tmaj = pl.BlockSpec((tm, BR), lambda i: (i % nt, i // nt))
    const = lambda shp: pl.BlockSpec(shp, lambda i: (0,) * len(shp))
    return pl.pallas_call(
        functools.partial(_mixout_kernel, route=route), grid=(T // tm,),
        in_specs=[tok(D), _mod_spec(modg, tm, L),
                  tmaj, tmaj, tmaj, tok(2 * BR), tok(BR), tok(BR),
                  const((1, BR)), const((BR, BR)), const((1, BR)), const((1, BR)),
                  const((SGU_HEADS, SGU_CHUNK, SGU_CHUNK)), const((SGU_CHUNK, BR)),
                  const((3, BR)), const((D, D)), const((1, D)), const((D, 128))],
        out_specs=[tok(D), tok(D), tok(128)],
        out_shape=[jax.ShapeDtypeStruct((T, D), F32), jax.ShapeDtypeStruct((T, D), BF16),
                   jax.ShapeDtypeStruct((T, 128), F32)],
        compiler_params=_cparams("parallel"), name="mix_out",
    )(x, modg, u_tm2, yf2, yb2, uv, hy, da,
      lp['s5_d'], lp['s5_w_glu'], lp['s5_b_glu'], lp['sgu_norm'], lp['sgu_w_s'], lp['sgu_b'],
      lp['branch_norm'], lp['w_out'], lp['norm_ffn'], lp['router'])


def _ffn_kernel(h_ref, x_ref, mod_ref, w1_ref, w3_ref, w2_ref, o_ref, acc_ref):
    j = pl.program_id(1)
    h = h_ref[...]
    a = _dot(h, w1_ref[...])
    t = (a * jax.nn.sigmoid(a) * _dot(h, w3_ref[...])).astype(BF16)
    part = _dot(t, w2_ref[...])

    @pl.when(j == 0)
    def _():
        acc_ref[...] = part

    @pl.when(j > 0)
    def _():
        acc_ref[...] += part

    @pl.when(j == pl.num_programs(1) - 1)
    def _():
        o_ref[...] = x_ref[...] + mod_ref[:, 5 * D:6 * D] * acc_ref[...]


def _ffn(h2, x, modg, w1, w3, w2, L, tm, tf):
    T = x.shape[0]
    nt = L // tm
    dff = w1.shape[1]
    return pl.pallas_call(
        _ffn_kernel, grid=(T // tm, dff // tf),
        in_specs=[pl.BlockSpec((tm, D), lambda i, j: (i, 0)),
                  pl.BlockSpec((tm, D), lambda i, j: (i, 0)),
                  _mod_spec(modg, tm, L),
                  pl.BlockSpec((D, tf), lambda i, j: (0, j)),
                  pl.BlockSpec((D, tf), lambda i, j: (0, j)),
                  pl.BlockSpec((tf, D), lambda i, j: (j, 0))],
        out_specs=pl.BlockSpec((tm, D), lambda i, j: (i, 0)),
        out_shape=jax.ShapeDtypeStruct((T, D), F32),
        scratch_shapes=[pltpu.VMEM((tm, D), F32)],
        compiler_params=_cparams("parallel", "arbitrary"), name="ffn",
    )(h2, x, modg, w1, w3, w2)


def _route_kernel(cb_ref, rk_ref, rkt_ref, cnt_ref):
    tm = cb_ref.shape[0]
    mask = cb_ref[...] > 0.0
    mf = jnp.where(mask, 1.0, 0.0)
    r = lax.broadcasted_iota(jnp.int32, (tm, tm), 0)
    c = lax.broadcasted_iota(jnp.int32, (tm, tm), 1)
    before = jnp.where(c < r, 1.0, 0.0).astype(BF16)
    rank = jnp.where(mask, _dot(before, mf.astype(BF16)), -1.0)
    rk_ref[...] = rank
    rkt_ref[...] = rank.T[0:N_EXPERTS, :]
    cnt_ref[...] = jnp.sum(mf, axis=0, keepdims=True)


def _route(cb, tm):
    T = cb.shape[0]
    nt = T // tm
    rk, rkt, cnt = pl.pallas_call(
        _route_kernel, grid=(nt,),
        in_specs=[pl.BlockSpec((tm, 128), lambda i: (i, 0))],
        out_specs=[pl.BlockSpec((tm, 128), lambda i: (i, 0)),
                   pl.BlockSpec((None, N_EXPERTS, tm), lambda i: (i, 0, 0)),
                   pl.BlockSpec((None, 1, 128), lambda i: (i, 0, 0))],
        out_shape=[jax.ShapeDtypeStruct((T, 128), F32),
                   jax.ShapeDtypeStruct((nt, N_EXPERTS, tm), F32),
                   jax.ShapeDtypeStruct((nt, 1, 128), F32)],
        compiler_params=_cparams("parallel"), name="moe_route",
    )(cb)
    return rk, rkt, cnt[:, 0, :N_EXPERTS].astype(jnp.int32).reshape(-1)


MOE_CHUNK = 128


def _moe_kernel(cnt_ref, h_ref, x_ref, mod_ref, cb_ref, rk_ref, rkt_ref, w1_ref, w3_ref, w2_ref,
                o_ref, xg_ref, y_ref):
    i, e, j = pl.program_id(0), pl.program_id(1), pl.program_id(2)
    ne, nj = pl.num_programs(1), pl.num_programs(2)
    tm = h_ref.shape[0]
    ch = MOE_CHUNK
    nch = (cnt_ref[i * ne + e] + (ch - 1)) // ch

    def rows(c):
        return pl.ds(pl.multiple_of(c * ch, ch), ch)

    @pl.when((e == 0) & (j == 0))
    def _():
        o_ref[...] = jnp.zeros_like(o_ref)
        y_ref[...] = jnp.zeros_like(y_ref)

    @pl.when(j == 0)
    def _():
        rid = lax.broadcasted_iota(jnp.int32, (ch, tm), 0).astype(F32)
        rrow = rkt_ref[pl.ds(e, 1), :]

        def gather(c, _):
            sel = jnp.where(rid == rrow - (c * ch).astype(F32), 1.0, 0.0).astype(BF16)
            xg_ref[rows(c), :] = _dot(sel, h_ref[...]).astype(BF16)
            return 0

        lax.fori_loop(0, nch, gather, 0)

    def expert(c, _):
        xg = xg_ref[rows(c), :]
        a = _dot(xg, w1_ref[0])
        t = (a * jax.nn.sigmoid(a) * _dot(xg, w3_ref[0])).astype(BF16)
        part = _dot(t, w2_ref[0])

        @pl.when(j == 0)
        def _():
            y_ref[rows(c), :] = part

        @pl.when(j > 0)
        def _():
            y_ref[rows(c), :] += part

        return 0

    lax.fori_loop(0, nch, expert, 0)

    @pl.when(j == nj - 1)
    def _():
        el = lax.broadcasted_iota(jnp.int32, (tm, 128), 1)
        rcol = jnp.sum(jnp.where(el == e, rk_ref[...], 0.0), axis=-1, keepdims=True)
        wcol = jnp.sum(jnp.where(el == e, cb_ref[...], 0.0), axis=-1, keepdims=True)
        cid = lax.broadcasted_iota(jnp.int32, (tm, 2 * ch), 1).astype(F32)

        def scatter(c, _):
            r0 = pl.multiple_of(c * (2 * ch), 2 * ch)
            selt = jnp.where(cid == rcol - r0.astype(F32), 1.0, 0.0).astype(BF16)
            o_ref[...] += wcol * _dot(selt, y_ref[pl.ds(r0, 2 * ch), :].astype(BF16))
            return 0

        lax.fori_loop(0, (nch + 1) // 2, scatter, 0)

    @pl.when((e == ne - 1) & (j == nj - 1))
    def _():
        o_ref[...] = x_ref[...] + mod_ref[:, 5 * D:6 * D] * o_ref[...]


def _moe(h2, x, modg, cb, w1, w3, w2, L, tm, tf):
    T = x.shape[0]
    ne, _, dff = w1.shape
    rk, rkt, cnt = _route(cb, tm)
    nmod = modg.shape[0]
    grid_spec = pltpu.PrefetchScalarGridSpec(
        num_scalar_prefetch=1, grid=(T // tm, ne, dff // tf),
        in_specs=[pl.BlockSpec((tm, D), lambda i, e, j, c: (i, 0)),
                  pl.BlockSpec((tm, D), lambda i, e, j, c: (i, 0)),
                  pl.BlockSpec((None, 1, 6 * D), lambda i, e, j, c: ((i * tm // L) % nmod, 0, 0)),
                  pl.BlockSpec((tm, 128), lambda i, e, j, c: (i, 0)),
                  pl.BlockSpec((tm, 128), lambda i, e, j, c: (i, 0)),
                  pl.BlockSpec((None, N_EXPERTS, tm), lambda i, e, j, c: (i, 0, 0)),
                  pl.BlockSpec((1, D, tf), lambda i, e, j, c: (e, 0, j)),
                  pl.BlockSpec((1, D, tf), lambda i, e, j, c: (e, 0, j)),
                  pl.BlockSpec((1, tf, D), lambda i, e, j, c: (e, j, 0))],
        out_specs=pl.BlockSpec((tm, D), lambda i, e, j, c: (i, 0)),
        scratch_shapes=[pltpu.VMEM((tm, D), BF16), pltpu.VMEM((tm, D), F32)])
    return pl.pallas_call(
        _moe_kernel, grid_spec=grid_spec,
        out_shape=jax.ShapeDtypeStruct((T, D), F32),
        compiler_params=_cparams("parallel", "arbitrary", "arbitrary"), name="moe",
    )(cnt, h2, x, modg, cb, rk, rkt, w1, w3, w2)


def _final_kernel(x_ref, g_ref, o_ref):
    o_ref[...] = _rms(x_ref[...], g_ref[...])


def _final_norm(x, g, tm):
    T = x.shape[0]
    return pl.pallas_call(
        _final_kernel, grid=(T // tm,),
        in_specs=[pl.BlockSpec((tm, D), lambda i: (i, 0)), pl.BlockSpec((1, D), lambda i: (0, 0))],
        out_specs=pl.BlockSpec((tm, D), lambda i: (i, 0)),
        out_shape=jax.ShapeDtypeStruct((T, D), F32),
        compiler_params=_cparams("parallel"), name="final_norm",
    )(x, g.reshape(1, D))


def _layer(x, modg, lp, l, B, L, ctx):
    is_ctx = ctx is None
    T = B * L
    Bp = -(-B // 8) * 8
    tm = min(L, 512)
    rope_tabs = None if is_ctx else _rope_tables(L, F32)
    u_tm, uv, z, q, k, v = _inproj(x, modg, lp['norm_mix'], lp['w_in'], B, L, tm, rope_tabs)

    u3 = jnp.pad(u_tm.reshape(L, B, BR), ((0, 0), (0, Bp - B), (0, 0)))
    if is_ctx:
        h0 = jnp.zeros((2, Bp, 2 * S5_STATES), F32)
    else:
        h0 = ctx[2]
        h0 = jnp.transpose(h0, (1, 0, 4, 2, 3)).reshape(2, B, 2 * S5_STATES)
        h0 = jnp.pad(h0, ((0, 0), (0, Bp - B), (0, 0)))
    yf, yb, hfin = _s5(u3, h0, lp['s5_wb'], lp['s5_wc'], lp['s5_a'], L, Bp, min(L, 128))

    tabs, hf = lp['hy'][L]
    if L <= HY_DIRECT_MAX:
        y_hy = _hy_direct(z, lp['hy_conv_w'], lp['hy_conv_b'], lp['hy_bias'], tabs, hf, L)
    else:
        hv, hx1, hx2 = _short_conv(z, lp['hy_conv_w'], lp['hy_conv_b'], L, tm)
        y1 = _hy_split(hv.reshape(B, L, BR), hx1.reshape(B, L, BR), lp['hy_bias'][0], hf[0], tabs, L)
        y_hy = _hy_split(y1, hx2.reshape(B, L, BR), lp['hy_bias'][1], hf[1], tabs, L).reshape(T, BR)

    q3 = q.reshape(B, L, BR)
    k3 = k.reshape(B, L, BR)
    v3 = v.reshape(B, L, BR)
    kt, vb = _attn_inputs(k3, v3, None if is_ctx else ctx)
    da = _attn(lp['da_lam'], q3, kt, vb, lp['da_subln'], min(L, 256), 1.0 - lp['lam_init'])

    x, h2, cb = _mixout(x, modg, u_tm, yf.reshape(L, Bp * BR), yb.reshape(L, Bp * BR), uv,
                        y_hy, da.reshape(T, BR), lp, B, L, tm, l % 2 == 1)
    tmf = min(T, 1024)
    if l % 2 == 0:
        x = _ffn(h2, x, modg, lp['ffn_w1'], lp['ffn_w3'], lp['ffn_w2'], L, tmf,
                 lp['ffn_w1'].shape[1] // 2)
    else:
        x = _moe(h2, x, modg, cb, lp['moe_w1'], lp['moe_w3'], lp['moe_w2'], L, tmf, 1792)
    if is_ctx:
        fin = hfin[:, :B].reshape(2, B, 2, S5_G, S5_N)
        fin = jnp.transpose(fin, (1, 0, 3, 4, 2))
        return x, k3.reshape(B, L, HEADS, 2 * HEAD_DIM), v3.reshape(B, L, HEADS, VAL_DIM), fin
    return x


def kernel(x_prompt, x_sample, cache_k, cache_v, state_ssm, c, c_ctx, w_ada, b_ada, norm_mix, norm_ffn, w_in, w_out, branch_norm, s5_lam_re, s5_lam_im, s5_log_dt, s5_b_re, s5_b_im, s5_c_re, s5_c_im, s5_d, s5_w_glu, s5_b_glu, sgu_norm, sgu_w_s, sgu_b_s, hy_conv_w, hy_conv_b, hy_w1, hy_b1, hy_w2, hy_b2, hy_w3, hy_bias, da_lq1, da_lk1, da_lq2, da_lk2, da_subln, ffn_w1, ffn_w3, ffn_w2, moe_router, moe_w1, moe_w3, moe_w2, norm_final):
    depth = w_in.shape[0]
    Bc, Lc, _ = x_prompt.shape
    Bs, Ls, _ = x_sample.shape

    cond = jnp.concatenate([c_ctx[None, :], c], axis=0)
    cond8 = jnp.pad(cond, ((0, 8 - cond.shape[0]), (0, 0)))
    mod = _ada(cond8, w_ada, b_ada)

    def hyena_tables(L):
        return _hy_direct_tables(L) if L <= HY_DIRECT_MAX else _hy_split_tables(L)

    def hyena_spectrum(L, tabs, *hy_args):
        filt = _hy_filters(L, *hy_args)
        if L <= HY_DIRECT_MAX:
            return _hy_spec_direct(filt, tabs[0])
        return _hy_spec_split(filt.reshape(L, 2, BR).transpose(1, 0, 2), tabs, L)

    hy_tabs = {L: hyena_tables(L) for L in {Lc, Ls}}

    layers = []
    for l in range(depth):
        j = l // 2
        wb, wc, a = _s5_prep(s5_lam_re[l], s5_lam_im[l], s5_log_dt[l], s5_b_re[l], s5_b_im[l],
                             s5_c_re[l], s5_c_im[l])
        lam_init = 0.8 - 0.6 * math.exp(-0.3 * l)
        lam = (jnp.exp(jnp.sum(da_lq1[l] * da_lk1[l])) - jnp.exp(jnp.sum(da_lq2[l] * da_lk2[l]))
               + lam_init)
        hy_args = (hy_w1[l], hy_b1[l], hy_w2[l], hy_b2[l], hy_w3[l])
        lp = dict(
            norm_mix=norm_mix[l], w_in=w_in[l].astype(BF16), w_out=w_out[l].astype(BF16),
            norm_ffn=norm_ffn[l].reshape(1, D), branch_norm=branch_norm[l],
            s5_wb=wb, s5_wc=wc, s5_a=a, s5_d=s5_d[l].reshape(1, BR),
            s5_w_glu=s5_w_glu[l].astype(BF16), s5_b_glu=s5_b_glu[l].reshape(1, BR),
            sgu_norm=sgu_norm[l].reshape(1, BR), sgu_w_s=sgu_w_s[l].astype(BF16),
            sgu_b=jnp.repeat(sgu_b_s[l].T, BR // SGU_HEADS, axis=1),
            hy_conv_w=hy_conv_w[l], hy_conv_b=hy_conv_b[l], hy_bias=hy_bias[l],
            hy={L: (t, hyena_spectrum(L, t, *hy_args)) for L, t in hy_tabs.items()},
            da_lam=lam, lam_init=lam_init, da_subln=da_subln[l],
        )
        if l % 2 == 0:
            lp.update(ffn_w1=ffn_w1[j].astype(BF16), ffn_w3=ffn_w3[j].astype(BF16),
                      ffn_w2=ffn_w2[j].astype(BF16),
                      router=jnp.zeros((D, 128), F32))
        else:
            lp.update(moe_w1=moe_w1[j].astype(BF16), moe_w3=moe_w3[j].astype(BF16),
                      moe_w2=moe_w2[j].astype(BF16),
                      router=jnp.pad(moe_router[j], ((0, 0), (0, 128 - N_EXPERTS))))
        layers.append(lp)

    xc = x_prompt.reshape(Bc * Lc, D)
    ks, vs, ss = [], [], []
    for l in range(depth):
        xc, k_l, v_l, s_l = _layer(xc, mod[l, 0:1].reshape(1, 1, 6 * D), layers[l], l, Bc, Lc, None)
        ks.append(k_l)
        vs.append(v_l)
        ss.append(s_l)
    y_prompt = _final_norm(xc, norm_final, 512).reshape(Bc, Lc, D)

    xs = x_sample.reshape(Bs * Ls, D)
    for l in range(depth):
        xs = _layer(xs, mod[l, 1:1 + Bs].reshape(Bs, 1, 6 * D), layers[l], l, Bs, Ls,
                    (cache_k[:, l], cache_v[:, l], state_ssm[:, l]))
    y_sample = _final_norm(xs, norm_final, 512).reshape(Bs, Ls, D)
    return (y_prompt, y_sample, jnp.stack(ks, axis=1), jnp.stack(vs, axis=1), jnp.stack(ss, axis=1))
```

```python
import functools
import math

import jax
import jax.numpy as jnp
import numpy as np
from jax import lax
from jax.experimental import pallas as pl
from jax.experimental.pallas import tpu as pltpu

F32 = jnp.float32
BF16 = jnp.bfloat16

D = 1024
BR = 256
PROJ = 9 * BR
S5_G, S5_N, S5_P = 16, 64, 16
S5_STATES = S5_G * S5_N
SGU_CHUNK, SGU_HEADS = 128, 4
HEADS, HEAD_DIM, VAL_DIM = 4, 32, 64
GRID_WIDTH = 64
ROPE_DIM = HEAD_DIM // 2
ROPE_BASE = 10000.0
HY_EMB, HY_BANDS = 33, 16
HY_DECAY_TARGET, HY_FAST, HY_SLOW = 1e-2, 0.3, 1.5
N_EXPERTS = 8
EPS = 1e-6
LOG2E = 1.4426950408889634

HY_DIRECT_MAX = 512
VMEM_LIMIT = 56 * 1024 * 1024


def _cparams(*sem):
    return pltpu.CompilerParams(dimension_semantics=sem, vmem_limit_bytes=VMEM_LIMIT)


def _dot(a, b):
    return jnp.dot(a, b, preferred_element_type=F32)


def _dot3(a, b):
    a_hi = a.astype(BF16)
    b_hi = b.astype(BF16)
    a_lo = (a - a_hi.astype(F32)).astype(BF16)
    b_lo = (b - b_hi.astype(F32)).astype(BF16)
    return _dot(a_hi, b_hi) + (_dot(a_hi, b_lo) + _dot(a_lo, b_hi))


def _rms(x, g):
    return x * lax.rsqrt(jnp.mean(x * x, axis=-1, keepdims=True) + EPS) * g


def _mod_spec(modg, tm, L):
    nmod = modg.shape[0]
    return pl.BlockSpec((None, 1, 6 * D), lambda i, *_: ((i * tm // L) % nmod, 0, 0))


def _ada_kernel(c_ref, w_ref, b_ref, o_ref):
    c = c_ref[...]
    s = (c * jax.nn.sigmoid(c)).astype(BF16)
    o_ref[0] = _dot(s, w_ref[0].astype(BF16)) + b_ref[0]


def _ada(cond8, w_ada, b_ada):
    depth = w_ada.shape[0]
    tn = 1536
    return pl.pallas_call(
        _ada_kernel,
        grid=(depth, 6 * D // tn),
        in_specs=[pl.BlockSpec((8, D), lambda l, j: (0, 0)),
                  pl.BlockSpec((1, D, tn), lambda l, j: (l, 0, j)),
                  pl.BlockSpec((1, 1, tn), lambda l, j: (l, 0, j))],
        out_specs=pl.BlockSpec((1, 8, tn), lambda l, j: (l, 0, j)),
        out_shape=jax.ShapeDtypeStruct((depth, 8, 6 * D), F32),
        compiler_params=_cparams("parallel", "parallel"),
        name="ada",
    )(cond8, w_ada, b_ada.reshape(depth, 1, 6 * D))


def _inproj_kernel(*refs, rope):
    if rope:
        (x_ref, mod_ref, g_ref, w_ref, cos_ref, sin_ref,
         u_ref, uv_ref, z_ref, q_ref, k_ref, v_ref) = refs
    else:
        x_ref, mod_ref, g_ref, w_ref, u_ref, uv_ref, z_ref, q_ref, k_ref, v_ref = refs
    x = x_ref[...]
    h = _rms(x, g_ref[...]) * (1.0 + mod_ref[:, D:2 * D]) + mod_ref[:, 0:D]
    p = _dot(h.astype(BF16), w_ref[...])
    u_ref[...] = p[:, 0:BR]
    uv_ref[...] = p[:, BR:3 * BR]
    z_ref[...] = p[:, 3 * BR:6 * BR]
    q = p[:, 6 * BR:7 * BR]
    k = p[:, 7 * BR:8 * BR]
    if rope:
        cs = cos_ref[...]
        sn = sin_ref[...]
        lane = lax.broadcasted_iota(jnp.int32, q.shape, 1)
        first = (lane % (2 * (ROPE_DIM // 2))) < (ROPE_DIM // 2)
        half = ROPE_DIM // 2

        def rot(t):
            return jnp.where(first, pltpu.roll(t, BR - half, 1), pltpu.roll(t, half, 1))

        q = q * cs + rot(q) * sn
        k = k * cs + rot(k) * sn
    q_ref[...] = q
    k_ref[...] = k
    v_ref[...] = p[:, 8 * BR:9 * BR]


def _inproj(x, modg, g, w_bf, l, B, L, tm, rope_tabs):
    T = B * L
    nt = L // tm
    rope = rope_tabs is not None
    in_specs = [pl.BlockSpec((tm, D), lambda i: (i, 0)),
                _mod_spec(modg, tm, L),
                pl.BlockSpec((1, D), lambda i: (0, 0)),
                pl.BlockSpec((None, D, PROJ), lambda i: (l, 0, 0))]
    args = [x, modg, g.reshape(1, D), w_bf]
    if rope:
        in_specs += [pl.BlockSpec((tm, BR), lambda i: (i % nt, 0))] * 2
        args += list(rope_tabs)
    tok = lambda n: pl.BlockSpec((tm, n), lambda i: (i, 0))
    out_specs = [pl.BlockSpec((tm, BR), lambda i: (i % nt, i // nt)),
                 tok(2 * BR), tok(3 * BR), tok(BR), tok(BR), tok(BR)]
    out_shape = [jax.ShapeDtypeStruct((L, B * BR), F32),
                 jax.ShapeDtypeStruct((T, 2 * BR), F32),
                 jax.ShapeDtypeStruct((T, 3 * BR), F32),
                 jax.ShapeDtypeStruct((T, BR), F32),
                 jax.ShapeDtypeStruct((T, BR), F32),
                 jax.ShapeDtypeStruct((T, BR), F32)]
    return pl.pallas_call(
        functools.partial(_inproj_kernel, rope=rope),
        grid=(T // tm,), in_specs=in_specs, out_specs=out_specs, out_shape=out_shape,
        compiler_params=_cparams("parallel"), name="inproj",
    )(*args)


def _s5_kernel(uf_ref, ub_ref, h0_ref, wb_ref, wc_ref, a_ref, yf_ref, yb_ref, hfin_ref,
               xs_ref, hc_ref, *, tt):
    i = pl.program_id(1)
    last = pl.num_programs(1) - 1
    half = S5_STATES // 2

    @pl.when(i == 0)
    def _():
        hc_ref[...] = h0_ref[...]

    for d in range(2):
        u_ref = uf_ref if d == 0 else ub_ref
        y_ref = yf_ref if d == 0 else yb_ref
        u2 = u_ref[...].reshape(tt * 8, BR).astype(BF16)
        xs_ref[...] = _dot(u2, wb_ref[d])
        for c in range(2):
            cr = slice(c * half, (c + 1) * half)
            ci = slice(S5_STATES + c * half, S5_STATES + (c + 1) * half)
            ar = jnp.broadcast_to(a_ref[d, 0:1, cr], (8, half))
            ai = jnp.broadcast_to(a_ref[d, 1:2, cr], (8, half))

            def body(s, carry, cr=cr, ci=ci, ar=ar, ai=ai, d=d):
                hr, hi = carry
                t = s if d == 0 else tt - 1 - s
                r0 = pl.multiple_of(t * 8, 8)
                nr = ar * hr - ai * hi + xs_ref[pl.ds(r0, 8), cr]
                ni = ar * hi + ai * hr + xs_ref[pl.ds(r0, 8), ci]
                xs_ref[pl.ds(r0, 8), cr] = nr
                xs_ref[pl.ds(r0, 8), ci] = ni
                return nr, ni

            hr, hi = lax.fori_loop(0, tt, body, (hc_ref[d, :, cr], hc_ref[d, :, ci]), unroll=4)
            hc_ref[d, :, cr] = hr
            hc_ref[d, :, ci] = hi
        y = _dot(xs_ref[...].astype(BF16), wc_ref[d])
        y_ref[...] = y.reshape(tt, 8, BR)

    @pl.when(i == last)
    def _():
        hfin_ref[...] = hc_ref[...]


def _s5(u_tm, h0, wb, wc, a, L, Bp, tt):
    nT = L // tt
    ng = Bp // 8
    blk = lambda f: pl.BlockSpec((tt, 8, BR), f)
    const = lambda shp: pl.BlockSpec(shp, lambda g, i: (0,) * len(shp))
    return pl.pallas_call(
        functools.partial(_s5_kernel, tt=tt),
        grid=(ng, nT),
        in_specs=[blk(lambda g, i: (i, g, 0)), blk(lambda g, i: (nT - 1 - i, g, 0)),
                  pl.BlockSpec((2, 8, 2 * S5_STATES), lambda g, i: (0, g, 0)),
                  const((2, BR, 2 * S5_STATES)), const((2, 2 * S5_STATES, BR)),
                  const((2, 2, S5_STATES))],
        out_specs=[blk(lambda g, i: (i, g, 0)), blk(lambda g, i: (nT - 1 - i, g, 0)),
                   pl.BlockSpec((2, 8, 2 * S5_STATES), lambda g, i: (0, g, 0))],
        out_shape=[jax.ShapeDtypeStruct((L, Bp, BR), F32),
                   jax.ShapeDtypeStruct((L, Bp, BR), F32),
                   jax.ShapeDtypeStruct((2, Bp, 2 * S5_STATES), F32)],
        scratch_shapes=[pltpu.VMEM((tt * 8, 2 * S5_STATES), F32),
                        pltpu.VMEM((2, 8, 2 * S5_STATES), F32)],
        compiler_params=_cparams("parallel", "arbitrary"), name="s5",
    )(u_tm, u_tm, h0, wb, wc, a)


def _s5_prep(lam_re, lam_im, log_dt, b_re, b_im, c_re, c_im):
    dt = jnp.exp(log_dt)[..., None]
    mag = jnp.exp(lam_re * dt)
    lb_re = mag * jnp.cos(lam_im * dt)
    lb_im = mag * jnp.sin(lam_im * dt)
    den = lam_re * lam_re + lam_im * lam_im
    nr = lb_re - 1.0
    coef_re = ((nr * lam_re + lb_im * lam_im) / den)[..., None]
    coef_im = ((lb_im * lam_re - nr * lam_im) / den)[..., None]
    bp_re = coef_re * b_re - coef_im * b_im
    bp_im = coef_re * b_im + coef_im * b_re
    eye = jnp.eye(S5_G, dtype=lam_re.dtype)

    def blockdiag_in(b):
        return jnp.einsum('dgnp,gh->dgphn', b, eye).reshape(2, BR, S5_STATES)

    def blockdiag_out(c):
        return jnp.einsum('dgpn,gh->dgnhp', c, eye).reshape(2, S5_STATES, BR)

    wb = jnp.concatenate([blockdiag_in(bp_re), blockdiag_in(bp_im)], axis=-1)
    wc = jnp.concatenate([blockdiag_out(c_re), -blockdiag_out(c_im)], axis=1)
    a = jnp.stack([lb_re.reshape(2, S5_STATES), lb_im.reshape(2, S5_STATES)], axis=1)
    return wb.astype(BF16), wc.astype(BF16), a


def _short_kernel(z_ref, zp_ref, zn_ref, w_ref, b_ref, v_ref, x1_ref, x2_ref, *, nt):
    j = pl.program_id(0) % nt
    z = z_ref[...]
    tm = z.shape[0]
    row = lax.broadcasted_iota(jnp.int32, z.shape, 0)
    prev_row = jnp.where(j > 0, zp_ref[7:8, :], 0.0)
    next_row = jnp.where(j < nt - 1, zn_ref[0:1, :], 0.0)
    zprev = jnp.where(row == 0, prev_row, pltpu.roll(z, 1, 0))
    znext = jnp.where(row == tm - 1, next_row, pltpu.roll(z, tm - 1, 0))
    y = zprev * w_ref[0:1, :] + z * w_ref[1:2, :] + znext * w_ref[2:3, :] + b_ref[...]
    v_ref[...] = y[:, 0:BR]
    x1_ref[...] = y[:, BR:2 * BR]
    x2_ref[...] = y[:, 2 * BR:3 * BR]


def _short_conv(z, w, b, L, tm):
    T = z.shape[0]
    nt = L // tm
    r8 = tm // 8
    nblk8 = T // 8
    out = jax.ShapeDtypeStruct((T, BR), F32)
    return pl.pallas_call(
        functools.partial(_short_kernel, nt=nt),
        grid=(T // tm,),
        in_specs=[pl.BlockSpec((tm, 3 * BR), lambda i: (i, 0)),
                  pl.BlockSpec((8, 3 * BR), lambda i: (jnp.maximum(i * r8 - 1, 0), 0)),
                  pl.BlockSpec((8, 3 * BR), lambda i: (jnp.minimum((i + 1) * r8, nblk8 - 1), 0)),
                  pl.BlockSpec((3, 3 * BR), lambda i: (0, 0)),
                  pl.BlockSpec((1, 3 * BR), lambda i: (0, 0))],
        out_specs=[pl.BlockSpec((tm, BR), lambda i: (i, 0))] * 3,
        out_shape=[out, out, out],
        compiler_params=_cparams("parallel"), name="short_conv",
    )(z, z, z, w, b.reshape(1, 3 * BR))


def _cs(num, den):
    th = (2.0 * math.pi / den) * (num % den).astype(F32)
    return jnp.cos(th), jnp.sin(th)


def _iota(m):
    return jnp.arange(m, dtype=jnp.int32)


def _hy_direct_tables(L):
    n = 2 * L
    c, s = _cs(_iota(n)[:, None] * _iota(L)[None, :], n)
    wf = jnp.concatenate([c, -s], axis=0)
    c, s = _cs((_iota(L)[:, None] + L // 2) * _iota(n)[None, :], n)
    wi = jnp.concatenate([c, -s], axis=1) * (1.0 / n)
    return wf.astype(BF16), wi.astype(BF16)


def _hy_spec_direct_kernel(f_ref, wf_ref, o_ref):
    o_ref[...] = _dot(wf_ref[...], f_ref[...].astype(BF16))


def _hy_spec_direct(filt, wf):
    L, C = filt.shape
    spec = pl.pallas_call(
        _hy_spec_direct_kernel,
        out_shape=jax.ShapeDtypeStruct((4 * L, C), F32), name="hy_spec_direct",
    )(filt, wf)
    return spec.reshape(2, 2 * L, 2, BR).transpose(2, 0, 1, 3)


def _hy_direct_kernel(z_ref, cw_ref, cb_ref, wf_ref, wi_ref, hf_ref, bias_ref, o_ref, *, L, nseq):
    n = 2 * L
    row = lax.broadcasted_iota(jnp.int32, (L, 3 * BR), 0)
    for s in range(nseq):
        z = z_ref[s * L:(s + 1) * L, :]
        zprev = jnp.where(row == 0, 0.0, pltpu.roll(z, 1, 0))
        znext = jnp.where(row == L - 1, 0.0, pltpu.roll(z, L - 1, 0))
        zc = zprev * cw_ref[0:1, :] + z * cw_ref[1:2, :] + znext * cw_ref[2:3, :] + cb_ref[...]
        y = zc[:, 0:BR]
        for o in range(2):
            x = _dot(wf_ref[...], y.astype(BF16))
            xr, xi = x[0:n], x[n:2 * n]
            hr, hi = hf_ref[o, 0], hf_ref[o, 1]
            yc = jnp.concatenate([xr * hr - xi * hi, xr * hi + xi * hr], axis=0).astype(BF16)
            y = zc[:, (o + 1) * BR:(o + 2) * BR] * (_dot(wi_ref[...], yc) + y * bias_ref[o:o + 1, :])
        o_ref[s * L:(s + 1) * L, :] = y


def _hy_direct(z, cw, cb, bias, tabs, hf, L):
    T = z.shape[0]
    wf, wi = tabs
    nseq = 4
    const = lambda shp: pl.BlockSpec(shp, lambda i: (0,) * len(shp))
    return pl.pallas_call(
        functools.partial(_hy_direct_kernel, L=L, nseq=nseq), grid=(T // (nseq * L),),
        in_specs=[pl.BlockSpec((nseq * L, 3 * BR), lambda i: (i, 0)),
                  const((3, 3 * BR)), const((1, 3 * BR)), const((4 * L, L)), const((L, 4 * L)),
                  const((2, 2, 2 * L, BR)), const((2, BR))],
        out_specs=pl.BlockSpec((nseq * L, BR), lambda i: (i, 0)),
        out_shape=jax.ShapeDtypeStruct((T, BR), F32),
        compiler_params=_cparams("parallel"), name="hy_direct",
    )(z, cw, cb.reshape(1, 3 * BR), wf, wi, hf, bias)


HY_N1 = 16
HY_K1 = HY_N1 // 2 + 1
HY_HALVES = 1


def _hy_split_tables(L):
    n = 2 * L
    n2 = n // HY_N1
    hk = n2 // HY_HALVES
    k1 = _iota(HY_K1)[:, None, None]
    k2 = _iota(n2)[None, :, None]
    j2 = _iota(n2)[None, None, :]
    c, s = _cs(j2 * k2 * HY_N1 + j2 * k1, n)
    top = jnp.concatenate([c, s], axis=2).reshape(HY_K1, HY_HALVES, hk, 2 * n2)
    bot = jnp.concatenate([-s, c], axis=2).reshape(HY_K1, HY_HALVES, hk, 2 * n2)
    wf = jnp.concatenate([top, bot], axis=2)
    ct = jnp.swapaxes(c, 1, 2).reshape(HY_K1, n2, HY_HALVES, hk).transpose(0, 2, 1, 3)
    st = jnp.swapaxes(s, 1, 2).reshape(HY_K1, n2, HY_HALVES, hk).transpose(0, 2, 1, 3)
    wi = jnp.concatenate([jnp.concatenate([ct, -st], axis=3),
                          jnp.concatenate([st, ct], axis=3)], axis=2)
    kk = _iota(HY_K1)[:, None]
    c1, s1 = _cs(kk * _iota(HY_N1 // 2)[None, :], HY_N1)
    wgt = jnp.where((kk == 0) | (kk == HY_N1 // 2), 1.0, 2.0) / n
    co, so = _cs(kk * (_iota(HY_N1 // 2)[None, :] + HY_N1 // 4), HY_N1)
    coef = jnp.concatenate([c1, -s1, wgt * co, -wgt * so], axis=1)
    return wf.astype(BF16), wi.astype(BF16), coef.astype(F32)


def _hy_split_stage1(coef_ref, v_ref, a_ref, k1, n2):
    nin = HY_N1 // 2
    ar = ai = None
    for j in range(nin):
        xj = v_ref[0, j * n2:(j + 1) * n2, :]
        tr, ti = coef_ref[k1, j] * xj, coef_ref[k1, nin + j] * xj
        ar, ai = (tr, ti) if ar is None else (ar + tr, ai + ti)
    a_ref[0:n2, :] = ar.astype(BF16)
    a_ref[n2:2 * n2, :] = ai.astype(BF16)


def _hy_spec_split_kernel(coef_ref, f_ref, wf_ref, o_ref, a_ref, *, L):
    n2 = 2 * L // HY_N1

    @pl.when(pl.program_id(2) == 0)
    def _():
        _hy_split_stage1(coef_ref, f_ref, a_ref, pl.program_id(1), n2)

    o_ref[0, 0, 0] = _dot(wf_ref[0, 0], a_ref[...])


def _hy_split_kernel(coef_ref, v_ref, xg_ref, bias_ref, wf_ref, wi_ref, hf_ref, o_ref, a_ref, *, L):
    k1, hh = pl.program_id(1), pl.program_id(2)
    n2 = 2 * L // HY_N1
    hk = n2 // HY_HALVES
    nin = HY_N1 // 2

    @pl.when(hh == 0)
    def _():
        _hy_split_stage1(coef_ref, v_ref, a_ref, k1, n2)

    @pl.when((k1 == 0) & (hh == 0))
    def _():
        o_ref[...] = jnp.zeros_like(o_ref)

    x = _dot(wf_ref[0, 0], a_ref[...])
    xr, xi = x[0:hk], x[hk:2 * hk]
    hr, hi = hf_ref[0, 0, 0:hk, :], hf_ref[0, 0, hk:2 * hk, :]
    yc = jnp.concatenate([xr * hr - xi * hi, xr * hi + xi * hr], axis=0).astype(BF16)
    b = _dot(wi_ref[0, 0], yc)
    for o in range(nin):
        o_ref[0, o * n2:(o + 1) * n2, :] += (coef_ref[k1, 2 * nin + o] * b[0:n2]
                                             + coef_ref[k1, 3 * nin + o] * b[n2:2 * n2])

    @pl.when((k1 == HY_K1 - 1) & (hh == HY_HALVES - 1))
    def _():
        o_ref[0] = xg_ref[0] * (o_ref[0] + v_ref[0] * bias_ref[...])


def _hy_spec_split(filt2, tabs, L):
    wf, _, coef = tabs
    n2 = 2 * L // HY_N1
    hk = n2 // HY_HALVES
    return pl.pallas_call(
        functools.partial(_hy_spec_split_kernel, L=L), grid=(2, HY_K1, HY_HALVES),
        in_specs=[pl.BlockSpec(memory_space=pltpu.SMEM),
                  pl.BlockSpec((1, L, BR), lambda o, k, h: (o, 0, 0)),
                  pl.BlockSpec((1, 1, 2 * hk, 2 * n2), lambda o, k, h: (k, h, 0, 0))],
        out_specs=pl.BlockSpec((1, 1, 1, 2 * hk, BR), lambda o, k, h: (o, k, h, 0, 0)),
        out_shape=jax.ShapeDtypeStruct((2, HY_K1, HY_HALVES, 2 * hk, BR), F32),
        scratch_shapes=[pltpu.VMEM((2 * n2, BR), BF16)],
        compiler_params=_cparams("parallel", "arbitrary", "arbitrary"), name="hy_spec_split",
    )(coef, filt2, wf)


def _hy_split(v, xg, bias, hf, tabs, L):
    wf, wi, coef = tabs
    B = v.shape[0]
    n2 = 2 * L // HY_N1
    hk = n2 // HY_HALVES
    seq = pl.BlockSpec((1, L, BR), lambda b, k, h: (b, 0, 0))
    return pl.pallas_call(
        functools.partial(_hy_split_kernel, L=L), grid=(B, HY_K1, HY_HALVES),
        in_specs=[pl.BlockSpec(memory_space=pltpu.SMEM), seq, seq,
                  pl.BlockSpec((1, BR), lambda b, k, h: (0, 0)),
                  pl.BlockSpec((1, 1, 2 * hk, 2 * n2), lambda b, k, h: (k, h, 0, 0)),
                  pl.BlockSpec((1, 1, 2 * n2, 2 * hk), lambda b, k, h: (k, h, 0, 0)),
                  pl.BlockSpec((1, 1, 2 * hk, BR), lambda b, k, h: (k, h, 0, 0))],
        out_specs=seq,
        out_shape=jax.ShapeDtypeStruct((B, L, BR), F32),
        scratch_shapes=[pltpu.VMEM((2 * n2, BR), BF16)],
        compiler_params=_cparams("parallel", "arbitrary", "arbitrary"), name="hy_split",
    )(coef, v, xg, bias.reshape(1, BR), wf, wi, hf)


def _hy_filters(L, w1, b1, w2, b2, w3):
    dt = w1.dtype
    pos = jnp.arange(L, dtype=dt)
    t01 = jnp.linspace(0.0, 1.0, L, dtype=dt)[:, None]
    w = (2.0 * math.pi / L) * pos[:, None]
    bands = jnp.linspace(1e-4, HY_BANDS - 1, HY_BANDS, dtype=dt)[None, :]
    feats = jnp.concatenate([t01, jnp.cos(bands * w), -jnp.sin(bands * w)], axis=-1)
    h = jnp.sin(feats @ w1 + b1)
    h = jnp.sin(h @ w2 + b2)
    h = h @ w3
    dist = jnp.abs(pos - (L // 2)) / L
    decay = jnp.abs(jnp.linspace(math.log(HY_DECAY_TARGET) / HY_SLOW,
                                 math.log(HY_DECAY_TARGET) / HY_FAST, 2 * BR, dtype=dt))
    return h * jnp.exp(-dist[:, None] * decay[None, :])


def _attn_kernel(lam_ref, q_ref, kt_ref, v_ref, sub_ref, o_ref, *, qscale, post):
    q = (q_ref[0] * qscale).astype(BF16)
    lam = lam_ref[0, 0]
    sub = sub_ref[...]
    for h in range(HEADS):
        vh = v_ref[0, :, h * VAL_DIM:(h + 1) * VAL_DIM]
        res = []
        for m in range(2):
            c0 = h * 2 * HEAD_DIM + m * HEAD_DIM
            s = _dot(q[:, c0:c0 + HEAD_DIM], kt_ref[0, c0:c0 + HEAD_DIM, :])
            p = jnp.exp2(s - jnp.max(s, axis=-1, keepdims=True))
            l = jnp.sum(p, axis=-1, keepdims=True)
            res.append(_dot(p.astype(BF16), vh) / l)
        o = res[0] - lam * res[1]
        o = o * lax.rsqrt(jnp.mean(o * o, axis=-1, keepdims=True) + EPS) * sub * post
        o_ref[0, :, h * VAL_DIM:(h + 1) * VAL_DIM] = o


def _attn(lam, q, kt, v, subln, tq, post):
    B, L, _ = q.shape
    Lk = kt.shape[2]
    return pl.pallas_call(
        functools.partial(_attn_kernel, qscale=HEAD_DIM ** -0.5 * LOG2E, post=post),
        grid=(B, L // tq),
        in_specs=[pl.BlockSpec(memory_space=pltpu.SMEM),
                  pl.BlockSpec((1, tq, BR), lambda b, i: (b, i, 0)),
                  pl.BlockSpec((1, BR, Lk), lambda b, i: (b, 0, 0)),
                  pl.BlockSpec((1, Lk, BR), lambda b, i: (b, 0, 0)),
                  pl.BlockSpec((1, VAL_DIM), lambda b, i: (0, 0))],
        out_specs=pl.BlockSpec((1, tq, BR), lambda b, i: (b, i, 0)),
        out_shape=jax.ShapeDtypeStruct((B, L, BR), F32),
        compiler_params=_cparams("parallel", "parallel"), name="diff_attn",
    )(lam.reshape(1, 1), q, kt, v, subln.reshape(1, VAL_DIM))


def _attn_inputs(k3, v3, ctx):
    B = k3.shape[0]
    if ctx is not None:
        k3 = jnp.concatenate([ctx[0].reshape(B, -1, BR), k3], axis=1)
        v3 = jnp.concatenate([ctx[1].reshape(B, -1, BR), v3], axis=1)
    return jnp.swapaxes(k3, 1, 2).astype(BF16), v3.astype(BF16)


def _attn_latent_test(q, k, v, ck, cv):
    kt, vb = _attn_inputs(k, v, (ck, cv))
    return _attn(jnp.float32(0.5), q, kt, vb, jnp.ones((VAL_DIM,), F32), 256, 0.5)


def _rope_tables(L, dt):
    n_rows = L // GRID_WIDTH
    row = jnp.repeat(jnp.arange(n_rows, dtype=dt), GRID_WIDTH)
    col = (jnp.arange(L) % GRID_WIDTH).astype(dt)
    inv = ROPE_BASE ** (-jnp.arange(0, ROPE_DIM, 2, dtype=dt) / ROPE_DIM)
    ar = row[:, None] * inv[None, :]
    ac = col[:, None] * inv[None, :]
    ang = jnp.concatenate([ar, ar, ac, ac], axis=-1)
    sign = jnp.tile(jnp.concatenate([-jnp.ones(ROPE_DIM // 2, dt), jnp.ones(ROPE_DIM // 2, dt)]), 2)
    reps = BR // HEAD_DIM
    return jnp.tile(jnp.cos(ang), (1, reps)), jnp.tile(jnp.sin(ang) * sign, (1, reps))


def _mixout_kernel(x_ref, mod_ref, u_ref, yf_ref, yb_ref, uv_ref, hy_ref, da_ref,
                   sd_ref, wg_ref, bg_ref, sn_ref, ws_ref, bs_ref, bn_ref, wo_ref, gf_ref, rt_ref,
                   xo_ref, h2_ref, cb_ref, *, route):
    tm = x_ref.shape[0]
    u = u_ref[...]
    y = jax.nn.gelu(sd_ref[...] * u + yf_ref[...] + yb_ref[...])
    y = y * jax.nn.sigmoid(_dot(y.astype(BF16), wg_ref[...]) + bg_ref[...])
    acc = _dot(_rms(y, bn_ref[0:1, :]).astype(BF16), wo_ref[0:BR, :])
    guv = jax.nn.gelu(uv_ref[...])
    gu = guv[:, 0:BR]
    gv = _rms(guv[:, BR:2 * BR], sn_ref[...]).astype(BF16)
    lane = lax.broadcasted_iota(jnp.int32, (SGU_CHUNK, BR), 1)
    hd = BR // SGU_HEADS
    zs = []
    for c in range(tm // SGU_CHUNK):
        vc = gv[c * SGU_CHUNK:(c + 1) * SGU_CHUNK, :]
        z = bs_ref[...]
        for h in range(SGU_HEADS):
            z = z + jnp.where(lane // hd == h, _dot(ws_ref[h], vc), 0.0)
        zs.append(z)
    z = zs[0] if len(zs) == 1 else jnp.concatenate(zs, axis=0)
    acc = acc + _dot(_rms(gu * z, bn_ref[1:2, :]).astype(BF16), wo_ref[BR:2 * BR, :])
    acc = acc + _dot(_rms(hy_ref[...], bn_ref[2:3, :]).astype(BF16), wo_ref[2 * BR:3 * BR, :])
    acc = acc + _dot(da_ref[...].astype(BF16), wo_ref[3 * BR:4 * BR, :])
    xn = x_ref[...] + mod_ref[:, 2 * D:3 * D] * acc
    xo_ref[...] = xn
    h2 = _rms(xn, gf_ref[...]) * (1.0 + mod_ref[:, 4 * D:5 * D]) + mod_ref[:, 3 * D:4 * D]
    h2_ref[...] = h2.astype(BF16)
    if not route:
        cb_ref[...] = jnp.zeros_like(cb_ref)
        return
    logits = _dot3(h2, rt_ref[...])
    el = lax.broadcasted_iota(jnp.int32, logits.shape, 1)
    logits = jnp.where(el < N_EXPERTS, logits, -jnp.inf)
    e = jnp.exp(logits - jnp.max(logits, axis=-1, keepdims=True))
    probs = e / jnp.sum(e, axis=-1, keepdims=True)
    big = logits.shape[1]
    m1 = jnp.max(probs, axis=-1, keepdims=True)
    i1 = jnp.min(jnp.where(probs == m1, el, big), axis=-1, keepdims=True)
    p2 = jnp.where((el == i1) | (el >= N_EXPERTS), -1.0, probs)
    m2 = jnp.max(p2, axis=-1, keepdims=True)
    i2 = jnp.min(jnp.where(p2 == m2, el, big), axis=-1, keepdims=True)
    tot = m1 + m2
    cb_ref[...] = jnp.where(el == i1, m1 / tot, 0.0) + jnp.where(el == i2, m2 / tot, 0.0)


def _mixout(x, modg, u_tm2, yf2, yb2, uv, hy, da, lp, B, L, tm, route):
    T = B * L
    nt = L // tm
    tok = lambda n: pl.BlockSpec((tm, n), lambda i: (i, 0))
    tmaj = pl.BlockSpec((tm, BR), lambda i: (i % nt, i // nt))
    const = lambda shp: pl.BlockSpec(shp, lambda i: (0,) * len(shp))
    return pl.pallas_call(
        functools.partial(_mixout_kernel, route=route), grid=(T // tm,),
        in_specs=[tok(D), _mod_spec(modg, tm, L),
                  tmaj, tmaj, tmaj, tok(2 * BR), tok(BR), tok(BR),
                  const((1, BR)), const((BR, BR)), const((1, BR)), const((1, BR)),
                  const((SGU_HEADS, SGU_CHUNK, SGU_CHUNK)), const((SGU_CHUNK, BR)),
                  const((3, BR)), pl.BlockSpec((None, D, D), lambda i: (lp['l'], 0, 0)),
                  const((1, D)), const((D, 128))],
        out_specs=[tok(D), tok(D), tok(128)],
        out_shape=[jax.ShapeDtypeStruct((T, D), F32), jax.ShapeDtypeStruct((T, D), BF16),
                   jax.ShapeDtypeStruct((T, 128), F32)],
        compiler_params=_cparams("parallel"), name="mix_out",
    )(x, modg, u_tm2, yf2, yb2, uv, hy, da,
      lp['s5_d'], lp['s5_w_glu'], lp['s5_b_glu'], lp['sgu_norm'], lp['sgu_w_s'], lp['sgu_b'],
      lp['branch_norm'], lp['w_out'], lp['norm_ffn'], lp['router'])


def _ffn_kernel(h_ref, x_ref, mod_ref, w1_ref, w3_ref, w2_ref, o_ref, acc_ref):
    j = pl.program_id(1)
    h = h_ref[...]
    a = _dot(h, w1_ref[...])
    t = (a * jax.nn.sigmoid(a) * _dot(h, w3_ref[...])).astype(BF16)
    part = _dot(t, w2_ref[...])

    @pl.when(j == 0)
    def _():
        acc_ref[...] = part

    @pl.when(j > 0)
    def _():
        acc_ref[...] += part

    @pl.when(j == pl.num_programs(1) - 1)
    def _():
        o_ref[...] = x_ref[...] + mod_ref[:, 5 * D:6 * D] * acc_ref[...]


def _ffn(h2, x, modg, w1, w3, w2, jl, L, tm, tf):
    T = x.shape[0]
    dff = w1.shape[2]
    return pl.pallas_call(
        _ffn_kernel, grid=(T // tm, dff // tf),
        in_specs=[pl.BlockSpec((tm, D), lambda i, j: (i, 0)),
                  pl.BlockSpec((tm, D), lambda i, j: (i, 0)),
                  _mod_spec(modg, tm, L),
                  pl.BlockSpec((None, D, tf), lambda i, j: (jl, 0, j)),
                  pl.BlockSpec((None, D, tf), lambda i, j: (jl, 0, j)),
                  pl.BlockSpec((None, tf, D), lambda i, j: (jl, j, 0))],
        out_specs=pl.BlockSpec((tm, D), lambda i, j: (i, 0)),
        out_shape=jax.ShapeDtypeStruct((T, D), F32),
        scratch_shapes=[pltpu.VMEM((tm, D), F32)],
        compiler_params=_cparams("parallel", "arbitrary"), name="ffn",
    )(h2, x, modg, w1, w3, w2)


def _route_kernel(cb_ref, rk_ref, rkt_ref, cnt_ref):
    tm = cb_ref.shape[0]
    mask = cb_ref[...] > 0.0
    mf = jnp.where(mask, 1.0, 0.0)
    r = lax.broadcasted_iota(jnp.int32, (tm, tm), 0)
    c = lax.broadcasted_iota(jnp.int32, (tm, tm), 1)
    before = jnp.where(c < r, 1.0, 0.0).astype(BF16)
    rank = jnp.where(mask, _dot(before, mf.astype(BF16)), -1.0)
    rk_ref[...] = rank
    rkt_ref[...] = rank.T[0:N_EXPERTS, :]
    cnt_ref[...] = jnp.sum(mf, axis=0, keepdims=True)


def _route(cb, tm):
    T = cb.shape[0]
    nt = T // tm
    rk, rkt, cnt = pl.pallas_call(
        _route_kernel, grid=(nt,),
        in_specs=[pl.BlockSpec((tm, 128), lambda i: (i, 0))],
        out_specs=[pl.BlockSpec((tm, 128), lambda i: (i, 0)),
                   pl.BlockSpec((None, N_EXPERTS, tm), lambda i: (i, 0, 0)),
                   pl.BlockSpec((None, 1, 128), lambda i: (i, 0, 0))],
        out_shape=[jax.ShapeDtypeStruct((T, 128), F32),
                   jax.ShapeDtypeStruct((nt, N_EXPERTS, tm), F32),
                   jax.ShapeDtypeStruct((nt, 1, 128), F32)],
        compiler_params=_cparams("parallel"), name="moe_route",
    )(cb)
    return rk, rkt, cnt[:, 0, :N_EXPERTS].astype(jnp.int32).reshape(-1)


MOE_CHUNK = 128


def _moe_kernel(cnt_ref, h_ref, x_ref, mod_ref, cb_ref, rk_ref, rkt_ref, w1_ref, w3_ref, w2_ref,
                o_ref, xg_ref, y_ref):
    i, e, j = pl.program_id(0), pl.program_id(1), pl.program_id(2)
    ne, nj = pl.num_programs(1), pl.num_programs(2)
    tm = h_ref.shape[0]
    ch = MOE_CHUNK
    nch = (cnt_ref[i * ne + e] + (ch - 1)) // ch

    def rows(c):
        return pl.ds(pl.multiple_of(c * ch, ch), ch)

    @pl.when((e == 0) & (j == 0))
    def _():
        o_ref[...] = jnp.zeros_like(o_ref)
        y_ref[...] = jnp.zeros_like(y_ref)

    @pl.when(j == 0)
    def _():
        rid = lax.broadcasted_iota(jnp.int32, (ch, tm), 0).astype(F32)
        rrow = rkt_ref[pl.ds(e, 1), :]

        def gather(c, _):
            sel = jnp.where(rid == rrow - (c * ch).astype(F32), 1.0, 0.0).astype(BF16)
            xg_ref[rows(c), :] = _dot(sel, h_ref[...]).astype(BF16)
            return 0

        lax.fori_loop(0, nch, gather, 0)

    def expert(c, _):
        xg = xg_ref[rows(c), :]
        a = _dot(xg, w1_ref[0])
        t = (a * jax.nn.sigmoid(a) * _dot(xg, w3_ref[0])).astype(BF16)
        part = _dot(t, w2_ref[0])

        @pl.when(j == 0)
        def _():
            y_ref[rows(c), :] = part

        @pl.when(j > 0)
        def _():
            y_ref[rows(c), :] += part

        return 0

    lax.fori_loop(0, nch, expert, 0)

    @pl.when(j == nj - 1)
    def _():
        el = lax.broadcasted_iota(jnp.int32, (tm, 128), 1)
        rcol = jnp.sum(jnp.where(el == e, rk_ref[...], 0.0), axis=-1, keepdims=True)
        wcol = jnp.sum(jnp.where(el == e, cb_ref[...], 0.0), axis=-1, keepdims=True)
        cid = lax.broadcasted_iota(jnp.int32, (tm, 2 * ch), 1).astype(F32)

        def scatter(c, _):
            r0 = pl.multiple_of(c * (2 * ch), 2 * ch)
            selt = jnp.where(cid == rcol - r0.astype(F32), 1.0, 0.0).astype(BF16)
            o_ref[...] += wcol * _dot(selt, y_ref[pl.ds(r0, 2 * ch), :].astype(BF16))
            return 0

        lax.fori_loop(0, (nch + 1) // 2, scatter, 0)

    @pl.when((e == ne - 1) & (j == nj - 1))
    def _():
        o_ref[...] = x_ref[...] + mod_ref[:, 5 * D:6 * D] * o_ref[...]


def _moe(h2, x, modg, cb, w1, w3, w2, jl, L, tm, tf):
    T = x.shape[0]
    _, ne, _, dff = w1.shape
    rk, rkt, cnt = _route(cb, tm)
    nmod = modg.shape[0]
    grid_spec = pltpu.PrefetchScalarGridSpec(
        num_scalar_prefetch=1, grid=(T // tm, ne, dff // tf),
        in_specs=[pl.BlockSpec((tm, D), lambda i, e, j, c: (i, 0)),
                  pl.BlockSpec((tm, D), lambda i, e, j, c: (i, 0)),
                  pl.BlockSpec((None, 1, 6 * D), lambda i, e, j, c: ((i * tm // L) % nmod, 0, 0)),
                  pl.BlockSpec((tm, 128), lambda i, e, j, c: (i, 0)),
                  pl.BlockSpec((tm, 128), lambda i, e, j, c: (i, 0)),
                  pl.BlockSpec((None, N_EXPERTS, tm), lambda i, e, j, c: (i, 0, 0)),
                  pl.BlockSpec((None, 1, D, tf), lambda i, e, j, c: (jl, e, 0, j)),
                  pl.BlockSpec((None, 1, D, tf), lambda i, e, j, c: (jl, e, 0, j)),
                  pl.BlockSpec((None, 1, tf, D), lambda i, e, j, c: (jl, e, j, 0))],
        out_specs=pl.BlockSpec((tm, D), lambda i, e, j, c: (i, 0)),
        scratch_shapes=[pltpu.VMEM((tm, D), BF16), pltpu.VMEM((tm, D), F32)])
    return pl.pallas_call(
        _moe_kernel, grid_spec=grid_spec,
        out_shape=jax.ShapeDtypeStruct((T, D), F32),
        compiler_params=_cparams("parallel", "arbitrary", "arbitrary"), name="moe",
    )(cnt, h2, x, modg, cb, rk, rkt, w1, w3, w2)


def _final_kernel(x_ref, g_ref, o_ref):
    o_ref[...] = _rms(x_ref[...], g_ref[...])


def _final_norm(x, g, tm):
    T = x.shape[0]
    return pl.pallas_call(
        _final_kernel, grid=(T // tm,),
        in_specs=[pl.BlockSpec((tm, D), lambda i: (i, 0)), pl.BlockSpec((1, D), lambda i: (0, 0))],
        out_specs=pl.BlockSpec((tm, D), lambda i: (i, 0)),
        out_shape=jax.ShapeDtypeStruct((T, D), F32),
        compiler_params=_cparams("parallel"), name="final_norm",
    )(x, g.reshape(1, D))


def _layer(x, modg, lp, l, B, L, ctx):
    is_ctx = ctx is None
    T = B * L
    Bp = -(-B // 8) * 8
    tm = min(L, 512)
    rope_tabs = None if is_ctx else _rope_tables(L, F32)
    u_tm, uv, z, q, k, v = _inproj(x, modg, lp['norm_mix'], lp['w_in'], l, B, L, tm, rope_tabs)

    u3 = jnp.pad(u_tm.reshape(L, B, BR), ((0, 0), (0, Bp - B), (0, 0)))
    if is_ctx:
        h0 = jnp.zeros((2, Bp, 2 * S5_STATES), F32)
    else:
        h0 = ctx[2]
        h0 = jnp.transpose(h0, (1, 0, 4, 2, 3)).reshape(2, B, 2 * S5_STATES)
        h0 = jnp.pad(h0, ((0, 0), (0, Bp - B), (0, 0)))
    yf, yb, hfin = _s5(u3, h0, lp['s5_wb'], lp['s5_wc'], lp['s5_a'], L, Bp, min(L, 128))

    tabs, hf = lp['hy'][L]
    if L <= HY_DIRECT_MAX:
        y_hy = _hy_direct(z, lp['hy_conv_w'], lp['hy_conv_b'], lp['hy_bias'], tabs, hf, L)
    else:
        hv, hx1, hx2 = _short_conv(z, lp['hy_conv_w'], lp['hy_conv_b'], L, tm)
        y1 = _hy_split(hv.reshape(B, L, BR), hx1.reshape(B, L, BR), lp['hy_bias'][0], hf[0], tabs, L)
        y_hy = _hy_split(y1, hx2.reshape(B, L, BR), lp['hy_bias'][1], hf[1], tabs, L).reshape(T, BR)

    q3 = q.reshape(B, L, BR)
    k3 = k.reshape(B, L, BR)
    v3 = v.reshape(B, L, BR)
    kt, vb = _attn_inputs(k3, v3, None if is_ctx else ctx)
    da = _attn(lp['da_lam'], q3, kt, vb, lp['da_subln'], min(L, 256), 1.0 - lp['lam_init'])

    x, h2, cb = _mixout(x, modg, u_tm, yf.reshape(L, Bp * BR), yb.reshape(L, Bp * BR), uv,
                        y_hy, da.reshape(T, BR), lp, B, L, tm, l % 2 == 1)
    tmf = min(T, 1024)
    if l % 2 == 0:
        x = _ffn(h2, x, modg, lp['ffn_w1'], lp['ffn_w3'], lp['ffn_w2'], l // 2, L, tmf,
                 lp['ffn_w1'].shape[2] // 2)
    else:
        x = _moe(h2, x, modg, cb, lp['moe_w1'], lp['moe_w3'], lp['moe_w2'], l // 2, L, tmf, 1792)
    if is_ctx:
        fin = hfin[:, :B].reshape(2, B, 2, S5_G, S5_N)
        fin = jnp.transpose(fin, (1, 0, 3, 4, 2))
        return x, k3.reshape(B, L, HEADS, 2 * HEAD_DIM), v3.reshape(B, L, HEADS, VAL_DIM), fin
    return x


def kernel(x_prompt, x_sample, cache_k, cache_v, state_ssm, c, c_ctx, w_ada, b_ada, norm_mix, norm_ffn, w_in, w_out, branch_norm, s5_lam_re, s5_lam_im, s5_log_dt, s5_b_re, s5_b_im, s5_c_re, s5_c_im, s5_d, s5_w_glu, s5_b_glu, sgu_norm, sgu_w_s, sgu_b_s, hy_conv_w, hy_conv_b, hy_w1, hy_b1, hy_w2, hy_b2, hy_w3, hy_bias, da_lq1, da_lk1, da_lq2, da_lk2, da_subln, ffn_w1, ffn_w3, ffn_w2, moe_router, moe_w1, moe_w3, moe_w2, norm_final):
    depth = w_in.shape[0]
    Bc, Lc, _ = x_prompt.shape
    Bs, Ls, _ = x_sample.shape

    cond = jnp.concatenate([c_ctx[None, :], c], axis=0)
    cond8 = jnp.pad(cond, ((0, 8 - cond.shape[0]), (0, 0)))
    mod = _ada(cond8, w_ada, b_ada)

    def hyena_tables(L):
        return _hy_direct_tables(L) if L <= HY_DIRECT_MAX else _hy_split_tables(L)

    def hyena_spectrum(L, tabs, *hy_args):
        filt = _hy_filters(L, *hy_args)
        if L <= HY_DIRECT_MAX:
            return _hy_spec_direct(filt, tabs[0])
        return _hy_spec_split(filt.reshape(L, 2, BR).transpose(1, 0, 2), tabs, L)

    hy_tabs = {L: hyena_tables(L) for L in {Lc, Ls}}

    bf = {name: w.astype(BF16) for name, w in dict(
        w_in=w_in, w_out=w_out, ffn_w1=ffn_w1, ffn_w3=ffn_w3, ffn_w2=ffn_w2,
        moe_w1=moe_w1, moe_w3=moe_w3, moe_w2=moe_w2).items()}

    layers = []
    for l in range(depth):
        j = l // 2
        wb, wc, a = _s5_prep(s5_lam_re[l], s5_lam_im[l], s5_log_dt[l], s5_b_re[l], s5_b_im[l],
                             s5_c_re[l], s5_c_im[l])
        lam_init = 0.8 - 0.6 * math.exp(-0.3 * l)
        lam = (jnp.exp(jnp.sum(da_lq1[l] * da_lk1[l])) - jnp.exp(jnp.sum(da_lq2[l] * da_lk2[l]))
               + lam_init)
        hy_args = (hy_w1[l], hy_b1[l], hy_w2[l], hy_b2[l], hy_w3[l])
        lp = dict(
            l=l, norm_mix=norm_mix[l], norm_ffn=norm_ffn[l].reshape(1, D), branch_norm=branch_norm[l],
            s5_wb=wb, s5_wc=wc, s5_a=a, s5_d=s5_d[l].reshape(1, BR),
            s5_w_glu=s5_w_glu[l].astype(BF16), s5_b_glu=s5_b_glu[l].reshape(1, BR),
            sgu_norm=sgu_norm[l].reshape(1, BR), sgu_w_s=sgu_w_s[l].astype(BF16),
            sgu_b=jnp.repeat(sgu_b_s[l].T, BR // SGU_HEADS, axis=1),
            hy_conv_w=hy_conv_w[l], hy_conv_b=hy_conv_b[l], hy_bias=hy_bias[l],
            hy={L: (t, hyena_spectrum(L, t, *hy_args)) for L, t in hy_tabs.items()},
            da_lam=lam, lam_init=lam_init, da_subln=da_subln[l],
        )
        lp.update(bf)
        if l % 2 == 0:
            lp.update(router=jnp.zeros((D, 128), F32))
        else:
            lp.update(router=jnp.pad(moe_router[j], ((0, 0), (0, 128 - N_EXPERTS))))
        layers.append(lp)

    xc = x_prompt.reshape(Bc * Lc, D)
    ks, vs, ss = [], [], []
    for l in range(depth):
        xc, k_l, v_l, s_l = _layer(xc, mod[l, 0:1].reshape(1, 1, 6 * D), layers[l], l, Bc, Lc, None)
        ks.append(k_l)
        vs.append(v_l)
        ss.append(s_l)
    y_prompt = _final_norm(xc, norm_final, 512).reshape(Bc, Lc, D)

    xs = x_sample.reshape(Bs * Ls, D)
    for l in range(depth):
        xs = _layer(xs, mod[l, 1:1 + Bs].reshape(Bs, 1, 6 * D), layers[l], l, Bs, Ls,
                    (cache_k[:, l], cache_v[:, l], state_ssm[:, l]))
    y_sample = _final_norm(xs, norm_final, 512).reshape(Bs, Ls, D)
    return (y_prompt, y_sample, jnp.stack(ks, axis=1), jnp.stack(vs, axis=1), jnp.stack(ss, axis=1))
```

```python
import functools
import math

import jax
import jax.numpy as jnp
import numpy as np
from jax import lax
from jax.experimental import pallas as pl
from jax.experimental.pallas import tpu as pltpu

F32 = jnp.float32
BF16 = jnp.bfloat16

D = 1024
BR = 256
PROJ = 9 * BR
S5_G, S5_N, S5_P = 16, 64, 16
S5_STATES = S5_G * S5_N
SGU_CHUNK, SGU_HEADS = 128, 4
HEADS, HEAD_DIM, VAL_DIM = 4, 32, 64
GRID_WIDTH = 64
ROPE_DIM = HEAD_DIM // 2
ROPE_BASE = 10000.0
HY_EMB, HY_BANDS = 33, 16
HY_DECAY_TARGET, HY_FAST, HY_SLOW = 1e-2, 0.3, 1.5
N_EXPERTS = 8
EPS = 1e-6
LOG2E = 1.4426950408889634

HY_DIRECT_MAX = 512
VMEM_LIMIT = 56 * 1024 * 1024


def _cparams(*sem):
    return pltpu.CompilerParams(dimension_semantics=sem, vmem_limit_bytes=VMEM_LIMIT)


def _dot(a, b):
    return jnp.dot(a, b, preferred_element_type=F32)


def _dot3(a, b):
    a_hi = a.astype(BF16)
    b_hi = b.astype(BF16)
    a_lo = (a - a_hi.astype(F32)).astype(BF16)
    b_lo = (b - b_hi.astype(F32)).astype(BF16)
    return _dot(a_hi, b_hi) + (_dot(a_hi, b_lo) + _dot(a_lo, b_hi))


def _rms(x, g):
    return x * lax.rsqrt(jnp.mean(x * x, axis=-1, keepdims=True) + EPS) * g


def _mod_spec(modg, tm, L):
    nmod = modg.shape[0]
    return pl.BlockSpec((None, 1, 6 * D), lambda i, *_: ((i * tm // L) % nmod, 0, 0))


def _ada_kernel(c_ref, w_ref, b_ref, o_ref):
    c = c_ref[...]
    s = (c * jax.nn.sigmoid(c)).astype(BF16)
    o_ref[0] = _dot(s, w_ref[0].astype(BF16)) + b_ref[0]


def _ada(cond8, w_ada, b_ada):
    depth = w_ada.shape[0]
    tn = 1536
    return pl.pallas_call(
        _ada_kernel,
        grid=(depth, 6 * D // tn),
        in_specs=[pl.BlockSpec((8, D), lambda l, j: (0, 0)),
                  pl.BlockSpec((1, D, tn), lambda l, j: (l, 0, j)),
                  pl.BlockSpec((1, 1, tn), lambda l, j: (l, 0, j))],
        out_specs=pl.BlockSpec((1, 8, tn), lambda l, j: (l, 0, j)),
        out_shape=jax.ShapeDtypeStruct((depth, 8, 6 * D), F32),
        compiler_params=_cparams("parallel", "parallel"),
        name="ada",
    )(cond8, w_ada, b_ada.reshape(depth, 1, 6 * D))


def _inproj_kernel(*refs, rope):
    if rope:
        (x_ref, mod_ref, g_ref, w_ref, cos_ref, sin_ref,
         u_ref, uv_ref, z_ref, q_ref, k_ref, v_ref) = refs
    else:
        x_ref, mod_ref, g_ref, w_ref, u_ref, uv_ref, z_ref, q_ref, k_ref, v_ref = refs
    x = x_ref[...]
    h = _rms(x, g_ref[...]) * (1.0 + mod_ref[:, D:2 * D]) + mod_ref[:, 0:D]
    p = _dot(h.astype(BF16), w_ref[...])
    u_ref[...] = p[:, 0:BR]
    uv_ref[...] = p[:, BR:3 * BR]
    z_ref[...] = p[:, 3 * BR:6 * BR]
    q = p[:, 6 * BR:7 * BR]
    k = p[:, 7 * BR:8 * BR]
    if rope:
        cs = cos_ref[...]
        sn = sin_ref[...]
        lane = lax.broadcasted_iota(jnp.int32, q.shape, 1)
        first = (lane % (2 * (ROPE_DIM // 2))) < (ROPE_DIM // 2)
        half = ROPE_DIM // 2

        def rot(t):
            return jnp.where(first, pltpu.roll(t, BR - half, 1), pltpu.roll(t, half, 1))

        q = q * cs + rot(q) * sn
        k = k * cs + rot(k) * sn
    q_ref[...] = q
    k_ref[...] = k
    v_ref[...] = p[:, 8 * BR:9 * BR]


def _inproj(x, modg, g, w_bf, l, B, L, tm, rope_tabs):
    T = B * L
    nt = L // tm
    rope = rope_tabs is not None
    in_specs = [pl.BlockSpec((tm, D), lambda i: (i, 0)),
                _mod_spec(modg, tm, L),
                pl.BlockSpec((1, D), lambda i: (0, 0)),
                pl.BlockSpec((None, D, PROJ), lambda i: (l, 0, 0))]
    args = [x, modg, g.reshape(1, D), w_bf]
    if rope:
        in_specs += [pl.BlockSpec((tm, BR), lambda i: (i % nt, 0))] * 2
        args += list(rope_tabs)
    tok = lambda n: pl.BlockSpec((tm, n), lambda i: (i, 0))
    out_specs = [pl.BlockSpec((tm, BR), lambda i: (i % nt, i // nt)),
                 tok(2 * BR), tok(3 * BR), tok(BR), tok(BR), tok(BR)]
    out_shape = [jax.ShapeDtypeStruct((L, B * BR), F32),
                 jax.ShapeDtypeStruct((T, 2 * BR), F32),
                 jax.ShapeDtypeStruct((T, 3 * BR), F32),
                 jax.ShapeDtypeStruct((T, BR), F32),
                 jax.ShapeDtypeStruct((T, BR), F32),
                 jax.ShapeDtypeStruct((T, BR), F32)]
    return pl.pallas_call(
        functools.partial(_inproj_kernel, rope=rope),
        grid=(T // tm,), in_specs=in_specs, out_specs=out_specs, out_shape=out_shape,
        compiler_params=_cparams("parallel"), name="inproj",
    )(*args)


def _s5_kernel(uf_ref, ub_ref, h0_ref, wb_ref, wc_ref, a_ref, yf_ref, yb_ref, hfin_ref,
               xs_ref, hc_ref, *, tt):
    i = pl.program_id(1)
    last = pl.num_programs(1) - 1
    half = S5_STATES // 2

    @pl.when(i == 0)
    def _():
        hc_ref[...] = h0_ref[...]

    for d in range(2):
        u_ref = uf_ref if d == 0 else ub_ref
        y_ref = yf_ref if d == 0 else yb_ref
        u2 = u_ref[...].reshape(tt * 8, BR).astype(BF16)
        xs_ref[...] = _dot(u2, wb_ref[d])
        for c in range(2):
            cr = slice(c * half, (c + 1) * half)
            ci = slice(S5_STATES + c * half, S5_STATES + (c + 1) * half)
            ar = jnp.broadcast_to(a_ref[d, 0:1, cr], (8, half))
            ai = jnp.broadcast_to(a_ref[d, 1:2, cr], (8, half))

            def body(s, carry, cr=cr, ci=ci, ar=ar, ai=ai, d=d):
                hr, hi = carry
                t = s if d == 0 else tt - 1 - s
                r0 = pl.multiple_of(t * 8, 8)
                nr = ar * hr - ai * hi + xs_ref[pl.ds(r0, 8), cr]
                ni = ar * hi + ai * hr + xs_ref[pl.ds(r0, 8), ci]
                xs_ref[pl.ds(r0, 8), cr] = nr
                xs_ref[pl.ds(r0, 8), ci] = ni
                return nr, ni

            hr, hi = lax.fori_loop(0, tt, body, (hc_ref[d, :, cr], hc_ref[d, :, ci]), unroll=4)
            hc_ref[d, :, cr] = hr
            hc_ref[d, :, ci] = hi
        y = _dot(xs_ref[...].astype(BF16), wc_ref[d])
        y_ref[...] = y.reshape(tt, 8, BR)

    @pl.when(i == last)
    def _():
        hfin_ref[...] = hc_ref[...]


def _s5(u_tm, h0, wb, wc, a, L, Bp, tt):
    nT = L // tt
    ng = Bp // 8
    blk = lambda f: pl.BlockSpec((tt, 8, BR), f)
    const = lambda shp: pl.BlockSpec(shp, lambda g, i: (0,) * len(shp))
    return pl.pallas_call(
        functools.partial(_s5_kernel, tt=tt),
        grid=(ng, nT),
        in_specs=[blk(lambda g, i: (i, g, 0)), blk(lambda g, i: (nT - 1 - i, g, 0)),
                  pl.BlockSpec((2, 8, 2 * S5_STATES), lambda g, i: (0, g, 0)),
                  const((2, BR, 2 * S5_STATES)), const((2, 2 * S5_STATES, BR)),
                  const((2, 2, S5_STATES))],
        out_specs=[blk(lambda g, i: (i, g, 0)), blk(lambda g, i: (nT - 1 - i, g, 0)),
                   pl.BlockSpec((2, 8, 2 * S5_STATES), lambda g, i: (0, g, 0))],
        out_shape=[jax.ShapeDtypeStruct((L, Bp, BR), F32),
                   jax.ShapeDtypeStruct((L, Bp, BR), F32),
                   jax.ShapeDtypeStruct((2, Bp, 2 * S5_STATES), F32)],
        scratch_shapes=[pltpu.VMEM((tt * 8, 2 * S5_STATES), F32),
                        pltpu.VMEM((2, 8, 2 * S5_STATES), F32)],
        compiler_params=_cparams("parallel", "arbitrary"), name="s5",
    )(u_tm, u_tm, h0, wb, wc, a)


def _s5_prep(lam_re, lam_im, log_dt, b_re, b_im, c_re, c_im):
    dt = jnp.exp(log_dt)[..., None]
    mag = jnp.exp(lam_re * dt)
    lb_re = mag * jnp.cos(lam_im * dt)
    lb_im = mag * jnp.sin(lam_im * dt)
    den = lam_re * lam_re + lam_im * lam_im
    nr = lb_re - 1.0
    coef_re = ((nr * lam_re + lb_im * lam_im) / den)[..., None]
    coef_im = ((lb_im * lam_re - nr * lam_im) / den)[..., None]
    bp_re = coef_re * b_re - coef_im * b_im
    bp_im = coef_re * b_im + coef_im * b_re
    eye = jnp.eye(S5_G, dtype=lam_re.dtype)

    def blockdiag_in(b):
        return jnp.einsum('dgnp,gh->dgphn', b, eye).reshape(2, BR, S5_STATES)

    def blockdiag_out(c):
        return jnp.einsum('dgpn,gh->dgnhp', c, eye).reshape(2, S5_STATES, BR)

    wb = jnp.concatenate([blockdiag_in(bp_re), blockdiag_in(bp_im)], axis=-1)
    wc = jnp.concatenate([blockdiag_out(c_re), -blockdiag_out(c_im)], axis=1)
    a = jnp.stack([lb_re.reshape(2, S5_STATES), lb_im.reshape(2, S5_STATES)], axis=1)
    return wb.astype(BF16), wc.astype(BF16), a


def _short_kernel(z_ref, zp_ref, zn_ref, w_ref, b_ref, v_ref, x1_ref, x2_ref, *, nt):
    j = pl.program_id(0) % nt
    z = z_ref[...]
    tm = z.shape[0]
    row = lax.broadcasted_iota(jnp.int32, z.shape, 0)
    prev_row = jnp.where(j > 0, zp_ref[7:8, :], 0.0)
    next_row = jnp.where(j < nt - 1, zn_ref[0:1, :], 0.0)
    zprev = jnp.where(row == 0, prev_row, pltpu.roll(z, 1, 0))
    znext = jnp.where(row == tm - 1, next_row, pltpu.roll(z, tm - 1, 0))
    y = zprev * w_ref[0:1, :] + z * w_ref[1:2, :] + znext * w_ref[2:3, :] + b_ref[...]
    v_ref[...] = y[:, 0:BR]
    x1_ref[...] = y[:, BR:2 * BR]
    x2_ref[...] = y[:, 2 * BR:3 * BR]


def _short_conv(z, w, b, L, tm):
    T = z.shape[0]
    nt = L // tm
    r8 = tm // 8
    nblk8 = T // 8
    out = jax.ShapeDtypeStruct((T, BR), F32)
    return pl.pallas_call(
        functools.partial(_short_kernel, nt=nt),
        grid=(T // tm,),
        in_specs=[pl.BlockSpec((tm, 3 * BR), lambda i: (i, 0)),
                  pl.BlockSpec((8, 3 * BR), lambda i: (jnp.maximum(i * r8 - 1, 0), 0)),
                  pl.BlockSpec((8, 3 * BR), lambda i: (jnp.minimum((i + 1) * r8, nblk8 - 1), 0)),
                  pl.BlockSpec((3, 3 * BR), lambda i: (0, 0)),
                  pl.BlockSpec((1, 3 * BR), lambda i: (0, 0))],
        out_specs=[pl.BlockSpec((tm, BR), lambda i: (i, 0))] * 3,
        out_shape=[out, out, out],
        compiler_params=_cparams("parallel"), name="short_conv",
    )(z, z, z, w, b.reshape(1, 3 * BR))


def _cs(num, den):
    th = (2.0 * math.pi / den) * (num % den).astype(F32)
    return jnp.cos(th), jnp.sin(th)


def _iota(m):
    return jnp.arange(m, dtype=jnp.int32)


def _hy_direct_tables(L):
    n = 2 * L
    c, s = _cs(_iota(n)[:, None] * _iota(L)[None, :], n)
    wf = jnp.concatenate([c, -s], axis=0)
    c, s = _cs((_iota(L)[:, None] + L // 2) * _iota(n)[None, :], n)
    wi = jnp.concatenate([c, -s], axis=1) * (1.0 / n)
    return wf.astype(BF16), wi.astype(BF16)


def _hy_spec_direct_kernel(f_ref, wf_ref, o_ref):
    o_ref[...] = _dot(wf_ref[...], f_ref[...].astype(BF16))


def _hy_spec_direct(filt, wf):
    L, C = filt.shape
    spec = pl.pallas_call(
        _hy_spec_direct_kernel,
        out_shape=jax.ShapeDtypeStruct((4 * L, C), F32), name="hy_spec_direct",
    )(filt, wf)
    return spec.reshape(2, 2 * L, 2, BR).transpose(2, 0, 1, 3)


def _hy_direct_kernel(z_ref, cw_ref, cb_ref, wf_ref, wi_ref, hf_ref, bias_ref, o_ref, *, L, nseq):
    n = 2 * L
    row = lax.broadcasted_iota(jnp.int32, (L, 3 * BR), 0)
    for s in range(nseq):
        z = z_ref[s * L:(s + 1) * L, :]
        zprev = jnp.where(row == 0, 0.0, pltpu.roll(z, 1, 0))
        znext = jnp.where(row == L - 1, 0.0, pltpu.roll(z, L - 1, 0))
        zc = zprev * cw_ref[0:1, :] + z * cw_ref[1:2, :] + znext * cw_ref[2:3, :] + cb_ref[...]
        y = zc[:, 0:BR]
        for o in range(2):
            x = _dot(wf_ref[...], y.astype(BF16))
            xr, xi = x[0:n], x[n:2 * n]
            hr, hi = hf_ref[o, 0], hf_ref[o, 1]
            yc = jnp.concatenate([xr * hr - xi * hi, xr * hi + xi * hr], axis=0).astype(BF16)
            y = zc[:, (o + 1) * BR:(o + 2) * BR] * (_dot(wi_ref[...], yc) + y * bias_ref[o:o + 1, :])
        o_ref[s * L:(s + 1) * L, :] = y


def _hy_direct(z, cw, cb, bias, tabs, hf, L):
    T = z.shape[0]
    wf, wi = tabs
    nseq = 4
    const = lambda shp: pl.BlockSpec(shp, lambda i: (0,) * len(shp))
    return pl.pallas_call(
        functools.partial(_hy_direct_kernel, L=L, nseq=nseq), grid=(T // (nseq * L),),
        in_specs=[pl.BlockSpec((nseq * L, 3 * BR), lambda i: (i, 0)),
                  const((3, 3 * BR)), const((1, 3 * BR)), const((4 * L, L)), const((L, 4 * L)),
                  const((2, 2, 2 * L, BR)), const((2, BR))],
        out_specs=pl.BlockSpec((nseq * L, BR), lambda i: (i, 0)),
        out_shape=jax.ShapeDtypeStruct((T, BR), F32),
        compiler_params=_cparams("parallel"), name="hy_direct",
    )(z, cw, cb.reshape(1, 3 * BR), wf, wi, hf, bias)


HY_N1 = 16
HY_K1 = HY_N1 // 2 + 1
HY_HALVES = 1


def _hy_split_tables(L):
    n = 2 * L
    n2 = n // HY_N1
    hk = n2 // HY_HALVES
    k1 = _iota(HY_K1)[:, None, None]
    k2 = _iota(n2)[None, :, None]
    j2 = _iota(n2)[None, None, :]
    c, s = _cs(j2 * k2 * HY_N1 + j2 * k1, n)
    top = jnp.concatenate([c, s], axis=2).reshape(HY_K1, HY_HALVES, hk, 2 * n2)
    bot = jnp.concatenate([-s, c], axis=2).reshape(HY_K1, HY_HALVES, hk, 2 * n2)
    wf = jnp.concatenate([top, bot], axis=2)
    ct = jnp.swapaxes(c, 1, 2).reshape(HY_K1, n2, HY_HALVES, hk).transpose(0, 2, 1, 3)
    st = jnp.swapaxes(s, 1, 2).reshape(HY_K1, n2, HY_HALVES, hk).transpose(0, 2, 1, 3)
    wi = jnp.concatenate([jnp.concatenate([ct, -st], axis=3),
                          jnp.concatenate([st, ct], axis=3)], axis=2)
    kk = _iota(HY_K1)[:, None]
    c1, s1 = _cs(kk * _iota(HY_N1 // 2)[None, :], HY_N1)
    wgt = jnp.where((kk == 0) | (kk == HY_N1 // 2), 1.0, 2.0) / n
    co, so = _cs(kk * (_iota(HY_N1 // 2)[None, :] + HY_N1 // 4), HY_N1)
    coef = jnp.concatenate([c1, -s1, wgt * co, -wgt * so], axis=1)
    return wf.astype(BF16), wi.astype(BF16), coef.astype(F32)


def _hy_split_stage1(coef_ref, v_ref, a_ref, k1, n2):
    nin = HY_N1 // 2
    ar = ai = None
    for j in range(nin):
        xj = v_ref[0, j * n2:(j + 1) * n2, :]
        tr, ti = coef_ref[k1, j] * xj, coef_ref[k1, nin + j] * xj
        ar, ai = (tr, ti) if ar is None else (ar + tr, ai + ti)
    a_ref[0:n2, :] = ar.astype(BF16)
    a_ref[n2:2 * n2, :] = ai.astype(BF16)


def _hy_spec_split_kernel(coef_ref, f_ref, wf_ref, o_ref, a_ref, *, L):
    n2 = 2 * L // HY_N1

    @pl.when(pl.program_id(2) == 0)
    def _():
        _hy_split_stage1(coef_ref, f_ref, a_ref, pl.program_id(1), n2)

    o_ref[0, 0, 0] = _dot(wf_ref[0, 0], a_ref[...])


def _hy_split_kernel(coef_ref, v_ref, xg_ref, bias_ref, wf_ref, wi_ref, hf_ref, o_ref, a_ref, *, L):
    k1, hh = pl.program_id(1), pl.program_id(2)
    n2 = 2 * L // HY_N1
    hk = n2 // HY_HALVES
    nin = HY_N1 // 2

    @pl.when(hh == 0)
    def _():
        _hy_split_stage1(coef_ref, v_ref, a_ref, k1, n2)

    @pl.when((k1 == 0) & (hh == 0))
    def _():
        o_ref[...] = jnp.zeros_like(o_ref)

    x = _dot(wf_ref[0, 0], a_ref[...])
    xr, xi = x[0:hk], x[hk:2 * hk]
    hr, hi = hf_ref[0, 0, 0:hk, :], hf_ref[0, 0, hk:2 * hk, :]
    yc = jnp.concatenate([xr * hr - xi * hi, xr * hi + xi * hr], axis=0).astype(BF16)
    b = _dot(wi_ref[0, 0], yc)
    for o in range(nin):
        o_ref[0, o * n2:(o + 1) * n2, :] += (coef_ref[k1, 2 * nin + o] * b[0:n2]
                                             + coef_ref[k1, 3 * nin + o] * b[n2:2 * n2])

    @pl.when((k1 == HY_K1 - 1) & (hh == HY_HALVES - 1))
    def _():
        o_ref[0] = xg_ref[0] * (o_ref[0] + v_ref[0] * bias_ref[...])


def _hy_spec_split(filt2, tabs, L):
    wf, _, coef = tabs
    n2 = 2 * L // HY_N1
    hk = n2 // HY_HALVES
    return pl.pallas_call(
        functools.partial(_hy_spec_split_kernel, L=L), grid=(2, HY_K1, HY_HALVES),
        in_specs=[pl.BlockSpec(memory_space=pltpu.SMEM),
                  pl.BlockSpec((1, L, BR), lambda o, k, h: (o, 0, 0)),
                  pl.BlockSpec((1, 1, 2 * hk, 2 * n2), lambda o, k, h: (k, h, 0, 0))],
        out_specs=pl.BlockSpec((1, 1, 1, 2 * hk, BR), lambda o, k, h: (o, k, h, 0, 0)),
        out_shape=jax.ShapeDtypeStruct((2, HY_K1, HY_HALVES, 2 * hk, BR), F32),
        scratch_shapes=[pltpu.VMEM((2 * n2, BR), BF16)],
        compiler_params=_cparams("parallel", "arbitrary", "arbitrary"), name="hy_spec_split",
    )(coef, filt2, wf)


def _hy_split(v, xg, bias, hf, tabs, L):
    wf, wi, coef = tabs
    B = v.shape[0]
    n2 = 2 * L // HY_N1
    hk = n2 // HY_HALVES
    seq = pl.BlockSpec((1, L, BR), lambda b, k, h: (b, 0, 0))
    return pl.pallas_call(
        functools.partial(_hy_split_kernel, L=L), grid=(B, HY_K1, HY_HALVES),
        in_specs=[pl.BlockSpec(memory_space=pltpu.SMEM), seq, seq,
                  pl.BlockSpec((1, BR), lambda b, k, h: (0, 0)),
                  pl.BlockSpec((1, 1, 2 * hk, 2 * n2), lambda b, k, h: (k, h, 0, 0)),
                  pl.BlockSpec((1, 1, 2 * n2, 2 * hk), lambda b, k, h: (k, h, 0, 0)),
                  pl.BlockSpec((1, 1, 2 * hk, BR), lambda b, k, h: (k, h, 0, 0))],
        out_specs=seq,
        out_shape=jax.ShapeDtypeStruct((B, L, BR), F32),
        scratch_shapes=[pltpu.VMEM((2 * n2, BR), BF16)],
        compiler_params=_cparams("parallel", "arbitrary", "arbitrary"), name="hy_split",
    )(coef, v, xg, bias.reshape(1, BR), wf, wi, hf)


def _hy_filters(L, w1, b1, w2, b2, w3):
    dt = w1.dtype
    pos = jnp.arange(L, dtype=dt)
    t01 = jnp.linspace(0.0, 1.0, L, dtype=dt)[:, None]
    w = (2.0 * math.pi / L) * pos[:, None]
    bands = jnp.linspace(1e-4, HY_BANDS - 1, HY_BANDS, dtype=dt)[None, :]
    feats = jnp.concatenate([t01, jnp.cos(bands * w), -jnp.sin(bands * w)], axis=-1)
    h = jnp.sin(feats @ w1 + b1)
    h = jnp.sin(h @ w2 + b2)
    h = h @ w3
    dist = jnp.abs(pos - (L // 2)) / L
    decay = jnp.abs(jnp.linspace(math.log(HY_DECAY_TARGET) / HY_SLOW,
                                 math.log(HY_DECAY_TARGET) / HY_FAST, 2 * BR, dtype=dt))
    return h * jnp.exp(-dist[:, None] * decay[None, :])


def _attn_kernel(lam_ref, q_ref, kt_ref, v_ref, sub_ref, o_ref, *, qscale, post):
    q = (q_ref[0] * qscale).astype(BF16)
    lam = lam_ref[0, 0]
    sub = sub_ref[...]
    for h in range(HEADS):
        vh = v_ref[0, :, h * VAL_DIM:(h + 1) * VAL_DIM]
        res = []
        for m in range(2):
            c0 = h * 2 * HEAD_DIM + m * HEAD_DIM
            s = _dot(q[:, c0:c0 + HEAD_DIM], kt_ref[0, c0:c0 + HEAD_DIM, :])
            p = jnp.exp2(s - jnp.max(s, axis=-1, keepdims=True))
            l = jnp.sum(p, axis=-1, keepdims=True)
            res.append(_dot(p.astype(BF16), vh) / l)
        o = res[0] - lam * res[1]
        o = o * lax.rsqrt(jnp.mean(o * o, axis=-1, keepdims=True) + EPS) * sub * post
        o_ref[0, :, h * VAL_DIM:(h + 1) * VAL_DIM] = o


def _attn(lam, q, kt, v, subln, tq, post):
    B, L, _ = q.shape
    Lk = kt.shape[2]
    return pl.pallas_call(
        functools.partial(_attn_kernel, qscale=HEAD_DIM ** -0.5 * LOG2E, post=post),
        grid=(B, L // tq),
        in_specs=[pl.BlockSpec(memory_space=pltpu.SMEM),
                  pl.BlockSpec((1, tq, BR), lambda b, i: (b, i, 0)),
                  pl.BlockSpec((1, BR, Lk), lambda b, i: (b, 0, 0)),
                  pl.BlockSpec((1, Lk, BR), lambda b, i: (b, 0, 0)),
                  pl.BlockSpec((1, VAL_DIM), lambda b, i: (0, 0))],
        out_specs=pl.BlockSpec((1, tq, BR), lambda b, i: (b, i, 0)),
        out_shape=jax.ShapeDtypeStruct((B, L, BR), F32),
        compiler_params=_cparams("parallel", "parallel"), name="diff_attn",
    )(lam.reshape(1, 1), q, kt, v, subln.reshape(1, VAL_DIM))


def _attn_inputs(k3, v3, ctx):
    B = k3.shape[0]
    if ctx is not None:
        k3 = jnp.concatenate([ctx[0].reshape(B, -1, BR), k3], axis=1)
        v3 = jnp.concatenate([ctx[1].reshape(B, -1, BR), v3], axis=1)
    return jnp.swapaxes(k3, 1, 2).astype(BF16), v3.astype(BF16)


def _attn_latent_test(q, k, v, ck, cv):
    kt, vb = _attn_inputs(k, v, (ck, cv))
    return _attn(jnp.float32(0.5), q, kt, vb, jnp.ones((VAL_DIM,), F32), 256, 0.5)


def _rope_tables(L, dt):
    n_rows = L // GRID_WIDTH
    row = jnp.repeat(jnp.arange(n_rows, dtype=dt), GRID_WIDTH)
    col = (jnp.arange(L) % GRID_WIDTH).astype(dt)
    inv = ROPE_BASE ** (-jnp.arange(0, ROPE_DIM, 2, dtype=dt) / ROPE_DIM)
    ar = row[:, None] * inv[None, :]
    ac = col[:, None] * inv[None, :]
    ang = jnp.concatenate([ar, ar, ac, ac], axis=-1)
    sign = jnp.tile(jnp.concatenate([-jnp.ones(ROPE_DIM // 2, dt), jnp.ones(ROPE_DIM // 2, dt)]), 2)
    reps = BR // HEAD_DIM
    return jnp.tile(jnp.cos(ang), (1, reps)), jnp.tile(jnp.sin(ang) * sign, (1, reps))


def _mixout_kernel(x_ref, mod_ref, u_ref, yf_ref, yb_ref, uv_ref, hy_ref, da_ref,
                   sd_ref, wg_ref, bg_ref, sn_ref, ws_ref, bs_ref, bn_ref, wo_ref, gf_ref, rt_ref,
                   xo_ref, h2_ref, cb_ref, *, route):
    tm = x_ref.shape[0]
    u = u_ref[...]
    y = jax.nn.gelu(sd_ref[...] * u + yf_ref[...] + yb_ref[...])
    y = y * jax.nn.sigmoid(_dot(y.astype(BF16), wg_ref[...]) + bg_ref[...])
    acc = _dot(_rms(y, bn_ref[0:1, :]).astype(BF16), wo_ref[0:BR, :])
    guv = jax.nn.gelu(uv_ref[...])
    gu = guv[:, 0:BR]
    gv = _rms(guv[:, BR:2 * BR], sn_ref[...]).astype(BF16)
    lane = lax.broadcasted_iota(jnp.int32, (SGU_CHUNK, BR), 1)
    hd = BR // SGU_HEADS
    zs = []
    for c in range(tm // SGU_CHUNK):
        vc = gv[c * SGU_CHUNK:(c + 1) * SGU_CHUNK, :]
        z = bs_ref[...]
        for h in range(SGU_HEADS):
            z = z + jnp.where(lane // hd == h, _dot(ws_ref[h], vc), 0.0)
        zs.append(z)
    z = zs[0] if len(zs) == 1 else jnp.concatenate(zs, axis=0)
    acc = acc + _dot(_rms(gu * z, bn_ref[1:2, :]).astype(BF16), wo_ref[BR:2 * BR, :])
    acc = acc + _dot(_rms(hy_ref[...], bn_ref[2:3, :]).astype(BF16), wo_ref[2 * BR:3 * BR, :])
    acc = acc + _dot(da_ref[...].astype(BF16), wo_ref[3 * BR:4 * BR, :])
    xn = x_ref[...] + mod_ref[:, 2 * D:3 * D] * acc
    xo_ref[...] = xn
    h2 = _rms(xn, gf_ref[...]) * (1.0 + mod_ref[:, 4 * D:5 * D]) + mod_ref[:, 3 * D:4 * D]
    h2_ref[...] = h2.astype(BF16)
    if not route:
        cb_ref[...] = jnp.zeros_like(cb_ref)
        return
    logits = _dot3(h2, rt_ref[...])
    el = lax.broadcasted_iota(jnp.int32, logits.shape, 1)
    logits = jnp.where(el < N_EXPERTS, logits, -jnp.inf)
    e = jnp.exp(logits - jnp.max(logits, axis=-1, keepdims=True))
    probs = e / jnp.sum(e, axis=-1, keepdims=True)
    big = logits.shape[1]
    m1 = jnp.max(probs, axis=-1, keepdims=True)
    i1 = jnp.min(jnp.where(probs == m1, el, big), axis=-1, keepdims=True)
    p2 = jnp.where((el == i1) | (el >= N_EXPERTS), -1.0, probs)
    m2 = jnp.max(p2, axis=-1, keepdims=True)
    i2 = jnp.min(jnp.where(p2 == m2, el, big), axis=-1, keepdims=True)
    tot = m1 + m2
    cb_ref[...] = jnp.where(el == i1, m1 / tot, 0.0) + jnp.where(el == i2, m2 / tot, 0.0)


def _mixout(x, modg, u_tm2, yf2, yb2, uv, hy, da, lp, B, L, tm, route):
    T = B * L
    nt = L // tm
    tok = lambda n: pl.BlockSpec((tm, n), lambda i: (i, 0))
    tmaj = pl.BlockSpec((tm, BR), lambda i: (i % nt, i // nt))
    const = lambda shp: pl.BlockSpec(shp, lambda i: (0,) * len(shp))
    return pl.pallas_call(
        functools.partial(_mixout_kernel, route=route), grid=(T // tm,),
        in_specs=[tok(D), _mod_spec(modg, tm, L),
                  tmaj, tmaj, tmaj, tok(2 * BR), tok(BR), tok(BR),
                  const((1, BR)), const((BR, BR)), const((1, BR)), const((1, BR)),
                  const((SGU_HEADS, SGU_CHUNK, SGU_CHUNK)), const((SGU_CHUNK, BR)),
                  const((3, BR)), pl.BlockSpec((None, D, D), lambda i: (lp['l'], 0, 0)),
                  const((1, D)), const((D, 128))],
        out_specs=[tok(D), tok(D), tok(128)],
        out_shape=[jax.ShapeDtypeStruct((T, D), F32), jax.ShapeDtypeStruct((T, D), BF16),
                   jax.ShapeDtypeStruct((T, 128), F32)],
        compiler_params=_cparams("parallel"), name="mix_out",
    )(x, modg, u_tm2, yf2, yb2, uv, hy, da,
      lp['s5_d'], lp['s5_w_glu'], lp['s5_b_glu'], lp['sgu_norm'], lp['sgu_w_s'], lp['sgu_b'],
      lp['branch_norm'], lp['w_out'], lp['norm_ffn'], lp['router'])


def _ffn_kernel(h_ref, x_ref, mod_ref, w1_ref, w3_ref, w2_ref, o_ref, acc_ref):
    j = pl.program_id(1)
    h = h_ref[...]
    a = _dot(h, w1_ref[...])
    t = (a * jax.nn.sigmoid(a) * _dot(h, w3_ref[...])).astype(BF16)
    part = _dot(t, w2_ref[...])

    @pl.when(j == 0)
    def _():
        acc_ref[...] = part

    @pl.when(j > 0)
    def _():
        acc_ref[...] += part

    @pl.when(j == pl.num_programs(1) - 1)
    def _():
        o_ref[...] = x_ref[...] + mod_ref[:, 5 * D:6 * D] * acc_ref[...]


def _ffn(h2, x, modg, w1, w3, w2, jl, L, tm, tf):
    T = x.shape[0]
    dff = w1.shape[2]
    return pl.pallas_call(
        _ffn_kernel, grid=(T // tm, dff // tf),
        in_specs=[pl.BlockSpec((tm, D), lambda i, j: (i, 0)),
                  pl.BlockSpec((tm, D), lambda i, j: (i, 0)),
                  _mod_spec(modg, tm, L),
                  pl.BlockSpec((None, D, tf), lambda i, j: (jl, 0, j)),
                  pl.BlockSpec((None, D, tf), lambda i, j: (jl, 0, j)),
                  pl.BlockSpec((None, tf, D), lambda i, j: (jl, j, 0))],
        out_specs=pl.BlockSpec((tm, D), lambda i, j: (i, 0)),
        out_shape=jax.ShapeDtypeStruct((T, D), F32),
        scratch_shapes=[pltpu.VMEM((tm, D), F32)],
        compiler_params=_cparams("parallel", "arbitrary"), name="ffn",
    )(h2, x, modg, w1, w3, w2)


def _route_kernel(cb_ref, rk_ref, rkt_ref, cnt_ref):
    tm = cb_ref.shape[0]
    mask = cb_ref[...] > 0.0
    mf = jnp.where(mask, 1.0, 0.0)
    r = lax.broadcasted_iota(jnp.int32, (tm, tm), 0)
    c = lax.broadcasted_iota(jnp.int32, (tm, tm), 1)
    before = jnp.where(c < r, 1.0, 0.0).astype(BF16)
    rank = jnp.where(mask, _dot(before, mf.astype(BF16)), -1.0)
    rk_ref[...] = rank
    rkt_ref[...] = rank.T[0:N_EXPERTS, :]
    cnt_ref[...] = jnp.sum(mf, axis=0, keepdims=True)


def _route(cb, tm):
    T = cb.shape[0]
    nt = T // tm
    rk, rkt, cnt = pl.pallas_call(
        _route_kernel, grid=(nt,),
        in_specs=[pl.BlockSpec((tm, 128), lambda i: (i, 0))],
        out_specs=[pl.BlockSpec((tm, 128), lambda i: (i, 0)),
                   pl.BlockSpec((None, N_EXPERTS, tm), lambda i: (i, 0, 0)),
                   pl.BlockSpec((None, 1, 128), lambda i: (i, 0, 0))],
        out_shape=[jax.ShapeDtypeStruct((T, 128), F32),
                   jax.ShapeDtypeStruct((nt, N_EXPERTS, tm), F32),
                   jax.ShapeDtypeStruct((nt, 1, 128), F32)],
        compiler_params=_cparams("parallel"), name="moe_route",
    )(cb)
    return rk, rkt, cnt[:, 0, :N_EXPERTS].astype(jnp.int32).reshape(-1)


MOE_CHUNK = 144


def _moe_kernel(cnt_ref, h_ref, x_ref, mod_ref, cb_ref, rk_ref, rkt_ref, w1_ref, w3_ref, w2_ref,
                o_ref, xg_ref, y_ref):
    i, e, j = pl.program_id(0), pl.program_id(1), pl.program_id(2)
    ne, nj = pl.num_programs(1), pl.num_programs(2)
    tm = h_ref.shape[0]
    ch = MOE_CHUNK
    nch = (cnt_ref[i * ne + e] + (ch - 1)) // ch

    def rows(c):
        return pl.ds(pl.multiple_of(c * ch, ch), ch)

    @pl.when((e == 0) & (j == 0))
    def _():
        o_ref[...] = jnp.zeros_like(o_ref)
        y_ref[...] = jnp.zeros_like(y_ref)

    @pl.when(j == 0)
    def _():
        rid = lax.broadcasted_iota(jnp.int32, (ch, tm), 0).astype(F32)
        rrow = rkt_ref[pl.ds(e, 1), :]

        def gather(c, _):
            sel = jnp.where(rid == rrow - (c * ch).astype(F32), 1.0, 0.0).astype(BF16)
            xg_ref[rows(c), :] = _dot(sel, h_ref[...]).astype(BF16)
            return 0

        lax.fori_loop(0, nch, gather, 0)

    def expert(c, _):
        xg = xg_ref[rows(c), :]
        a = _dot(xg, w1_ref[0])
        t = (a * jax.nn.sigmoid(a) * _dot(xg, w3_ref[0])).astype(BF16)
        part = _dot(t, w2_ref[0])

        @pl.when(j == 0)
        def _():
            y_ref[rows(c), :] = part

        @pl.when(j > 0)
        def _():
            y_ref[rows(c), :] += part

        return 0

    lax.fori_loop(0, nch, expert, 0)

    @pl.when(j == nj - 1)
    def _():
        el = lax.broadcasted_iota(jnp.int32, (tm, 128), 1)
        rcol = jnp.sum(jnp.where(el == e, rk_ref[...], 0.0), axis=-1, keepdims=True)
        wcol = jnp.sum(jnp.where(el == e, cb_ref[...], 0.0), axis=-1, keepdims=True)
        cid = lax.broadcasted_iota(jnp.int32, (tm, 2 * ch), 1).astype(F32)

        def scatter(c, _):
            r0 = pl.multiple_of(c * (2 * ch), 2 * ch)
            selt = jnp.where(cid == rcol - r0.astype(F32), 1.0, 0.0).astype(BF16)
            o_ref[...] += wcol * _dot(selt, y_ref[pl.ds(r0, 2 * ch), :].astype(BF16))
            return 0

        lax.fori_loop(0, (nch + 1) // 2, scatter, 0)

    @pl.when((e == ne - 1) & (j == nj - 1))
    def _():
        o_ref[...] = x_ref[...] + mod_ref[:, 5 * D:6 * D] * o_ref[...]


def _moe(h2, x, modg, cb, w1, w3, w2, jl, L, tm, tf):
    T = x.shape[0]
    _, ne, _, dff = w1.shape
    rk, rkt, cnt = _route(cb, tm)
    nmod = modg.shape[0]
    rows = -(-tm // (2 * MOE_CHUNK)) * 2 * MOE_CHUNK
    grid_spec = pltpu.PrefetchScalarGridSpec(
        num_scalar_prefetch=1, grid=(T // tm, ne, dff // tf),
        in_specs=[pl.BlockSpec((tm, D), lambda i, e, j, c: (i, 0)),
                  pl.BlockSpec((tm, D), lambda i, e, j, c: (i, 0)),
                  pl.BlockSpec((None, 1, 6 * D), lambda i, e, j, c: ((i * tm // L) % nmod, 0, 0)),
                  pl.BlockSpec((tm, 128), lambda i, e, j, c: (i, 0)),
                  pl.BlockSpec((tm, 128), lambda i, e, j, c: (i, 0)),
                  pl.BlockSpec((None, N_EXPERTS, tm), lambda i, e, j, c: (i, 0, 0)),
                  pl.BlockSpec((None, 1, D, tf), lambda i, e, j, c: (jl, e, 0, j)),
                  pl.BlockSpec((None, 1, D, tf), lambda i, e, j, c: (jl, e, 0, j)),
                  pl.BlockSpec((None, 1, tf, D), lambda i, e, j, c: (jl, e, j, 0))],
        out_specs=pl.BlockSpec((tm, D), lambda i, e, j, c: (i, 0)),
        scratch_shapes=[pltpu.VMEM((rows, D), BF16), pltpu.VMEM((rows, D), F32)])
    return pl.pallas_call(
        _moe_kernel, grid_spec=grid_spec,
        out_shape=jax.ShapeDtypeStruct((T, D), F32),
        compiler_params=_cparams("parallel", "arbitrary", "arbitrary"), name="moe",
    )(cnt, h2, x, modg, cb, rk, rkt, w1, w3, w2)


def _final_kernel(x_ref, g_ref, o_ref):
    o_ref[...] = _rms(x_ref[...], g_ref[...])


def _final_norm(x, g, tm):
    T = x.shape[0]
    return pl.pallas_call(
        _final_kernel, grid=(T // tm,),
        in_specs=[pl.BlockSpec((tm, D), lambda i: (i, 0)), pl.BlockSpec((1, D), lambda i: (0, 0))],
        out_specs=pl.BlockSpec((tm, D), lambda i: (i, 0)),
        out_shape=jax.ShapeDtypeStruct((T, D), F32),
        compiler_params=_cparams("parallel"), name="final_norm",
    )(x, g.reshape(1, D))


def _layer(x, modg, lp, l, B, L, ctx):
    is_ctx = ctx is None
    T = B * L
    Bp = -(-B // 8) * 8
    tm = min(L, 512)
    rope_tabs = None if is_ctx else _rope_tables(L, F32)
    u_tm, uv, z, q, k, v = _inproj(x, modg, lp['norm_mix'], lp['w_in'], l, B, L, tm, rope_tabs)

    u3 = jnp.pad(u_tm.reshape(L, B, BR), ((0, 0), (0, Bp - B), (0, 0)))
    if is_ctx:
        h0 = jnp.zeros((2, Bp, 2 * S5_STATES), F32)
    else:
        h0 = ctx[2]
        h0 = jnp.transpose(h0, (1, 0, 4, 2, 3)).reshape(2, B, 2 * S5_STATES)
        h0 = jnp.pad(h0, ((0, 0), (0, Bp - B), (0, 0)))
    yf, yb, hfin = _s5(u3, h0, lp['s5_wb'], lp['s5_wc'], lp['s5_a'], L, Bp, min(L, 128))

    tabs, hf = lp['hy'][L]
    if L <= HY_DIRECT_MAX:
        y_hy = _hy_direct(z, lp['hy_conv_w'], lp['hy_conv_b'], lp['hy_bias'], tabs, hf, L)
    else:
        hv, hx1, hx2 = _short_conv(z, lp['hy_conv_w'], lp['hy_conv_b'], L, tm)
        y1 = _hy_split(hv.reshape(B, L, BR), hx1.reshape(B, L, BR), lp['hy_bias'][0], hf[0], tabs, L)
        y_hy = _hy_split(y1, hx2.reshape(B, L, BR), lp['hy_bias'][1], hf[1], tabs, L).reshape(T, BR)

    q3 = q.reshape(B, L, BR)
    k3 = k.reshape(B, L, BR)
    v3 = v.reshape(B, L, BR)
    kt, vb = _attn_inputs(k3, v3, None if is_ctx else ctx)
    da = _attn(lp['da_lam'], q3, kt, vb, lp['da_subln'], min(L, 256), 1.0 - lp['lam_init'])

    x, h2, cb = _mixout(x, modg, u_tm, yf.reshape(L, Bp * BR), yb.reshape(L, Bp * BR), uv,
                        y_hy, da.reshape(T, BR), lp, B, L, tm, l % 2 == 1)
    tmf = min(T, 1024)
    if l % 2 == 0:
        x = _ffn(h2, x, modg, lp['ffn_w1'], lp['ffn_w3'], lp['ffn_w2'], l // 2, L, tmf,
                 lp['ffn_w1'].shape[2] // 2)
    else:
        x = _moe(h2, x, modg, cb, lp['moe_w1'], lp['moe_w3'], lp['moe_w2'], l // 2, L, tmf, 1792)
    if is_ctx:
        fin = hfin[:, :B].reshape(2, B, 2, S5_G, S5_N)
        fin = jnp.transpose(fin, (1, 0, 3, 4, 2))
        return x, k3.reshape(B, L, HEADS, 2 * HEAD_DIM), v3.reshape(B, L, HEADS, VAL_DIM), fin
    return x


def kernel(x_prompt, x_sample, cache_k, cache_v, state_ssm, c, c_ctx, w_ada, b_ada, norm_mix, norm_ffn, w_in, w_out, branch_norm, s5_lam_re, s5_lam_im, s5_log_dt, s5_b_re, s5_b_im, s5_c_re, s5_c_im, s5_d, s5_w_glu, s5_b_glu, sgu_norm, sgu_w_s, sgu_b_s, hy_conv_w, hy_conv_b, hy_w1, hy_b1, hy_w2, hy_b2, hy_w3, hy_bias, da_lq1, da_lk1, da_lq2, da_lk2, da_subln, ffn_w1, ffn_w3, ffn_w2, moe_router, moe_w1, moe_w3, moe_w2, norm_final):
    depth = w_in.shape[0]
    Bc, Lc, _ = x_prompt.shape
    Bs, Ls, _ = x_sample.shape

    cond = jnp.concatenate([c_ctx[None, :], c], axis=0)
    cond8 = jnp.pad(cond, ((0, 8 - cond.shape[0]), (0, 0)))
    mod = _ada(cond8, w_ada, b_ada)

    def hyena_tables(L):
        return _hy_direct_tables(L) if L <= HY_DIRECT_MAX else _hy_split_tables(L)

    def hyena_spectrum(L, tabs, *hy_args):
        filt = _hy_filters(L, *hy_args)
        if L <= HY_DIRECT_MAX:
            return _hy_spec_direct(filt, tabs[0])
        return _hy_spec_split(filt.reshape(L, 2, BR).transpose(1, 0, 2), tabs, L)

    hy_tabs = {L: hyena_tables(L) for L in {Lc, Ls}}

    bf = {name: w.astype(BF16) for name, w in dict(
        w_in=w_in, w_out=w_out, ffn_w1=ffn_w1, ffn_w3=ffn_w3, ffn_w2=ffn_w2,
        moe_w1=moe_w1, moe_w3=moe_w3, moe_w2=moe_w2).items()}

    layers = []
    for l in range(depth):
        j = l // 2
        wb, wc, a = _s5_prep(s5_lam_re[l], s5_lam_im[l], s5_log_dt[l], s5_b_re[l], s5_b_im[l],
                             s5_c_re[l], s5_c_im[l])
        lam_init = 0.8 - 0.6 * math.exp(-0.3 * l)
        lam = (jnp.exp(jnp.sum(da_lq1[l] * da_lk1[l])) - jnp.exp(jnp.sum(da_lq2[l] * da_lk2[l]))
               + lam_init)
        hy_args = (hy_w1[l], hy_b1[l], hy_w2[l], hy_b2[l], hy_w3[l])
        lp = dict(
            l=l, norm_mix=norm_mix[l], norm_ffn=norm_ffn[l].reshape(1, D), branch_norm=branch_norm[l],
            s5_wb=wb, s5_wc=wc, s5_a=a, s5_d=s5_d[l].reshape(1, BR),
            s5_w_glu=s5_w_glu[l].astype(BF16), s5_b_glu=s5_b_glu[l].reshape(1, BR),
            sgu_norm=sgu_norm[l].reshape(1, BR), sgu_w_s=sgu_w_s[l].astype(BF16),
            sgu_b=jnp.repeat(sgu_b_s[l].T, BR // SGU_HEADS, axis=1),
            hy_conv_w=hy_conv_w[l], hy_conv_b=hy_conv_b[l], hy_bias=hy_bias[l],
            hy={L: (t, hyena_spectrum(L, t, *hy_args)) for L, t in hy_tabs.items()},
            da_lam=lam, lam_init=lam_init, da_subln=da_subln[l],
        )
        lp.update(bf)
        if l % 2 == 0:
            lp.update(router=jnp.zeros((D, 128), F32))
        else:
            lp.update(router=jnp.pad(moe_router[j], ((0, 0), (0, 128 - N_EXPERTS))))
        layers.append(lp)

    xc = x_prompt.reshape(Bc * Lc, D)
    ks, vs, ss = [], [], []
    for l in range(depth):
        xc, k_l, v_l, s_l = _layer(xc, mod[l, 0:1].reshape(1, 1, 6 * D), layers[l], l, Bc, Lc, None)
        ks.append(k_l)
        vs.append(v_l)
        ss.append(s_l)
    y_prompt = _final_norm(xc, norm_final, 512).reshape(Bc, Lc, D)

    xs = x_sample.reshape(Bs * Ls, D)
    for l in range(depth):
        xs = _layer(xs, mod[l, 1:1 + Bs].reshape(Bs, 1, 6 * D), layers[l], l, Bs, Ls,
                    (cache_k[:, l], cache_v[:, l], state_ssm[:, l]))
    y_sample = _final_norm(xs, norm_final, 512).reshape(Bs, Ls, D)
    return (y_prompt, y_sample, jnp.stack(ks, axis=1), jnp.stack(vs, axis=1), jnp.stack(ss, axis=1))
```

```python
import functools
import math

import jax
import jax.numpy as jnp
import numpy as np
from jax import lax
from jax.experimental import pallas as pl
from jax.experimental.pallas import tpu as pltpu

F32 = jnp.float32
BF16 = jnp.bfloat16

D = 1024
BR = 256
PROJ = 9 * BR
S5_G, S5_N, S5_P = 16, 64, 16
S5_STATES = S5_G * S5_N
SGU_CHUNK, SGU_HEADS = 128, 4
HEADS, HEAD_DIM, VAL_DIM = 4, 32, 64
GRID_WIDTH = 64
ROPE_DIM = HEAD_DIM // 2
ROPE_BASE = 10000.0
HY_EMB, HY_BANDS = 33, 16
HY_DECAY_TARGET, HY_FAST, HY_SLOW = 1e-2, 0.3, 1.5
N_EXPERTS = 8
EPS = 1e-6
LOG2E = 1.4426950408889634

HY_DIRECT_MAX = 512
VMEM_LIMIT = 56 * 1024 * 1024


def _cparams(*sem):
    return pltpu.CompilerParams(dimension_semantics=sem, vmem_limit_bytes=VMEM_LIMIT)


def _dot(a, b):
    return jnp.dot(a, b, preferred_element_type=F32)


def _dot3(a, b):
    a_hi = a.astype(BF16)
    b_hi = b.astype(BF16)
    a_lo = (a - a_hi.astype(F32)).astype(BF16)
    b_lo = (b - b_hi.astype(F32)).astype(BF16)
    return _dot(a_hi, b_hi) + (_dot(a_hi, b_lo) + _dot(a_lo, b_hi))


def _rms(x, g):
    return x * lax.rsqrt(jnp.mean(x * x, axis=-1, keepdims=True) + EPS) * g


def _mod_spec(modg, tm, L):
    nmod = modg.shape[0]
    return pl.BlockSpec((None, 1, 6 * D), lambda i, *_: ((i * tm // L) % nmod, 0, 0))


def _ada_kernel(c_ref, w_ref, b_ref, o_ref):
    c = c_ref[...]
    s = (c * jax.nn.sigmoid(c)).astype(BF16)
    o_ref[0] = _dot(s, w_ref[0].astype(BF16)) + b_ref[0]


def _ada(cond8, w_ada, b_ada):
    depth = w_ada.shape[0]
    tn = 1536
    return pl.pallas_call(
        _ada_kernel,
        grid=(depth, 6 * D // tn),
        in_specs=[pl.BlockSpec((8, D), lambda l, j: (0, 0)),
                  pl.BlockSpec((1, D, tn), lambda l, j: (l, 0, j)),
                  pl.BlockSpec((1, 1, tn), lambda l, j: (l, 0, j))],
        out_specs=pl.BlockSpec((1, 8, tn), lambda l, j: (l, 0, j)),
        out_shape=jax.ShapeDtypeStruct((depth, 8, 6 * D), F32),
        compiler_params=_cparams("parallel", "parallel"),
        name="ada",
    )(cond8, w_ada, b_ada.reshape(depth, 1, 6 * D))


def _inproj_kernel(*refs, rope):
    if rope:
        (x_ref, mod_ref, g_ref, w_ref, cos_ref, sin_ref,
         u_ref, uv_ref, z_ref, q_ref, k_ref, v_ref) = refs
    else:
        x_ref, mod_ref, g_ref, w_ref, u_ref, uv_ref, z_ref, q_ref, k_ref, v_ref = refs
    x = x_ref[...]
    h = _rms(x, g_ref[...]) * (1.0 + mod_ref[:, D:2 * D]) + mod_ref[:, 0:D]
    p = _dot(h.astype(BF16), w_ref[...])
    u_ref[...] = p[:, 0:BR]
    uv_ref[...] = p[:, BR:3 * BR]
    z_ref[...] = p[:, 3 * BR:6 * BR]
    q = p[:, 6 * BR:7 * BR]
    k = p[:, 7 * BR:8 * BR]
    if rope:
        cs = cos_ref[...]
        sn = sin_ref[...]
        lane = lax.broadcasted_iota(jnp.int32, q.shape, 1)
        first = (lane % (2 * (ROPE_DIM // 2))) < (ROPE_DIM // 2)
        half = ROPE_DIM // 2

        def rot(t):
            return jnp.where(first, pltpu.roll(t, BR - half, 1), pltpu.roll(t, half, 1))

        q = q * cs + rot(q) * sn
        k = k * cs + rot(k) * sn
    q_ref[...] = q
    k_ref[...] = k
    v_ref[...] = p[:, 8 * BR:9 * BR]


def _inproj(x, modg, g, w_bf, l, B, L, tm, rope_tabs):
    T = B * L
    nt = L // tm
    rope = rope_tabs is not None
    in_specs = [pl.BlockSpec((tm, D), lambda i: (i, 0)),
                _mod_spec(modg, tm, L),
                pl.BlockSpec((1, D), lambda i: (0, 0)),
                pl.BlockSpec((None, D, PROJ), lambda i: (l, 0, 0))]
    args = [x, modg, g.reshape(1, D), w_bf]
    if rope:
        in_specs += [pl.BlockSpec((tm, BR), lambda i: (i % nt, 0))] * 2
        args += list(rope_tabs)
    tok = lambda n: pl.BlockSpec((tm, n), lambda i: (i, 0))
    out_specs = [pl.BlockSpec((tm, BR), lambda i: (i % nt, i // nt)),
                 tok(2 * BR), tok(3 * BR), tok(BR), tok(BR), tok(BR)]
    out_shape = [jax.ShapeDtypeStruct((L, B * BR), F32),
                 jax.ShapeDtypeStruct((T, 2 * BR), F32),
                 jax.ShapeDtypeStruct((T, 3 * BR), F32),
                 jax.ShapeDtypeStruct((T, BR), F32),
                 jax.ShapeDtypeStruct((T, BR), F32),
                 jax.ShapeDtypeStruct((T, BR), F32)]
    return pl.pallas_call(
        functools.partial(_inproj_kernel, rope=rope),
        grid=(T // tm,), in_specs=in_specs, out_specs=out_specs, out_shape=out_shape,
        compiler_params=_cparams("parallel"), name="inproj",
    )(*args)


def _s5_kernel(uf_ref, ub_ref, h0_ref, wb_ref, wc_ref, a_ref, yf_ref, yb_ref, hfin_ref,
               xs_ref, hc_ref, *, tt):
    i = pl.program_id(1)
    last = pl.num_programs(1) - 1
    half = S5_STATES // 2

    @pl.when(i == 0)
    def _():
        hc_ref[...] = h0_ref[...]

    for d in range(2):
        u_ref = uf_ref if d == 0 else ub_ref
        y_ref = yf_ref if d == 0 else yb_ref
        u2 = u_ref[...].reshape(tt * 8, BR).astype(BF16)
        xs_ref[...] = _dot(u2, wb_ref[d])
        for c in range(2):
            cr = slice(c * half, (c + 1) * half)
            ci = slice(S5_STATES + c * half, S5_STATES + (c + 1) * half)
            ar = jnp.broadcast_to(a_ref[d, 0:1, cr], (8, half))
            ai = jnp.broadcast_to(a_ref[d, 1:2, cr], (8, half))

            def body(s, carry, cr=cr, ci=ci, ar=ar, ai=ai, d=d):
                hr, hi = carry
                t = s if d == 0 else tt - 1 - s
                r0 = pl.multiple_of(t * 8, 8)
                nr = ar * hr - ai * hi + xs_ref[pl.ds(r0, 8), cr]
                ni = ar * hi + ai * hr + xs_ref[pl.ds(r0, 8), ci]
                xs_ref[pl.ds(r0, 8), cr] = nr
                xs_ref[pl.ds(r0, 8), ci] = ni
                return nr, ni

            hr, hi = lax.fori_loop(0, tt, body, (hc_ref[d, :, cr], hc_ref[d, :, ci]), unroll=4)
            hc_ref[d, :, cr] = hr
            hc_ref[d, :, ci] = hi
        y = _dot(xs_ref[...].astype(BF16), wc_ref[d])
        y_ref[...] = y.reshape(tt, 8, BR)

    @pl.when(i == last)
    def _():
        hfin_ref[...] = hc_ref[...]


def _s5(u_tm, h0, wb, wc, a, L, Bp, tt):
    nT = L // tt
    ng = Bp // 8
    blk = lambda f: pl.BlockSpec((tt, 8, BR), f)
    const = lambda shp: pl.BlockSpec(shp, lambda g, i: (0,) * len(shp))
    return pl.pallas_call(
        functools.partial(_s5_kernel, tt=tt),
        grid=(ng, nT),
        in_specs=[blk(lambda g, i: (i, g, 0)), blk(lambda g, i: (nT - 1 - i, g, 0)),
                  pl.BlockSpec((2, 8, 2 * S5_STATES), lambda g, i: (0, g, 0)),
                  const((2, BR, 2 * S5_STATES)), const((2, 2 * S5_STATES, BR)),
                  const((2, 2, S5_STATES))],
        out_specs=[blk(lambda g, i: (i, g, 0)), blk(lambda g, i: (nT - 1 - i, g, 0)),
                   pl.BlockSpec((2, 8, 2 * S5_STATES), lambda g, i: (0, g, 0))],
        out_shape=[jax.ShapeDtypeStruct((L, Bp, BR), F32),
                   jax.ShapeDtypeStruct((L, Bp, BR), F32),
                   jax.ShapeDtypeStruct((2, Bp, 2 * S5_STATES), F32)],
        scratch_shapes=[pltpu.VMEM((tt * 8, 2 * S5_STATES), F32),
                        pltpu.VMEM((2, 8, 2 * S5_STATES), F32)],
        compiler_params=_cparams("parallel", "arbitrary"), name="s5",
    )(u_tm, u_tm, h0, wb, wc, a)


def _s5_prep(lam_re, lam_im, log_dt, b_re, b_im, c_re, c_im):
    dt = jnp.exp(log_dt)[..., None]
    mag = jnp.exp(lam_re * dt)
    lb_re = mag * jnp.cos(lam_im * dt)
    lb_im = mag * jnp.sin(lam_im * dt)
    den = lam_re * lam_re + lam_im * lam_im
    nr = lb_re - 1.0
    coef_re = ((nr * lam_re + lb_im * lam_im) / den)[..., None]
    coef_im = ((lb_im * lam_re - nr * lam_im) / den)[..., None]
    bp_re = coef_re * b_re - coef_im * b_im
    bp_im = coef_re * b_im + coef_im * b_re
    eye = jnp.eye(S5_G, dtype=lam_re.dtype)

    def blockdiag_in(b):
        return jnp.einsum('dgnp,gh->dgphn', b, eye).reshape(2, BR, S5_STATES)

    def blockdiag_out(c):
        return jnp.einsum('dgpn,gh->dgnhp', c, eye).reshape(2, S5_STATES, BR)

    wb = jnp.concatenate([blockdiag_in(bp_re), blockdiag_in(bp_im)], axis=-1)
    wc = jnp.concatenate([blockdiag_out(c_re), -blockdiag_out(c_im)], axis=1)
    a = jnp.stack([lb_re.reshape(2, S5_STATES), lb_im.reshape(2, S5_STATES)], axis=1)
    return wb.astype(BF16), wc.astype(BF16), a


def _short_kernel(z_ref, zp_ref, zn_ref, w_ref, b_ref, v_ref, x1_ref, x2_ref, *, nt):
    j = pl.program_id(0) % nt
    z = z_ref[...]
    tm = z.shape[0]
    row = lax.broadcasted_iota(jnp.int32, z.shape, 0)
    prev_row = jnp.where(j > 0, zp_ref[7:8, :], 0.0)
    next_row = jnp.where(j < nt - 1, zn_ref[0:1, :], 0.0)
    zprev = jnp.where(row == 0, prev_row, pltpu.roll(z, 1, 0))
    znext = jnp.where(row == tm - 1, next_row, pltpu.roll(z, tm - 1, 0))
    y = zprev * w_ref[0:1, :] + z * w_ref[1:2, :] + znext * w_ref[2:3, :] + b_ref[...]
    v_ref[...] = y[:, 0:BR]
    x1_ref[...] = y[:, BR:2 * BR]
    x2_ref[...] = y[:, 2 * BR:3 * BR]


def _short_conv(z, w, b, L, tm):
    T = z.shape[0]
    nt = L // tm
    r8 = tm // 8
    nblk8 = T // 8
    out = jax.ShapeDtypeStruct((T, BR), F32)
    return pl.pallas_call(
        functools.partial(_short_kernel, nt=nt),
        grid=(T // tm,),
        in_specs=[pl.BlockSpec((tm, 3 * BR), lambda i: (i, 0)),
                  pl.BlockSpec((8, 3 * BR), lambda i: (jnp.maximum(i * r8 - 1, 0), 0)),
                  pl.BlockSpec((8, 3 * BR), lambda i: (jnp.minimum((i + 1) * r8, nblk8 - 1), 0)),
                  pl.BlockSpec((3, 3 * BR), lambda i: (0, 0)),
                  pl.BlockSpec((1, 3 * BR), lambda i: (0, 0))],
        out_specs=[pl.BlockSpec((tm, BR), lambda i: (i, 0))] * 3,
        out_shape=[out, out, out],
        compiler_params=_cparams("parallel"), name="short_conv",
    )(z, z, z, w, b.reshape(1, 3 * BR))


def _cs(num, den):
    th = (2.0 * math.pi / den) * (num % den).astype(F32)
    return jnp.cos(th), jnp.sin(th)


def _iota(m):
    return jnp.arange(m, dtype=jnp.int32)


def _hy_direct_tables(L):
    n = 2 * L
    c, s = _cs(_iota(n)[:, None] * _iota(L)[None, :], n)
    wf = jnp.concatenate([c, -s], axis=0)
    c, s = _cs((_iota(L)[:, None] + L // 2) * _iota(n)[None, :], n)
    wi = jnp.concatenate([c, -s], axis=1) * (1.0 / n)
    return wf.astype(BF16), wi.astype(BF16)


def _hy_spec_direct_kernel(f_ref, wf_ref, o_ref):
    o_ref[...] = _dot(wf_ref[...], f_ref[...].astype(BF16))


def _hy_spec_direct(filt, wf):
    L, C = filt.shape
    spec = pl.pallas_call(
        _hy_spec_direct_kernel,
        out_shape=jax.ShapeDtypeStruct((4 * L, C), F32), name="hy_spec_direct",
    )(filt, wf)
    return spec.reshape(2, 2 * L, 2, BR).transpose(2, 0, 1, 3)


def _hy_direct_kernel(z_ref, cw_ref, cb_ref, wf_ref, wi_ref, hf_ref, bias_ref, o_ref, *, L, nseq):
    n = 2 * L
    row = lax.broadcasted_iota(jnp.int32, (L, 3 * BR), 0)
    for s in range(nseq):
        z = z_ref[s * L:(s + 1) * L, :]
        zprev = jnp.where(row == 0, 0.0, pltpu.roll(z, 1, 0))
        znext = jnp.where(row == L - 1, 0.0, pltpu.roll(z, L - 1, 0))
        zc = zprev * cw_ref[0:1, :] + z * cw_ref[1:2, :] + znext * cw_ref[2:3, :] + cb_ref[...]
        y = zc[:, 0:BR]
        for o in range(2):
            x = _dot(wf_ref[...], y.astype(BF16))
            xr, xi = x[0:n], x[n:2 * n]
            hr, hi = hf_ref[o, 0], hf_ref[o, 1]
            yc = jnp.concatenate([xr * hr - xi * hi, xr * hi + xi * hr], axis=0).astype(BF16)
            y = zc[:, (o + 1) * BR:(o + 2) * BR] * (_dot(wi_ref[...], yc) + y * bias_ref[o:o + 1, :])
        o_ref[s * L:(s + 1) * L, :] = y


def _hy_direct(z, cw, cb, bias, tabs, hf, L):
    T = z.shape[0]
    wf, wi = tabs
    nseq = 4
    const = lambda shp: pl.BlockSpec(shp, lambda i: (0,) * len(shp))
    return pl.pallas_call(
        functools.partial(_hy_direct_kernel, L=L, nseq=nseq), grid=(T // (nseq * L),),
        in_specs=[pl.BlockSpec((nseq * L, 3 * BR), lambda i: (i, 0)),
                  const((3, 3 * BR)), const((1, 3 * BR)), const((4 * L, L)), const((L, 4 * L)),
                  const((2, 2, 2 * L, BR)), const((2, BR))],
        out_specs=pl.BlockSpec((nseq * L, BR), lambda i: (i, 0)),
        out_shape=jax.ShapeDtypeStruct((T, BR), F32),
        compiler_params=_cparams("parallel"), name="hy_direct",
    )(z, cw, cb.reshape(1, 3 * BR), wf, wi, hf, bias)


HY_N1 = 16
HY_K1 = HY_N1 // 2 + 1
HY_HALVES = 1


def _hy_split_tables(L):
    n = 2 * L
    n2 = n // HY_N1
    hk = n2 // HY_HALVES
    k1 = _iota(HY_K1)[:, None, None]
    k2 = _iota(n2)[None, :, None]
    j2 = _iota(n2)[None, None, :]
    c, s = _cs(j2 * k2 * HY_N1 + j2 * k1, n)
    top = jnp.concatenate([c, s], axis=2).reshape(HY_K1, HY_HALVES, hk, 2 * n2)
    bot = jnp.concatenate([-s, c], axis=2).reshape(HY_K1, HY_HALVES, hk, 2 * n2)
    wf = jnp.concatenate([top, bot], axis=2)
    ct = jnp.swapaxes(c, 1, 2).reshape(HY_K1, n2, HY_HALVES, hk).transpose(0, 2, 1, 3)
    st = jnp.swapaxes(s, 1, 2).reshape(HY_K1, n2, HY_HALVES, hk).transpose(0, 2, 1, 3)
    wi = jnp.concatenate([jnp.concatenate([ct, -st], axis=3),
                          jnp.concatenate([st, ct], axis=3)], axis=2)
    kk = _iota(HY_K1)[:, None]
    c1, s1 = _cs(kk * _iota(HY_N1 // 2)[None, :], HY_N1)
    wgt = jnp.where((kk == 0) | (kk == HY_N1 // 2), 1.0, 2.0) / n
    co, so = _cs(kk * (_iota(HY_N1 // 2)[None, :] + HY_N1 // 4), HY_N1)
    coef = jnp.concatenate([c1, -s1, wgt * co, -wgt * so], axis=1)
    return wf.astype(BF16), wi.astype(BF16), coef.astype(F32)


def _hy_split_stage1(coef_ref, v_ref, a_ref, k1, n2):
    nin = HY_N1 // 2
    ar = ai = None
    for j in range(nin):
        xj = v_ref[0, j * n2:(j + 1) * n2, :]
        tr, ti = coef_ref[k1, j] * xj, coef_ref[k1, nin + j] * xj
        ar, ai = (tr, ti) if ar is None else (ar + tr, ai + ti)
    a_ref[0:n2, :] = ar.astype(BF16)
    a_ref[n2:2 * n2, :] = ai.astype(BF16)


def _hy_spec_split_kernel(coef_ref, f_ref, wf_ref, o_ref, a_ref, *, L):
    n2 = 2 * L // HY_N1

    @pl.when(pl.program_id(2) == 0)
    def _():
        _hy_split_stage1(coef_ref, f_ref, a_ref, pl.program_id(1), n2)

    o_ref[0, 0, 0] = _dot(wf_ref[0, 0], a_ref[...])


def _hy_split_kernel(coef_ref, v_ref, xg_ref, bias_ref, wf_ref, wi_ref, hf_ref, o_ref, a_ref, *, L):
    k1, hh = pl.program_id(1), pl.program_id(2)
    n2 = 2 * L // HY_N1
    hk = n2 // HY_HALVES
    nin = HY_N1 // 2

    @pl.when(hh == 0)
    def _():
        _hy_split_stage1(coef_ref, v_ref, a_ref, k1, n2)

    @pl.when((k1 == 0) & (hh == 0))
    def _():
        o_ref[...] = jnp.zeros_like(o_ref)

    x = _dot(wf_ref[0, 0], a_ref[...])
    xr, xi = x[0:hk], x[hk:2 * hk]
    hr, hi = hf_ref[0, 0, 0:hk, :], hf_ref[0, 0, hk:2 * hk, :]
    yc = jnp.concatenate([xr * hr - xi * hi, xr * hi + xi * hr], axis=0).astype(BF16)
    b = _dot(wi_ref[0, 0], yc)
    for o in range(nin):
        o_ref[0, o * n2:(o + 1) * n2, :] += (coef_ref[k1, 2 * nin + o] * b[0:n2]
                                             + coef_ref[k1, 3 * nin + o] * b[n2:2 * n2])

    @pl.when((k1 == HY_K1 - 1) & (hh == HY_HALVES - 1))
    def _():
        o_ref[0] = xg_ref[0] * (o_ref[0] + v_ref[0] * bias_ref[...])


def _hy_spec_split(filt2, tabs, L):
    wf, _, coef = tabs
    n2 = 2 * L // HY_N1
    hk = n2 // HY_HALVES
    return pl.pallas_call(
        functools.partial(_hy_spec_split_kernel, L=L), grid=(2, HY_K1, HY_HALVES),
        in_specs=[pl.BlockSpec(memory_space=pltpu.SMEM),
                  pl.BlockSpec((1, L, BR), lambda o, k, h: (o, 0, 0)),
                  pl.BlockSpec((1, 1, 2 * hk, 2 * n2), lambda o, k, h: (k, h, 0, 0))],
        out_specs=pl.BlockSpec((1, 1, 1, 2 * hk, BR), lambda o, k, h: (o, k, h, 0, 0)),
        out_shape=jax.ShapeDtypeStruct((2, HY_K1, HY_HALVES, 2 * hk, BR), F32),
        scratch_shapes=[pltpu.VMEM((2 * n2, BR), BF16)],
        compiler_params=_cparams("parallel", "arbitrary", "arbitrary"), name="hy_spec_split",
    )(coef, filt2, wf)


def _hy_split(v, xg, bias, hf, tabs, L):
    wf, wi, coef = tabs
    B = v.shape[0]
    n2 = 2 * L // HY_N1
    hk = n2 // HY_HALVES
    seq = pl.BlockSpec((1, L, BR), lambda b, k, h: (b, 0, 0))
    return pl.pallas_call(
        functools.partial(_hy_split_kernel, L=L), grid=(B, HY_K1, HY_HALVES),
        in_specs=[pl.BlockSpec(memory_space=pltpu.SMEM), seq, seq,
                  pl.BlockSpec((1, BR), lambda b, k, h: (0, 0)),
                  pl.BlockSpec((1, 1, 2 * hk, 2 * n2), lambda b, k, h: (k, h, 0, 0)),
                  pl.BlockSpec((1, 1, 2 * n2, 2 * hk), lambda b, k, h: (k, h, 0, 0)),
                  pl.BlockSpec((1, 1, 2 * hk, BR), lambda b, k, h: (k, h, 0, 0))],
        out_specs=seq,
        out_shape=jax.ShapeDtypeStruct((B, L, BR), F32),
        scratch_shapes=[pltpu.VMEM((2 * n2, BR), BF16)],
        compiler_params=_cparams("parallel", "arbitrary", "arbitrary"), name="hy_split",
    )(coef, v, xg, bias.reshape(1, BR), wf, wi, hf)


def _hy_filters(L, w1, b1, w2, b2, w3):
    dt = w1.dtype
    pos = jnp.arange(L, dtype=dt)
    t01 = jnp.linspace(0.0, 1.0, L, dtype=dt)[:, None]
    w = (2.0 * math.pi / L) * pos[:, None]
    bands = jnp.linspace(1e-4, HY_BANDS - 1, HY_BANDS, dtype=dt)[None, :]
    feats = jnp.concatenate([t01, jnp.cos(bands * w), -jnp.sin(bands * w)], axis=-1)
    h = jnp.sin(feats @ w1 + b1)
    h = jnp.sin(h @ w2 + b2)
    h = h @ w3
    dist = jnp.abs(pos - (L // 2)) / L
    decay = jnp.abs(jnp.linspace(math.log(HY_DECAY_TARGET) / HY_SLOW,
                                 math.log(HY_DECAY_TARGET) / HY_FAST, 2 * BR, dtype=dt))
    return h * jnp.exp(-dist[:, None] * decay[None, :])


def _attn_kernel(lam_ref, q_ref, kt_ref, v_ref, sub_ref, o_ref, *, qscale, post):
    q = (q_ref[0] * qscale).astype(BF16)
    lam = lam_ref[0, 0]
    sub = sub_ref[...]
    for h in range(HEADS):
        vh = v_ref[0, :, h * VAL_DIM:(h + 1) * VAL_DIM]
        res = []
        for m in range(2):
            c0 = h * 2 * HEAD_DIM + m * HEAD_DIM
            s = _dot(q[:, c0:c0 + HEAD_DIM], kt_ref[0, c0:c0 + HEAD_DIM, :])
            p = jnp.exp2(s - jnp.max(s, axis=-1, keepdims=True))
            l = jnp.sum(p, axis=-1, keepdims=True)
            res.append(_dot(p.astype(BF16), vh) / l)
        o = res[0] - lam * res[1]
        o = o * lax.rsqrt(jnp.mean(o * o, axis=-1, keepdims=True) + EPS) * sub * post
        o_ref[0, :, h * VAL_DIM:(h + 1) * VAL_DIM] = o


def _attn(lam, q, kt, v, subln, tq, post):
    B, L, _ = q.shape
    Lk = kt.shape[2]
    return pl.pallas_call(
        functools.partial(_attn_kernel, qscale=HEAD_DIM ** -0.5 * LOG2E, post=post),
        grid=(B, L // tq),
        in_specs=[pl.BlockSpec(memory_space=pltpu.SMEM),
                  pl.BlockSpec((1, tq, BR), lambda b, i: (b, i, 0)),
                  pl.BlockSpec((1, BR, Lk), lambda b, i: (b, 0, 0)),
                  pl.BlockSpec((1, Lk, BR), lambda b, i: (b, 0, 0)),
                  pl.BlockSpec((1, VAL_DIM), lambda b, i: (0, 0))],
        out_specs=pl.BlockSpec((1, tq, BR), lambda b, i: (b, i, 0)),
        out_shape=jax.ShapeDtypeStruct((B, L, BR), F32),
        compiler_params=_cparams("parallel", "parallel"), name="diff_attn",
    )(lam.reshape(1, 1), q, kt, v, subln.reshape(1, VAL_DIM))


def _attn_inputs(k3, v3, ctx):
    B = k3.shape[0]
    if ctx is not None:
        k3 = jnp.concatenate([ctx[0].reshape(B, -1, BR), k3], axis=1)
        v3 = jnp.concatenate([ctx[1].reshape(B, -1, BR), v3], axis=1)
    return jnp.swapaxes(k3, 1, 2).astype(BF16), v3.astype(BF16)


def _attn_latent_test(q, k, v, ck, cv):
    kt, vb = _attn_inputs(k, v, (ck, cv))
    return _attn(jnp.float32(0.5), q, kt, vb, jnp.ones((VAL_DIM,), F32), 256, 0.5)


def _rope_tables(L, dt):
    n_rows = L // GRID_WIDTH
    row = jnp.repeat(jnp.arange(n_rows, dtype=dt), GRID_WIDTH)
    col = (jnp.arange(L) % GRID_WIDTH).astype(dt)
    inv = ROPE_BASE ** (-jnp.arange(0, ROPE_DIM, 2, dtype=dt) / ROPE_DIM)
    ar = row[:, None] * inv[None, :]
    ac = col[:, None] * inv[None, :]
    ang = jnp.concatenate([ar, ar, ac, ac], axis=-1)
    sign = jnp.tile(jnp.concatenate([-jnp.ones(ROPE_DIM // 2, dt), jnp.ones(ROPE_DIM // 2, dt)]), 2)
    reps = BR // HEAD_DIM
    return jnp.tile(jnp.cos(ang), (1, reps)), jnp.tile(jnp.sin(ang) * sign, (1, reps))


def _mixout_kernel(x_ref, mod_ref, u_ref, yf_ref, yb_ref, uv_ref, hy_ref, da_ref,
                   sd_ref, wg_ref, bg_ref, sn_ref, ws_ref, bs_ref, bn_ref, wo_ref, gf_ref, rt_ref,
                   xo_ref, h2_ref, cb_ref, *, route):
    tm = x_ref.shape[0]
    u = u_ref[...]
    y = jax.nn.gelu(sd_ref[...] * u + yf_ref[...] + yb_ref[...])
    y = y * jax.nn.sigmoid(_dot(y.astype(BF16), wg_ref[...]) + bg_ref[...])
    acc = _dot(_rms(y, bn_ref[0:1, :]).astype(BF16), wo_ref[0:BR, :])
    guv = jax.nn.gelu(uv_ref[...])
    gu = guv[:, 0:BR]
    gv = _rms(guv[:, BR:2 * BR], sn_ref[...]).astype(BF16)
    lane = lax.broadcasted_iota(jnp.int32, (SGU_CHUNK, BR), 1)
    hd = BR // SGU_HEADS
    zs = []
    for c in range(tm // SGU_CHUNK):
        vc = gv[c * SGU_CHUNK:(c + 1) * SGU_CHUNK, :]
        z = bs_ref[...]
        for h in range(SGU_HEADS):
            z = z + jnp.where(lane // hd == h, _dot(ws_ref[h], vc), 0.0)
        zs.append(z)
    z = zs[0] if len(zs) == 1 else jnp.concatenate(zs, axis=0)
    acc = acc + _dot(_rms(gu * z, bn_ref[1:2, :]).astype(BF16), wo_ref[BR:2 * BR, :])
    acc = acc + _dot(_rms(hy_ref[...], bn_ref[2:3, :]).astype(BF16), wo_ref[2 * BR:3 * BR, :])
    acc = acc + _dot(da_ref[...].astype(BF16), wo_ref[3 * BR:4 * BR, :])
    xn = x_ref[...] + mod_ref[:, 2 * D:3 * D] * acc
    xo_ref[...] = xn
    h2 = _rms(xn, gf_ref[...]) * (1.0 + mod_ref[:, 4 * D:5 * D]) + mod_ref[:, 3 * D:4 * D]
    h2_ref[...] = h2.astype(BF16)
    if not route:
        cb_ref[...] = jnp.zeros_like(cb_ref)
        return
    logits = _dot3(h2, rt_ref[...])
    el = lax.broadcasted_iota(jnp.int32, logits.shape, 1)
    logits = jnp.where(el < N_EXPERTS, logits, -jnp.inf)
    e = jnp.exp(logits - jnp.max(logits, axis=-1, keepdims=True))
    probs = e / jnp.sum(e, axis=-1, keepdims=True)
    big = logits.shape[1]
    m1 = jnp.max(probs, axis=-1, keepdims=True)
    i1 = jnp.min(jnp.where(probs == m1, el, big), axis=-1, keepdims=True)
    p2 = jnp.where((el == i1) | (el >= N_EXPERTS), -1.0, probs)
    m2 = jnp.max(p2, axis=-1, keepdims=True)
    i2 = jnp.min(jnp.where(p2 == m2, el, big), axis=-1, keepdims=True)
    tot = m1 + m2
    cb_ref[...] = jnp.where(el == i1, m1 / tot, 0.0) + jnp.where(el == i2, m2 / tot, 0.0)


def _mixout(x, modg, u_tm2, yf2, yb2, uv, hy, da, lp, B, L, tm, route):
    T = B * L
    nt = L // tm
    tok = lambda n: pl.BlockSpec((tm, n), lambda i: (i, 0))
    tmaj = pl.BlockSpec((tm, BR), lambda i: (i % nt, i // nt))
    const = lambda shp: pl.BlockSpec(shp, lambda i: (0,) * len(shp))
    return pl.pallas_call(
        functools.partial(_mixout_kernel, route=route), grid=(T // tm,),
        in_specs=[tok(D), _mod_spec(modg, tm, L),
                  tmaj, tmaj, tmaj, tok(2 * BR), tok(BR), tok(BR),
                  const((1, BR)), const((BR, BR)), const((1, BR)), const((1, BR)),
                  const((SGU_HEADS, SGU_CHUNK, SGU_CHUNK)), const((SGU_CHUNK, BR)),
                  const((3, BR)), pl.BlockSpec((None, D, D), lambda i: (lp['l'], 0, 0)),
                  const((1, D)), const((D, 128))],
        out_specs=[tok(D), tok(D), tok(128)],
        out_shape=[jax.ShapeDtypeStruct((T, D), F32), jax.ShapeDtypeStruct((T, D), BF16),
                   jax.ShapeDtypeStruct((T, 128), F32)],
        compiler_params=_cparams("parallel"), name="mix_out",
    )(x, modg, u_tm2, yf2, yb2, uv, hy, da,
      lp['s5_d'], lp['s5_w_glu'], lp['s5_b_glu'], lp['sgu_norm'], lp['sgu_w_s'], lp['sgu_b'],
      lp['branch_norm'], lp['w_out'], lp['norm_ffn'], lp['router'])


def _ffn_kernel(h_ref, x_ref, mod_ref, w1_ref, w3_ref, w2_ref, o_ref, acc_ref):
    j = pl.program_id(1)
    h = h_ref[...]
    a = _dot(h, w1_ref[...])
    t = (a * jax.nn.sigmoid(a) * _dot(h, w3_ref[...])).astype(BF16)
    part = _dot(t, w2_ref[...])

    @pl.when(j == 0)
    def _():
        acc_ref[...] = part

    @pl.when(j > 0)
    def _():
        acc_ref[...] += part

    @pl.when(j == pl.num_programs(1) - 1)
    def _():
        o_ref[...] = x_ref[...] + mod_ref[:, 5 * D:6 * D] * acc_ref[...]


def _ffn(h2, x, modg, w1, w3, w2, jl, L, tm, tf):
    T = x.shape[0]
    dff = w1.shape[2]
    return pl.pallas_call(
        _ffn_kernel, grid=(T // tm, dff // tf),
        in_specs=[pl.BlockSpec((tm, D), lambda i, j: (i, 0)),
                  pl.BlockSpec((tm, D), lambda i, j: (i, 0)),
                  _mod_spec(modg, tm, L),
                  pl.BlockSpec((None, D, tf), lambda i, j: (jl, 0, j)),
                  pl.BlockSpec((None, D, tf), lambda i, j: (jl, 0, j)),
                  pl.BlockSpec((None, tf, D), lambda i, j: (jl, j, 0))],
        out_specs=pl.BlockSpec((tm, D), lambda i, j: (i, 0)),
        out_shape=jax.ShapeDtypeStruct((T, D), F32),
        scratch_shapes=[pltpu.VMEM((tm, D), F32)],
        compiler_params=_cparams("parallel", "arbitrary"), name="ffn",
    )(h2, x, modg, w1, w3, w2)


def _route_kernel(cb_ref, rk_ref, rkt_ref, cnt_ref):
    tm = cb_ref.shape[0]
    mask = cb_ref[...] > 0.0
    mf = jnp.where(mask, 1.0, 0.0)
    r = lax.broadcasted_iota(jnp.int32, (tm, tm), 0)
    c = lax.broadcasted_iota(jnp.int32, (tm, tm), 1)
    before = jnp.where(c < r, 1.0, 0.0).astype(BF16)
    rank = jnp.where(mask, _dot(before, mf.astype(BF16)), -1.0)
    rk_ref[...] = rank
    rkt_ref[...] = rank.T[0:N_EXPERTS, :]
    cnt_ref[...] = jnp.sum(mf, axis=0, keepdims=True)


def _route(cb, tm):
    T = cb.shape[0]
    nt = T // tm
    rk, rkt, cnt = pl.pallas_call(
        _route_kernel, grid=(nt,),
        in_specs=[pl.BlockSpec((tm, 128), lambda i: (i, 0))],
        out_specs=[pl.BlockSpec((tm, 128), lambda i: (i, 0)),
                   pl.BlockSpec((None, N_EXPERTS, tm), lambda i: (i, 0, 0)),
                   pl.BlockSpec((None, 1, 128), lambda i: (i, 0, 0))],
        out_shape=[jax.ShapeDtypeStruct((T, 128), F32),
                   jax.ShapeDtypeStruct((nt, N_EXPERTS, tm), F32),
                   jax.ShapeDtypeStruct((nt, 1, 128), F32)],
        compiler_params=_cparams("parallel"), name="moe_route",
    )(cb)
    return rk, rkt, cnt[:, 0, :N_EXPERTS].astype(jnp.int32).reshape(-1)


MOE_TF = 1792
MOE_CHUNK = 144


def _moe_kernel(cnt_ref, h_ref, x_ref, mod_ref, cb_ref, rk_ref, rkt_ref, w1_ref, w3_ref, w2_ref,
                o_ref, xg_ref, y_ref):
    i, e, j = pl.program_id(0), pl.program_id(1), pl.program_id(2)
    ne, nj = pl.num_programs(1), pl.num_programs(2)
    tm = h_ref.shape[0]
    ch = MOE_CHUNK
    nch = (cnt_ref[i * ne + e] + (ch - 1)) // ch

    def rows(c):
        return pl.ds(pl.multiple_of(c * ch, ch), ch)

    @pl.when((e == 0) & (j == 0))
    def _():
        o_ref[...] = jnp.zeros_like(o_ref)
        y_ref[...] = jnp.zeros_like(y_ref)

    @pl.when(j == 0)
    def _():
        rid = lax.broadcasted_iota(jnp.int32, (ch, tm), 0).astype(F32)
        rrow = rkt_ref[pl.ds(e, 1), :]

        def gather(c, _):
            sel = jnp.where(rid == rrow - (c * ch).astype(F32), 1.0, 0.0).astype(BF16)
            xg_ref[rows(c), :] = _dot(sel, h_ref[...]).astype(BF16)
            return 0

        lax.fori_loop(0, nch, gather, 0)

    def expert(c, _):
        xg = xg_ref[rows(c), :]
        a = _dot(xg, w1_ref[0])
        t = (a * jax.nn.sigmoid(a) * _dot(xg, w3_ref[0])).astype(BF16)
        part = _dot(t, w2_ref[0])

        @pl.when(j == 0)
        def _():
            y_ref[rows(c), :] = part

        @pl.when(j > 0)
        def _():
            y_ref[rows(c), :] += part

        return 0

    lax.fori_loop(0, nch, expert, 0)

    @pl.when(j == nj - 1)
    def _():
        el = lax.broadcasted_iota(jnp.int32, (tm, 128), 1)
        rcol = jnp.sum(jnp.where(el == e, rk_ref[...], 0.0), axis=-1, keepdims=True)
        wcol = jnp.sum(jnp.where(el == e, cb_ref[...], 0.0), axis=-1, keepdims=True)
        cid = lax.broadcasted_iota(jnp.int32, (tm, 2 * ch), 1).astype(F32)

        def scatter(c, _):
            r0 = pl.multiple_of(c * (2 * ch), 2 * ch)
            selt = jnp.where(cid == rcol - r0.astype(F32), 1.0, 0.0).astype(BF16)
            o_ref[...] += wcol * _dot(selt, y_ref[pl.ds(r0, 2 * ch), :].astype(BF16))
            return 0

        lax.fori_loop(0, (nch + 1) // 2, scatter, 0)

    @pl.when((e == ne - 1) & (j == nj - 1))
    def _():
        o_ref[...] = x_ref[...] + mod_ref[:, 5 * D:6 * D] * o_ref[...]


def _moe_chunk_weights(w):
    nm, ne, d, dff = w.shape
    return w.reshape(nm, ne, d, dff // MOE_TF, MOE_TF).transpose(0, 1, 3, 2, 4)


def _moe(h2, x, modg, cb, w1, w3, w2, jl, L, tm):
    T = x.shape[0]
    _, ne, nj, _, tf = w1.shape
    dff = nj * tf
    rk, rkt, cnt = _route(cb, tm)
    nmod = modg.shape[0]
    rows = -(-tm // (2 * MOE_CHUNK)) * 2 * MOE_CHUNK
    grid_spec = pltpu.PrefetchScalarGridSpec(
        num_scalar_prefetch=1, grid=(T // tm, ne, dff // tf),
        in_specs=[pl.BlockSpec((tm, D), lambda i, e, j, c: (i, 0)),
                  pl.BlockSpec((tm, D), lambda i, e, j, c: (i, 0)),
                  pl.BlockSpec((None, 1, 6 * D), lambda i, e, j, c: ((i * tm // L) % nmod, 0, 0)),
                  pl.BlockSpec((tm, 128), lambda i, e, j, c: (i, 0)),
                  pl.BlockSpec((tm, 128), lambda i, e, j, c: (i, 0)),
                  pl.BlockSpec((None, N_EXPERTS, tm), lambda i, e, j, c: (i, 0, 0)),
                  pl.BlockSpec((None, 1, None, D, tf), lambda i, e, j, c: (jl, e, j, 0, 0)),
                  pl.BlockSpec((None, 1, None, D, tf), lambda i, e, j, c: (jl, e, j, 0, 0)),
                  pl.BlockSpec((None, 1, tf, D), lambda i, e, j, c: (jl, e, j, 0))],
        out_specs=pl.BlockSpec((tm, D), lambda i, e, j, c: (i, 0)),
        scratch_shapes=[pltpu.VMEM((rows, D), BF16), pltpu.VMEM((rows, D), F32)])
    return pl.pallas_call(
        _moe_kernel, grid_spec=grid_spec,
        out_shape=jax.ShapeDtypeStruct((T, D), F32),
        compiler_params=_cparams("parallel", "arbitrary", "arbitrary"), name="moe",
    )(cnt, h2, x, modg, cb, rk, rkt, w1, w3, w2)


def _final_kernel(x_ref, g_ref, o_ref):
    o_ref[...] = _rms(x_ref[...], g_ref[...])


def _final_norm(x, g, tm):
    T = x.shape[0]
    return pl.pallas_call(
        _final_kernel, grid=(T // tm,),
        in_specs=[pl.BlockSpec((tm, D), lambda i: (i, 0)), pl.BlockSpec((1, D), lambda i: (0, 0))],
        out_specs=pl.BlockSpec((tm, D), lambda i: (i, 0)),
        out_shape=jax.ShapeDtypeStruct((T, D), F32),
        compiler_params=_cparams("parallel"), name="final_norm",
    )(x, g.reshape(1, D))


def _layer(x, modg, lp, l, B, L, ctx):
    is_ctx = ctx is None
    T = B * L
    Bp = -(-B // 8) * 8
    tm = min(L, 512)
    rope_tabs = None if is_ctx else _rope_tables(L, F32)
    u_tm, uv, z, q, k, v = _inproj(x, modg, lp['norm_mix'], lp['w_in'], l, B, L, tm, rope_tabs)

    u3 = jnp.pad(u_tm.reshape(L, B, BR), ((0, 0), (0, Bp - B), (0, 0)))
    if is_ctx:
        h0 = jnp.zeros((2, Bp, 2 * S5_STATES), F32)
    else:
        h0 = ctx[2]
        h0 = jnp.transpose(h0, (1, 0, 4, 2, 3)).reshape(2, B, 2 * S5_STATES)
        h0 = jnp.pad(h0, ((0, 0), (0, Bp - B), (0, 0)))
    yf, yb, hfin = _s5(u3, h0, lp['s5_wb'], lp['s5_wc'], lp['s5_a'], L, Bp, min(L, 256))

    tabs, hf = lp['hy'][L]
    if L <= HY_DIRECT_MAX:
        y_hy = _hy_direct(z, lp['hy_conv_w'], lp['hy_conv_b'], lp['hy_bias'], tabs, hf, L)
    else:
        hv, hx1, hx2 = _short_conv(z, lp['hy_conv_w'], lp['hy_conv_b'], L, tm)
        y1 = _hy_split(hv.reshape(B, L, BR), hx1.reshape(B, L, BR), lp['hy_bias'][0], hf[0], tabs, L)
        y_hy = _hy_split(y1, hx2.reshape(B, L, BR), lp['hy_bias'][1], hf[1], tabs, L).reshape(T, BR)

    q3 = q.reshape(B, L, BR)
    k3 = k.reshape(B, L, BR)
    v3 = v.reshape(B, L, BR)
    kt, vb = _attn_inputs(k3, v3, None if is_ctx else ctx)
    da = _attn(lp['da_lam'], q3, kt, vb, lp['da_subln'], min(L, 256), 1.0 - lp['lam_init'])

    x, h2, cb = _mixout(x, modg, u_tm, yf.reshape(L, Bp * BR), yb.reshape(L, Bp * BR), uv,
                        y_hy, da.reshape(T, BR), lp, B, L, tm, l % 2 == 1)
    tmf = min(T, 1024)
    if l % 2 == 0:
        x = _ffn(h2, x, modg, lp['ffn_w1'], lp['ffn_w3'], lp['ffn_w2'], l // 2, L, tmf,
                 lp['ffn_w1'].shape[2] // 2)
    else:
        x = _moe(h2, x, modg, cb, lp['moe_w1'], lp['moe_w3'], lp['moe_w2'], l // 2, L, tmf)
    if is_ctx:
        fin = hfin[:, :B].reshape(2, B, 2, S5_G, S5_N)
        fin = jnp.transpose(fin, (1, 0, 3, 4, 2))
        return x, k3.reshape(B, L, HEADS, 2 * HEAD_DIM), v3.reshape(B, L, HEADS, VAL_DIM), fin
    return x


def kernel(x_prompt, x_sample, cache_k, cache_v, state_ssm, c, c_ctx, w_ada, b_ada, norm_mix, norm_ffn, w_in, w_out, branch_norm, s5_lam_re, s5_lam_im, s5_log_dt, s5_b_re, s5_b_im, s5_c_re, s5_c_im, s5_d, s5_w_glu, s5_b_glu, sgu_norm, sgu_w_s, sgu_b_s, hy_conv_w, hy_conv_b, hy_w1, hy_b1, hy_w2, hy_b2, hy_w3, hy_bias, da_lq1, da_lk1, da_lq2, da_lk2, da_subln, ffn_w1, ffn_w3, ffn_w2, moe_router, moe_w1, moe_w3, moe_w2, norm_final):
    depth = w_in.shape[0]
    Bc, Lc, _ = x_prompt.shape
    Bs, Ls, _ = x_sample.shape

    cond = jnp.concatenate([c_ctx[None, :], c], axis=0)
    cond8 = jnp.pad(cond, ((0, 8 - cond.shape[0]), (0, 0)))
    mod = _ada(cond8, w_ada, b_ada)

    def hyena_tables(L):
        return _hy_direct_tables(L) if L <= HY_DIRECT_MAX else _hy_split_tables(L)

    def hyena_spectrum(L, tabs, *hy_args):
        filt = _hy_filters(L, *hy_args)
        if L <= HY_DIRECT_MAX:
            return _hy_spec_direct(filt, tabs[0])
        return _hy_spec_split(filt.reshape(L, 2, BR).transpose(1, 0, 2), tabs, L)

    hy_tabs = {L: hyena_tables(L) for L in {Lc, Ls}}

    bf = {name: w.astype(BF16) for name, w in dict(
        w_in=w_in, w_out=w_out, ffn_w1=ffn_w1, ffn_w3=ffn_w3, ffn_w2=ffn_w2,
        moe_w1=moe_w1, moe_w3=moe_w3, moe_w2=moe_w2).items()}
    bf['moe_w1'] = _moe_chunk_weights(bf['moe_w1'])
    bf['moe_w3'] = _moe_chunk_weights(bf['moe_w3'])

    layers = []
    for l in range(depth):
        j = l // 2
        wb, wc, a = _s5_prep(s5_lam_re[l], s5_lam_im[l], s5_log_dt[l], s5_b_re[l], s5_b_im[l],
                             s5_c_re[l], s5_c_im[l])
        lam_init = 0.8 - 0.6 * math.exp(-0.3 * l)
        lam = (jnp.exp(jnp.sum(da_lq1[l] * da_lk1[l])) - jnp.exp(jnp.sum(da_lq2[l] * da_lk2[l]))
               + lam_init)
        hy_args = (hy_w1[l], hy_b1[l], hy_w2[l], hy_b2[l], hy_w3[l])
        lp = dict(
            l=l, norm_mix=norm_mix[l], norm_ffn=norm_ffn[l].reshape(1, D), branch_norm=branch_norm[l],
            s5_wb=wb, s5_wc=wc, s5_a=a, s5_d=s5_d[l].reshape(1, BR),
            s5_w_glu=s5_w_glu[l].astype(BF16), s5_b_glu=s5_b_glu[l].reshape(1, BR),
            sgu_norm=sgu_norm[l].reshape(1, BR), sgu_w_s=sgu_w_s[l].astype(BF16),
            sgu_b=jnp.repeat(sgu_b_s[l].T, BR // SGU_HEADS, axis=1),
            hy_conv_w=hy_conv_w[l], hy_conv_b=hy_conv_b[l], hy_bias=hy_bias[l],
            hy={L: (t, hyena_spectrum(L, t, *hy_args)) for L, t in hy_tabs.items()},
            da_lam=lam, lam_init=lam_init, da_subln=da_subln[l],
        )
        lp.update(bf)
        if l % 2 == 0:
            lp.update(router=jnp.zeros((D, 128), F32))
        else:
            lp.update(router=jnp.pad(moe_router[j], ((0, 0), (0, 128 - N_EXPERTS))))
        layers.append(lp)

    xc = x_prompt.reshape(Bc * Lc, D)
    ks, vs, ss = [], [], []
    for l in range(depth):
        xc, k_l, v_l, s_l = _layer(xc, mod[l, 0:1].reshape(1, 1, 6 * D), layers[l], l, Bc, Lc, None)
        ks.append(k_l)
        vs.append(v_l)
        ss.append(s_l)
    y_prompt = _final_norm(xc, norm_final, 512).reshape(Bc, Lc, D)

    xs = x_sample.reshape(Bs * Ls, D)
    for l in range(depth):
        xs = _layer(xs, mod[l, 1:1 + Bs].reshape(Bs, 1, 6 * D), layers[l], l, Bs, Ls,
                    (cache_k[:, l], cache_v[:, l], state_ssm[:, l]))
    y_sample = _final_norm(xs, norm_final, 512).reshape(Bs, Ls, D)
    return (y_prompt, y_sample, jnp.stack(ks, axis=1), jnp.stack(vs, axis=1), jnp.stack(ss, axis=1))
```

```python
import functools
import math

import jax
import jax.numpy as jnp
import numpy as np
from jax import lax
from jax.experimental import pallas as pl
from jax.experimental.pallas import tpu as pltpu

F32 = jnp.float32
BF16 = jnp.bfloat16

D = 1024
BR = 256
PROJ = 9 * BR
S5_G, S5_N, S5_P = 16, 64, 16
S5_STATES = S5_G * S5_N
SGU_CHUNK, SGU_HEADS = 128, 4
HEADS, HEAD_DIM, VAL_DIM = 4, 32, 64
GRID_WIDTH = 64
ROPE_DIM = HEAD_DIM // 2
ROPE_BASE = 10000.0
HY_EMB, HY_BANDS = 33, 16
HY_DECAY_TARGET, HY_FAST, HY_SLOW = 1e-2, 0.3, 1.5
N_EXPERTS = 8
EPS = 1e-6
LOG2E = 1.4426950408889634

HY_DIRECT_MAX = 512
VMEM_LIMIT = 56 * 1024 * 1024


def _cparams(*sem):
    return pltpu.CompilerParams(dimension_semantics=sem, vmem_limit_bytes=VMEM_LIMIT)


def _dot(a, b):
    return jnp.dot(a, b, preferred_element_type=F32)


def _dot3(a, b):
    a_hi = a.astype(BF16)
    b_hi = b.astype(BF16)
    a_lo = (a - a_hi.astype(F32)).astype(BF16)
    b_lo = (b - b_hi.astype(F32)).astype(BF16)
    return _dot(a_hi, b_hi) + (_dot(a_hi, b_lo) + _dot(a_lo, b_hi))


def _rms(x, g):
    return x * lax.rsqrt(jnp.mean(x * x, axis=-1, keepdims=True) + EPS) * g


def _mod_spec(modg, tm, L):
    nmod = modg.shape[0]
    return pl.BlockSpec((None, 1, 6 * D), lambda i, *_: ((i * tm // L) % nmod, 0, 0))


def _ada_kernel(c_ref, w_ref, b_ref, o_ref):
    c = c_ref[...]
    s = (c * jax.nn.sigmoid(c)).astype(BF16)
    o_ref[0] = _dot(s, w_ref[0].astype(BF16)) + b_ref[0]


def _ada(cond8, w_ada, b_ada):
    depth = w_ada.shape[0]
    tn = 1536
    return pl.pallas_call(
        _ada_kernel,
        grid=(depth, 6 * D // tn),
        in_specs=[pl.BlockSpec((8, D), lambda l, j: (0, 0)),
                  pl.BlockSpec((1, D, tn), lambda l, j: (l, 0, j)),
                  pl.BlockSpec((1, 1, tn), lambda l, j: (l, 0, j))],
        out_specs=pl.BlockSpec((1, 8, tn), lambda l, j: (l, 0, j)),
        out_shape=jax.ShapeDtypeStruct((depth, 8, 6 * D), F32),
        compiler_params=_cparams("parallel", "parallel"),
        name="ada",
    )(cond8, w_ada, b_ada.reshape(depth, 1, 6 * D))


def _inproj_kernel(*refs, rope):
    if rope:
        (x_ref, mod_ref, g_ref, w_ref, cos_ref, sin_ref,
         u_ref, uv_ref, z_ref, q_ref, k_ref, v_ref) = refs
    else:
        x_ref, mod_ref, g_ref, w_ref, u_ref, uv_ref, z_ref, q_ref, k_ref, v_ref = refs
    x = x_ref[...]
    h = _rms(x, g_ref[...]) * (1.0 + mod_ref[:, D:2 * D]) + mod_ref[:, 0:D]
    p = _dot(h.astype(BF16), w_ref[...])
    u_ref[...] = p[:, 0:BR]
    uv_ref[...] = p[:, BR:3 * BR]
    z_ref[...] = p[:, 3 * BR:6 * BR]
    q = p[:, 6 * BR:7 * BR]
    k = p[:, 7 * BR:8 * BR]
    if rope:
        cs = cos_ref[...]
        sn = sin_ref[...]
        lane = lax.broadcasted_iota(jnp.int32, q.shape, 1)
        first = (lane % (2 * (ROPE_DIM // 2))) < (ROPE_DIM // 2)
        half = ROPE_DIM // 2

        def rot(t):
            return jnp.where(first, pltpu.roll(t, BR - half, 1), pltpu.roll(t, half, 1))

        q = q * cs + rot(q) * sn
        k = k * cs + rot(k) * sn
    q_ref[...] = q
    k_ref[...] = k
    v_ref[...] = p[:, 8 * BR:9 * BR]


def _inproj(x, modg, g, w_bf, l, B, L, tm, rope_tabs):
    T = B * L
    nt = L // tm
    rope = rope_tabs is not None
    in_specs = [pl.BlockSpec((tm, D), lambda i: (i, 0)),
                _mod_spec(modg, tm, L),
                pl.BlockSpec((1, D), lambda i: (0, 0)),
                pl.BlockSpec((None, D, PROJ), lambda i: (l, 0, 0))]
    args = [x, modg, g.reshape(1, D), w_bf]
    if rope:
        in_specs += [pl.BlockSpec((tm, BR), lambda i: (i % nt, 0))] * 2
        args += list(rope_tabs)
    tok = lambda n: pl.BlockSpec((tm, n), lambda i: (i, 0))
    out_specs = [pl.BlockSpec((tm, BR), lambda i: (i % nt, i // nt)),
                 tok(2 * BR), tok(3 * BR), tok(BR), tok(BR), tok(BR)]
    out_shape = [jax.ShapeDtypeStruct((L, B * BR), F32),
                 jax.ShapeDtypeStruct((T, 2 * BR), F32),
                 jax.ShapeDtypeStruct((T, 3 * BR), F32),
                 jax.ShapeDtypeStruct((T, BR), F32),
                 jax.ShapeDtypeStruct((T, BR), F32),
                 jax.ShapeDtypeStruct((T, BR), F32)]
    return pl.pallas_call(
        functools.partial(_inproj_kernel, rope=rope),
        grid=(T // tm,), in_specs=in_specs, out_specs=out_specs, out_shape=out_shape,
        compiler_params=_cparams("parallel"), name="inproj",
    )(*args)


def _s5_kernel(uf_ref, ub_ref, h0_ref, wb_ref, wc_ref, a_ref, yf_ref, yb_ref, hfin_ref,
               xs_ref, hc_ref, *, tt):
    i = pl.program_id(1)
    last = pl.num_programs(1) - 1
    half = S5_STATES // 2

    @pl.when(i == 0)
    def _():
        hc_ref[...] = h0_ref[...]

    for d in range(2):
        u_ref = uf_ref if d == 0 else ub_ref
        y_ref = yf_ref if d == 0 else yb_ref
        u2 = u_ref[...].reshape(tt * 8, BR).astype(BF16)
        xs_ref[...] = _dot(u2, wb_ref[d])
        for c in range(2):
            cr = slice(c * half, (c + 1) * half)
            ci = slice(S5_STATES + c * half, S5_STATES + (c + 1) * half)
            ar = jnp.broadcast_to(a_ref[d, 0:1, cr], (8, half))
            ai = jnp.broadcast_to(a_ref[d, 1:2, cr], (8, half))

            def body(s, carry, cr=cr, ci=ci, ar=ar, ai=ai, d=d):
                hr, hi = carry
                t = s if d == 0 else tt - 1 - s
                r0 = pl.multiple_of(t * 8, 8)
                nr = ar * hr - ai * hi + xs_ref[pl.ds(r0, 8), cr]
                ni = ar * hi + ai * hr + xs_ref[pl.ds(r0, 8), ci]
                xs_ref[pl.ds(r0, 8), cr] = nr
                xs_ref[pl.ds(r0, 8), ci] = ni
                return nr, ni

            hr, hi = lax.fori_loop(0, tt, body, (hc_ref[d, :, cr], hc_ref[d, :, ci]), unroll=4)
            hc_ref[d, :, cr] = hr
            hc_ref[d, :, ci] = hi
        y = _dot(xs_ref[...].astype(BF16), wc_ref[d])
        y_ref[...] = y.reshape(tt, 8, BR)

    @pl.when(i == last)
    def _():
        hfin_ref[...] = hc_ref[...]


def _s5(u_tm, h0, wb, wc, a, L, Bp, tt):
    nT = L // tt
    ng = Bp // 8
    blk = lambda f: pl.BlockSpec((tt, 8, BR), f)
    const = lambda shp: pl.BlockSpec(shp, lambda g, i: (0,) * len(shp))
    return pl.pallas_call(
        functools.partial(_s5_kernel, tt=tt),
        grid=(ng, nT),
        in_specs=[blk(lambda g, i: (i, g, 0)), blk(lambda g, i: (nT - 1 - i, g, 0)),
                  pl.BlockSpec((2, 8, 2 * S5_STATES), lambda g, i: (0, g, 0)),
                  const((2, BR, 2 * S5_STATES)), const((2, 2 * S5_STATES, BR)),
                  const((2, 2, S5_STATES))],
        out_specs=[blk(lambda g, i: (i, g, 0)), blk(lambda g, i: (nT - 1 - i, g, 0)),
                   pl.BlockSpec((2, 8, 2 * S5_STATES), lambda g, i: (0, g, 0))],
        out_shape=[jax.ShapeDtypeStruct((L, Bp, BR), F32),
                   jax.ShapeDtypeStruct((L, Bp, BR), F32),
                   jax.ShapeDtypeStruct((2, Bp, 2 * S5_STATES), F32)],
        scratch_shapes=[pltpu.VMEM((tt * 8, 2 * S5_STATES), F32),
                        pltpu.VMEM((2, 8, 2 * S5_STATES), F32)],
        compiler_params=_cparams("parallel", "arbitrary"), name="s5",
    )(u_tm, u_tm, h0, wb, wc, a)


def _s5_prep(lam_re, lam_im, log_dt, b_re, b_im, c_re, c_im):
    dt = jnp.exp(log_dt)[..., None]
    mag = jnp.exp(lam_re * dt)
    lb_re = mag * jnp.cos(lam_im * dt)
    lb_im = mag * jnp.sin(lam_im * dt)
    den = lam_re * lam_re + lam_im * lam_im
    nr = lb_re - 1.0
    coef_re = ((nr * lam_re + lb_im * lam_im) / den)[..., None]
    coef_im = ((lb_im * lam_re - nr * lam_im) / den)[..., None]
    bp_re = coef_re * b_re - coef_im * b_im
    bp_im = coef_re * b_im + coef_im * b_re
    eye = jnp.eye(S5_G, dtype=lam_re.dtype)

    def blockdiag_in(b):
        return jnp.einsum('dgnp,gh->dgphn', b, eye).reshape(2, BR, S5_STATES)

    def blockdiag_out(c):
        return jnp.einsum('dgpn,gh->dgnhp', c, eye).reshape(2, S5_STATES, BR)

    wb = jnp.concatenate([blockdiag_in(bp_re), blockdiag_in(bp_im)], axis=-1)
    wc = jnp.concatenate([blockdiag_out(c_re), -blockdiag_out(c_im)], axis=1)
    a = jnp.stack([lb_re.reshape(2, S5_STATES), lb_im.reshape(2, S5_STATES)], axis=1)
    return wb.astype(BF16), wc.astype(BF16), a


def _short_kernel(z_ref, zp_ref, zn_ref, w_ref, b_ref, v_ref, x1_ref, x2_ref, *, nt):
    j = pl.program_id(0) % nt
    z = z_ref[...]
    tm = z.shape[0]
    row = lax.broadcasted_iota(jnp.int32, z.shape, 0)
    prev_row = jnp.where(j > 0, zp_ref[7:8, :], 0.0)
    next_row = jnp.where(j < nt - 1, zn_ref[0:1, :], 0.0)
    zprev = jnp.where(row == 0, prev_row, pltpu.roll(z, 1, 0))
    znext = jnp.where(row == tm - 1, next_row, pltpu.roll(z, tm - 1, 0))
    y = zprev * w_ref[0:1, :] + z * w_ref[1:2, :] + znext * w_ref[2:3, :] + b_ref[...]
    v_ref[...] = y[:, 0:BR]
    x1_ref[...] = y[:, BR:2 * BR]
    x2_ref[...] = y[:, 2 * BR:3 * BR]


def _short_conv(z, w, b, L, tm):
    T = z.shape[0]
    nt = L // tm
    r8 = tm // 8
    nblk8 = T // 8
    out = jax.ShapeDtypeStruct((T, BR), F32)
    return pl.pallas_call(
        functools.partial(_short_kernel, nt=nt),
        grid=(T // tm,),
        in_specs=[pl.BlockSpec((tm, 3 * BR), lambda i: (i, 0)),
                  pl.BlockSpec((8, 3 * BR), lambda i: (jnp.maximum(i * r8 - 1, 0), 0)),
                  pl.BlockSpec((8, 3 * BR), lambda i: (jnp.minimum((i + 1) * r8, nblk8 - 1), 0)),
                  pl.BlockSpec((3, 3 * BR), lambda i: (0, 0)),
                  pl.BlockSpec((1, 3 * BR), lambda i: (0, 0))],
        out_specs=[pl.BlockSpec((tm, BR), lambda i: (i, 0))] * 3,
        out_shape=[out, out, out],
        compiler_params=_cparams("parallel"), name="short_conv",
    )(z, z, z, w, b.reshape(1, 3 * BR))


def _cs(num, den):
    th = (2.0 * math.pi / den) * (num % den).astype(F32)
    return jnp.cos(th), jnp.sin(th)


def _iota(m):
    return jnp.arange(m, dtype=jnp.int32)


def _hy_direct_tables(L):
    n = 2 * L
    c, s = _cs(_iota(n)[:, None] * _iota(L)[None, :], n)
    wf = jnp.concatenate([c, -s], axis=0)
    c, s = _cs((_iota(L)[:, None] + L // 2) * _iota(n)[None, :], n)
    wi = jnp.concatenate([c, -s], axis=1) * (1.0 / n)
    return wf.astype(BF16), wi.astype(BF16)


def _hy_spec_direct_kernel(f_ref, wf_ref, o_ref):
    o_ref[...] = _dot(wf_ref[...], f_ref[...].astype(BF16))


def _hy_spec_direct(filt, wf):
    L, C = filt.shape
    spec = pl.pallas_call(
        _hy_spec_direct_kernel,
        out_shape=jax.ShapeDtypeStruct((4 * L, C), F32), name="hy_spec_direct",
    )(filt, wf)
    return spec.reshape(2, 2 * L, 2, BR).transpose(2, 0, 1, 3)


def _hy_direct_kernel(z_ref, cw_ref, cb_ref, wf_ref, wi_ref, hf_ref, bias_ref, o_ref, *, L, nseq):
    n = 2 * L
    row = lax.broadcasted_iota(jnp.int32, (L, 3 * BR), 0)
    for s in range(nseq):
        z = z_ref[s * L:(s + 1) * L, :]
        zprev = jnp.where(row == 0, 0.0, pltpu.roll(z, 1, 0))
        znext = jnp.where(row == L - 1, 0.0, pltpu.roll(z, L - 1, 0))
        zc = zprev * cw_ref[0:1, :] + z * cw_ref[1:2, :] + znext * cw_ref[2:3, :] + cb_ref[...]
        y = zc[:, 0:BR]
        for o in range(2):
            x = _dot(wf_ref[...], y.astype(BF16))
            xr, xi = x[0:n], x[n:2 * n]
            hr, hi = hf_ref[o, 0], hf_ref[o, 1]
            yc = jnp.concatenate([xr * hr - xi * hi, xr * hi + xi * hr], axis=0).astype(BF16)
            y = zc[:, (o + 1) * BR:(o + 2) * BR] * (_dot(wi_ref[...], yc) + y * bias_ref[o:o + 1, :])
        o_ref[s * L:(s + 1) * L, :] = y


def _hy_direct(z, cw, cb, bias, tabs, hf, L):
    T = z.shape[0]
    wf, wi = tabs
    nseq = 4
    const = lambda shp: pl.BlockSpec(shp, lambda i: (0,) * len(shp))
    return pl.pallas_call(
        functools.partial(_hy_direct_kernel, L=L, nseq=nseq), grid=(T // (nseq * L),),
        in_specs=[pl.BlockSpec((nseq * L, 3 * BR), lambda i: (i, 0)),
                  const((3, 3 * BR)), const((1, 3 * BR)), const((4 * L, L)), const((L, 4 * L)),
                  const((2, 2, 2 * L, BR)), const((2, BR))],
        out_specs=pl.BlockSpec((nseq * L, BR), lambda i: (i, 0)),
        out_shape=jax.ShapeDtypeStruct((T, BR), F32),
        compiler_params=_cparams("parallel"), name="hy_direct",
    )(z, cw, cb.reshape(1, 3 * BR), wf, wi, hf, bias)


HY_N1 = 16
HY_K1 = HY_N1 // 2 + 1
HY_HALVES = 1


def _hy_split_tables(L):
    n = 2 * L
    n2 = n // HY_N1
    hk = n2 // HY_HALVES
    k1 = _iota(HY_K1)[:, None, None]
    k2 = _iota(n2)[None, :, None]
    j2 = _iota(n2)[None, None, :]
    c, s = _cs(j2 * k2 * HY_N1 + j2 * k1, n)
    top = jnp.concatenate([c, s], axis=2).reshape(HY_K1, HY_HALVES, hk, 2 * n2)
    bot = jnp.concatenate([-s, c], axis=2).reshape(HY_K1, HY_HALVES, hk, 2 * n2)
    wf = jnp.concatenate([top, bot], axis=2)
    ct = jnp.swapaxes(c, 1, 2).reshape(HY_K1, n2, HY_HALVES, hk).transpose(0, 2, 1, 3)
    st = jnp.swapaxes(s, 1, 2).reshape(HY_K1, n2, HY_HALVES, hk).transpose(0, 2, 1, 3)
    wi = jnp.concatenate([jnp.concatenate([ct, -st], axis=3),
                          jnp.concatenate([st, ct], axis=3)], axis=2)
    kk = _iota(HY_K1)[:, None]
    c1, s1 = _cs(kk * _iota(HY_N1 // 2)[None, :], HY_N1)
    wgt = jnp.where((kk == 0) | (kk == HY_N1 // 2), 1.0, 2.0) / n
    co, so = _cs(kk * (_iota(HY_N1 // 2)[None, :] + HY_N1 // 4), HY_N1)
    coef = jnp.concatenate([c1, -s1, wgt * co, -wgt * so], axis=1)
    return wf.astype(BF16), wi.astype(BF16), coef.astype(F32)


def _hy_split_stage1(coef_ref, v_ref, a_ref, k1, n2):
    nin = HY_N1 // 2
    ar = ai = None
    for j in range(nin):
        xj = v_ref[0, j * n2:(j + 1) * n2, :]
        tr, ti = coef_ref[k1, j] * xj, coef_ref[k1, nin + j] * xj
        ar, ai = (tr, ti) if ar is None else (ar + tr, ai + ti)
    a_ref[0:n2, :] = ar.astype(BF16)
    a_ref[n2:2 * n2, :] = ai.astype(BF16)


def _hy_spec_split_kernel(coef_ref, f_ref, wf_ref, o_ref, a_ref, *, L):
    n2 = 2 * L // HY_N1

    @pl.when(pl.program_id(2) == 0)
    def _():
        _hy_split_stage1(coef_ref, f_ref, a_ref, pl.program_id(1), n2)

    o_ref[0, 0, 0] = _dot(wf_ref[0, 0], a_ref[...])


def _hy_split_kernel(coef_ref, v_ref, xg_ref, bias_ref, wf_ref, wi_ref, hf_ref, o_ref, a_ref, *, L):
    k1, hh = pl.program_id(1), pl.program_id(2)
    n2 = 2 * L // HY_N1
    hk = n2 // HY_HALVES
    nin = HY_N1 // 2

    @pl.when(hh == 0)
    def _():
        _hy_split_stage1(coef_ref, v_ref, a_ref, k1, n2)

    @pl.when((k1 == 0) & (hh == 0))
    def _():
        o_ref[...] = jnp.zeros_like(o_ref)

    x = _dot(wf_ref[0, 0], a_ref[...])
    xr, xi = x[0:hk], x[hk:2 * hk]
    hr, hi = hf_ref[0, 0, 0:hk, :], hf_ref[0, 0, hk:2 * hk, :]
    yc = jnp.concatenate([xr * hr - xi * hi, xr * hi + xi * hr], axis=0).astype(BF16)
    b = _dot(wi_ref[0, 0], yc)
    for o in range(nin):
        o_ref[0, o * n2:(o + 1) * n2, :] += (coef_ref[k1, 2 * nin + o] * b[0:n2]
                                             + coef_ref[k1, 3 * nin + o] * b[n2:2 * n2])

    @pl.when((k1 == HY_K1 - 1) & (hh == HY_HALVES - 1))
    def _():
        o_ref[0] = xg_ref[0] * (o_ref[0] + v_ref[0] * bias_ref[...])


def _hy_spec_split(filt2, tabs, L):
    wf, _, coef = tabs
    n2 = 2 * L // HY_N1
    hk = n2 // HY_HALVES
    return pl.pallas_call(
        functools.partial(_hy_spec_split_kernel, L=L), grid=(2, HY_K1, HY_HALVES),
        in_specs=[pl.BlockSpec(memory_space=pltpu.SMEM),
                  pl.BlockSpec((1, L, BR), lambda o, k, h: (o, 0, 0)),
                  pl.BlockSpec((1, 1, 2 * hk, 2 * n2), lambda o, k, h: (k, h, 0, 0))],
        out_specs=pl.BlockSpec((1, 1, 1, 2 * hk, BR), lambda o, k, h: (o, k, h, 0, 0)),
        out_shape=jax.ShapeDtypeStruct((2, HY_K1, HY_HALVES, 2 * hk, BR), F32),
        scratch_shapes=[pltpu.VMEM((2 * n2, BR), BF16)],
        compiler_params=_cparams("parallel", "arbitrary", "arbitrary"), name="hy_spec_split",
    )(coef, filt2, wf)


def _hy_split(v, xg, bias, hf, tabs, L):
    wf, wi, coef = tabs
    B = v.shape[0]
    n2 = 2 * L // HY_N1
    hk = n2 // HY_HALVES
    seq = pl.BlockSpec((1, L, BR), lambda b, k, h: (b, 0, 0))
    return pl.pallas_call(
        functools.partial(_hy_split_kernel, L=L), grid=(B, HY_K1, HY_HALVES),
        in_specs=[pl.BlockSpec(memory_space=pltpu.SMEM), seq, seq,
                  pl.BlockSpec((1, BR), lambda b, k, h: (0, 0)),
                  pl.BlockSpec((1, 1, 2 * hk, 2 * n2), lambda b, k, h: (k, h, 0, 0)),
                  pl.BlockSpec((1, 1, 2 * n2, 2 * hk), lambda b, k, h: (k, h, 0, 0)),
                  pl.BlockSpec((1, 1, 2 * hk, BR), lambda b, k, h: (k, h, 0, 0))],
        out_specs=seq,
        out_shape=jax.ShapeDtypeStruct((B, L, BR), F32),
        scratch_shapes=[pltpu.VMEM((2 * n2, BR), BF16)],
        compiler_params=_cparams("parallel", "arbitrary", "arbitrary"), name="hy_split",
    )(coef, v, xg, bias.reshape(1, BR), wf, wi, hf)


def _hy_filters(L, w1, b1, w2, b2, w3):
    dt = w1.dtype
    pos = jnp.arange(L, dtype=dt)
    t01 = jnp.linspace(0.0, 1.0, L, dtype=dt)[:, None]
    w = (2.0 * math.pi / L) * pos[:, None]
    bands = jnp.linspace(1e-4, HY_BANDS - 1, HY_BANDS, dtype=dt)[None, :]
    feats = jnp.concatenate([t01, jnp.cos(bands * w), -jnp.sin(bands * w)], axis=-1)
    h = jnp.sin(feats @ w1 + b1)
    h = jnp.sin(h @ w2 + b2)
    h = h @ w3
    dist = jnp.abs(pos - (L // 2)) / L
    decay = jnp.abs(jnp.linspace(math.log(HY_DECAY_TARGET) / HY_SLOW,
                                 math.log(HY_DECAY_TARGET) / HY_FAST, 2 * BR, dtype=dt))
    return h * jnp.exp(-dist[:, None] * decay[None, :])


def _attn_kernel(lam_ref, q_ref, kt_ref, v_ref, sub_ref, o_ref, *, qscale, post):
    q = (q_ref[0] * qscale).astype(BF16)
    lam = lam_ref[0, 0]
    sub = sub_ref[...]
    for h in range(HEADS):
        vh = v_ref[0, :, h * VAL_DIM:(h + 1) * VAL_DIM]
        res = []
        for m in range(2):
            c0 = h * 2 * HEAD_DIM + m * HEAD_DIM
            s = _dot(q[:, c0:c0 + HEAD_DIM], kt_ref[0, c0:c0 + HEAD_DIM, :])
            p = jnp.exp2(s - jnp.max(s, axis=-1, keepdims=True))
            l = jnp.sum(p, axis=-1, keepdims=True)
            res.append(_dot(p.astype(BF16), vh) / l)
        o = res[0] - lam * res[1]
        o = o * lax.rsqrt(jnp.mean(o * o, axis=-1, keepdims=True) + EPS) * sub * post
        o_ref[0, :, h * VAL_DIM:(h + 1) * VAL_DIM] = o


def _attn(lam, q, kt, v, subln, tq, post):
    B, L, _ = q.shape
    Lk = kt.shape[2]
    return pl.pallas_call(
        functools.partial(_attn_kernel, qscale=HEAD_DIM ** -0.5 * LOG2E, post=post),
        grid=(B, L // tq),
        in_specs=[pl.BlockSpec(memory_space=pltpu.SMEM),
                  pl.BlockSpec((1, tq, BR), lambda b, i: (b, i, 0)),
                  pl.BlockSpec((1, BR, Lk), lambda b, i: (b, 0, 0)),
                  pl.BlockSpec((1, Lk, BR), lambda b, i: (b, 0, 0)),
                  pl.BlockSpec((1, VAL_DIM), lambda b, i: (0, 0))],
        out_specs=pl.BlockSpec((1, tq, BR), lambda b, i: (b, i, 0)),
        out_shape=jax.ShapeDtypeStruct((B, L, BR), F32),
        compiler_params=_cparams("parallel", "parallel"), name="diff_attn",
    )(lam.reshape(1, 1), q, kt, v, subln.reshape(1, VAL_DIM))


def _attn_inputs(k3, v3, ctx):
    B = k3.shape[0]
    if ctx is not None:
        k3 = jnp.concatenate([ctx[0].reshape(B, -1, BR), k3], axis=1)
        v3 = jnp.concatenate([ctx[1].reshape(B, -1, BR), v3], axis=1)
    return jnp.swapaxes(k3, 1, 2).astype(BF16), v3.astype(BF16)


def _attn_latent_test(q, k, v, ck, cv):
    kt, vb = _attn_inputs(k, v, (ck, cv))
    return _attn(jnp.float32(0.5), q, kt, vb, jnp.ones((VAL_DIM,), F32), 256, 0.5)


def _rope_tables(L, dt):
    n_rows = L // GRID_WIDTH
    row = jnp.repeat(jnp.arange(n_rows, dtype=dt), GRID_WIDTH)
    col = (jnp.arange(L) % GRID_WIDTH).astype(dt)
    inv = ROPE_BASE ** (-jnp.arange(0, ROPE_DIM, 2, dtype=dt) / ROPE_DIM)
    ar = row[:, None] * inv[None, :]
    ac = col[:, None] * inv[None, :]
    ang = jnp.concatenate([ar, ar, ac, ac], axis=-1)
    sign = jnp.tile(jnp.concatenate([-jnp.ones(ROPE_DIM // 2, dt), jnp.ones(ROPE_DIM // 2, dt)]), 2)
    reps = BR // HEAD_DIM
    return jnp.tile(jnp.cos(ang), (1, reps)), jnp.tile(jnp.sin(ang) * sign, (1, reps))


def _mixout_kernel(x_ref, mod_ref, u_ref, yf_ref, yb_ref, uv_ref, hy_ref, da_ref,
                   sd_ref, wg_ref, bg_ref, sn_ref, ws_ref, bs_ref, bn_ref, wo_ref, gf_ref, rt_ref,
                   xo_ref, h2_ref, cb_ref, *, route):
    tm = x_ref.shape[0]
    u = u_ref[...]
    y = jax.nn.gelu(sd_ref[...] * u + yf_ref[...] + yb_ref[...])
    y = y * jax.nn.sigmoid(_dot(y.astype(BF16), wg_ref[...]) + bg_ref[...])
    acc = _dot(_rms(y, bn_ref[0:1, :]).astype(BF16), wo_ref[0:BR, :])
    guv = jax.nn.gelu(uv_ref[...])
    gu = guv[:, 0:BR]
    gv = _rms(guv[:, BR:2 * BR], sn_ref[...]).astype(BF16)
    lane = lax.broadcasted_iota(jnp.int32, (SGU_CHUNK, BR), 1)
    hd = BR // SGU_HEADS
    zs = []
    for c in range(tm // SGU_CHUNK):
        vc = gv[c * SGU_CHUNK:(c + 1) * SGU_CHUNK, :]
        z = bs_ref[...]
        for h in range(SGU_HEADS):
            z = z + jnp.where(lane // hd == h, _dot(ws_ref[h], vc), 0.0)
        zs.append(z)
    z = zs[0] if len(zs) == 1 else jnp.concatenate(zs, axis=0)
    acc = acc + _dot(_rms(gu * z, bn_ref[1:2, :]).astype(BF16), wo_ref[BR:2 * BR, :])
    acc = acc + _dot(_rms(hy_ref[...], bn_ref[2:3, :]).astype(BF16), wo_ref[2 * BR:3 * BR, :])
    acc = acc + _dot(da_ref[...].astype(BF16), wo_ref[3 * BR:4 * BR, :])
    xn = x_ref[...] + mod_ref[:, 2 * D:3 * D] * acc
    xo_ref[...] = xn
    h2 = _rms(xn, gf_ref[...]) * (1.0 + mod_ref[:, 4 * D:5 * D]) + mod_ref[:, 3 * D:4 * D]
    h2_ref[...] = h2.astype(BF16)
    if not route:
        cb_ref[...] = jnp.zeros_like(cb_ref)
        return
    logits = _dot3(h2, rt_ref[...])
    el = lax.broadcasted_iota(jnp.int32, logits.shape, 1)
    logits = jnp.where(el < N_EXPERTS, logits, -jnp.inf)
    e = jnp.exp(logits - jnp.max(logits, axis=-1, keepdims=True))
    probs = e / jnp.sum(e, axis=-1, keepdims=True)
    big = logits.shape[1]
    m1 = jnp.max(probs, axis=-1, keepdims=True)
    i1 = jnp.min(jnp.where(probs == m1, el, big), axis=-1, keepdims=True)
    p2 = jnp.where((el == i1) | (el >= N_EXPERTS), -1.0, probs)
    m2 = jnp.max(p2, axis=-1, keepdims=True)
    i2 = jnp.min(jnp.where(p2 == m2, el, big), axis=-1, keepdims=True)
    tot = m1 + m2
    cb_ref[...] = jnp.where(el == i1, m1 / tot, 0.0) + jnp.where(el == i2, m2 / tot, 0.0)


def _mixout(x, modg, u_tm2, yf2, yb2, uv, hy, da, lp, B, L, tm, route):
    T = B * L
    nt = L // tm
    tok = lambda n: pl.BlockSpec((tm, n), lambda i: (i, 0))
    tmaj = pl.BlockSpec((tm, BR), lambda i: (i % nt, i // nt))
    const = lambda shp: pl.BlockSpec(shp, lambda i: (0,) * len(shp))
    return pl.pallas_call(
        functools.partial(_mixout_kernel, route=route), grid=(T // tm,),
        in_specs=[tok(D), _mod_spec(modg, tm, L),
                  tmaj, tmaj, tmaj, tok(2 * BR), tok(BR), tok(BR),
                  const((1, BR)), const((BR, BR)), const((1, BR)), const((1, BR)),
                  const((SGU_HEADS, SGU_CHUNK, SGU_CHUNK)), const((SGU_CHUNK, BR)),
                  const((3, BR)), pl.BlockSpec((None, D, D), lambda i: (lp['l'], 0, 0)),
                  const((1, D)), const((D, 128))],
        out_specs=[tok(D), tok(D), tok(128)],
        out_shape=[jax.ShapeDtypeStruct((T, D), F32), jax.ShapeDtypeStruct((T, D), BF16),
                   jax.ShapeDtypeStruct((T, 128), F32)],
        compiler_params=_cparams("parallel"), name="mix_out",
    )(x, modg, u_tm2, yf2, yb2, uv, hy, da,
      lp['s5_d'], lp['s5_w_glu'], lp['s5_b_glu'], lp['sgu_norm'], lp['sgu_w_s'], lp['sgu_b'],
      lp['branch_norm'], lp['w_out'], lp['norm_ffn'], lp['router'])


def _ffn_kernel(h_ref, x_ref, mod_ref, w1_ref, w3_ref, w2_ref, o_ref, acc_ref):
    j = pl.program_id(1)
    h = h_ref[...]
    a = _dot(h, w1_ref[...])
    t = (a * jax.nn.sigmoid(a) * _dot(h, w3_ref[...])).astype(BF16)
    part = _dot(t, w2_ref[...])

    @pl.when(j == 0)
    def _():
        acc_ref[...] = part

    @pl.when(j > 0)
    def _():
        acc_ref[...] += part

    @pl.when(j == pl.num_programs(1) - 1)
    def _():
        o_ref[...] = x_ref[...] + mod_ref[:, 5 * D:6 * D] * acc_ref[...]


def _ffn(h2, x, modg, w1, w3, w2, jl, L, tm, tf):
    T = x.shape[0]
    dff = w1.shape[2]
    return pl.pallas_call(
        _ffn_kernel, grid=(T // tm, dff // tf),
        in_specs=[pl.BlockSpec((tm, D), lambda i, j: (i, 0)),
                  pl.BlockSpec((tm, D), lambda i, j: (i, 0)),
                  _mod_spec(modg, tm, L),
                  pl.BlockSpec((None, D, tf), lambda i, j: (jl, 0, j)),
                  pl.BlockSpec((None, D, tf), lambda i, j: (jl, 0, j)),
                  pl.BlockSpec((None, tf, D), lambda i, j: (jl, j, 0))],
        out_specs=pl.BlockSpec((tm, D), lambda i, j: (i, 0)),
        out_shape=jax.ShapeDtypeStruct((T, D), F32),
        scratch_shapes=[pltpu.VMEM((tm, D), F32)],
        compiler_params=_cparams("parallel", "arbitrary"), name="ffn",
    )(h2, x, modg, w1, w3, w2)


def _route_kernel(cb_ref, rk_ref, rkt_ref, cnt_ref):
    tm = cb_ref.shape[0]
    mask = cb_ref[...] > 0.0
    mf = jnp.where(mask, 1.0, 0.0)
    r = lax.broadcasted_iota(jnp.int32, (tm, tm), 0)
    c = lax.broadcasted_iota(jnp.int32, (tm, tm), 1)
    before = jnp.where(c < r, 1.0, 0.0).astype(BF16)
    rank = jnp.where(mask, _dot(before, mf.astype(BF16)), -1.0)
    rk_ref[...] = rank
    rkt_ref[...] = rank.T[0:N_EXPERTS, :]
    cnt_ref[...] = jnp.sum(mf, axis=0, keepdims=True)


def _route(cb, tm):
    T = cb.shape[0]
    nt = T // tm
    rk, rkt, cnt = pl.pallas_call(
        _route_kernel, grid=(nt,),
        in_specs=[pl.BlockSpec((tm, 128), lambda i: (i, 0))],
        out_specs=[pl.BlockSpec((tm, 128), lambda i: (i, 0)),
                   pl.BlockSpec((None, N_EXPERTS, tm), lambda i: (i, 0, 0)),
                   pl.BlockSpec((None, 1, 128), lambda i: (i, 0, 0))],
        out_shape=[jax.ShapeDtypeStruct((T, 128), F32),
                   jax.ShapeDtypeStruct((nt, N_EXPERTS, tm), F32),
                   jax.ShapeDtypeStruct((nt, 1, 128), F32)],
        compiler_params=_cparams("parallel"), name="moe_route",
    )(cb)
    return rk, rkt, cnt[:, 0, :N_EXPERTS].astype(jnp.int32).reshape(-1)


MOE_TS = 1024
MOE_CHUNK = 128


def _moe_kernel(cnt_ref, h_ref, x_ref, mod_ref, cb_ref, rk_ref, rkt_ref, w1_ref, w3_ref, w2_ref,
                o_ref, xg_ref, y_ref, *, nsub):
    i, e, j = pl.program_id(0), pl.program_id(1), pl.program_id(2)
    ne, nj = pl.num_programs(1), pl.num_programs(2)
    ts, ch = MOE_TS, MOE_CHUNK

    def rows(c):
        return pl.ds(pl.multiple_of(c * ch, ch), ch)

    @pl.when((e == 0) & (j == 0))
    def _():
        o_ref[...] = jnp.zeros_like(o_ref)
        y_ref[...] = jnp.zeros_like(y_ref)

    for s in range(nsub):
        tok = slice(s * ts, (s + 1) * ts)
        nch = (cnt_ref[(i * nsub + s) * ne + e] + (ch - 1)) // ch

        @pl.when(j == 0)
        def _(s=s, tok=tok, nch=nch):
            rid = lax.broadcasted_iota(jnp.int32, (ch, ts), 0).astype(F32)
            rrow = rkt_ref[s, pl.ds(e, 1), :]

            def gather(c, _):
                sel = jnp.where(rid == rrow - (c * ch).astype(F32), 1.0, 0.0).astype(BF16)
                xg_ref[s, rows(c), :] = _dot(sel, h_ref[tok, :]).astype(BF16)
                return 0

            lax.fori_loop(0, nch, gather, 0)

        def expert(c, _, s=s):
            xg = xg_ref[s, rows(c), :]
            a = _dot(xg, w1_ref[0])
            t = (a * jax.nn.sigmoid(a) * _dot(xg, w3_ref[0])).astype(BF16)
            part = _dot(t, w2_ref[0])

            @pl.when(j == 0)
            def _():
                y_ref[s, rows(c), :] = part

            @pl.when(j > 0)
            def _():
                y_ref[s, rows(c), :] += part

            return 0

        lax.fori_loop(0, nch, expert, 0)

        @pl.when(j == nj - 1)
        def _(s=s, tok=tok, nch=nch):
            el = lax.broadcasted_iota(jnp.int32, (ts, 128), 1)
            rcol = jnp.sum(jnp.where(el == e, rk_ref[tok, :], 0.0), axis=-1, keepdims=True)
            wcol = jnp.sum(jnp.where(el == e, cb_ref[tok, :], 0.0), axis=-1, keepdims=True)
            cid = lax.broadcasted_iota(jnp.int32, (ts, 2 * ch), 1).astype(F32)

            def scatter(c, _):
                r0 = pl.multiple_of(c * (2 * ch), 2 * ch)
                selt = jnp.where(cid == rcol - r0.astype(F32), 1.0, 0.0).astype(BF16)
                o_ref[tok, :] += wcol * _dot(selt, y_ref[s, pl.ds(r0, 2 * ch), :].astype(BF16))
                return 0

            lax.fori_loop(0, (nch + 1) // 2, scatter, 0)

    @pl.when((e == ne - 1) & (j == nj - 1))
    def _():
        o_ref[...] = x_ref[...] + mod_ref[:, 5 * D:6 * D] * o_ref[...]


def _moe(h2, x, modg, cb, w1, w3, w2, jl, L, tm, tf):
    T = x.shape[0]
    _, ne, _, dff = w1.shape
    ts = MOE_TS
    nsub = tm // ts
    rk, rkt, cnt = _route(cb, ts)
    nmod = modg.shape[0]
    rows = -(-ts // (2 * MOE_CHUNK)) * 2 * MOE_CHUNK
    once = pl.Buffered(1)
    grid_spec = pltpu.PrefetchScalarGridSpec(
        num_scalar_prefetch=1, grid=(T // tm, ne, dff // tf),
        in_specs=[pl.BlockSpec((tm, D), lambda i, e, j, c: (i, 0), pipeline_mode=once),
                  pl.BlockSpec((tm, D), lambda i, e, j, c: (i, 0), pipeline_mode=once),
                  pl.BlockSpec((None, 1, 6 * D), lambda i, e, j, c: ((i * tm // L) % nmod, 0, 0)),
                  pl.BlockSpec((tm, 128), lambda i, e, j, c: (i, 0), pipeline_mode=once),
                  pl.BlockSpec((tm, 128), lambda i, e, j, c: (i, 0), pipeline_mode=once),
                  pl.BlockSpec((nsub, N_EXPERTS, ts), lambda i, e, j, c: (i, 0, 0)),
                  pl.BlockSpec((None, 1, D, tf), lambda i, e, j, c: (jl, e, 0, j)),
                  pl.BlockSpec((None, 1, D, tf), lambda i, e, j, c: (jl, e, 0, j)),
                  pl.BlockSpec((None, 1, tf, D), lambda i, e, j, c: (jl, e, j, 0))],
        out_specs=pl.BlockSpec((tm, D), lambda i, e, j, c: (i, 0)),
        scratch_shapes=[pltpu.VMEM((nsub, rows, D), BF16), pltpu.VMEM((nsub, rows, D), F32)])
    return pl.pallas_call(
        functools.partial(_moe_kernel, nsub=nsub), grid_spec=grid_spec,
        out_shape=jax.ShapeDtypeStruct((T, D), F32),
        compiler_params=_cparams("parallel", "arbitrary", "arbitrary"), name="moe",
    )(cnt, h2, x, modg, cb, rk, rkt, w1, w3, w2)


def _final_kernel(x_ref, g_ref, o_ref):
    o_ref[...] = _rms(x_ref[...], g_ref[...])


def _final_norm(x, g, tm):
    T = x.shape[0]
    return pl.pallas_call(
        _final_kernel, grid=(T // tm,),
        in_specs=[pl.BlockSpec((tm, D), lambda i: (i, 0)), pl.BlockSpec((1, D), lambda i: (0, 0))],
        out_specs=pl.BlockSpec((tm, D), lambda i: (i, 0)),
        out_shape=jax.ShapeDtypeStruct((T, D), F32),
        compiler_params=_cparams("parallel"), name="final_norm",
    )(x, g.reshape(1, D))


def _layer(x, modg, lp, l, B, L, ctx):
    is_ctx = ctx is None
    T = B * L
    Bp = -(-B // 8) * 8
    tm = min(L, 512)
    rope_tabs = None if is_ctx else _rope_tables(L, F32)
    u_tm, uv, z, q, k, v = _inproj(x, modg, lp['norm_mix'], lp['w_in'], l, B, L, tm, rope_tabs)

    u3 = jnp.pad(u_tm.reshape(L, B, BR), ((0, 0), (0, Bp - B), (0, 0)))
    if is_ctx:
        h0 = jnp.zeros((2, Bp, 2 * S5_STATES), F32)
    else:
        h0 = ctx[2]
        h0 = jnp.transpose(h0, (1, 0, 4, 2, 3)).reshape(2, B, 2 * S5_STATES)
        h0 = jnp.pad(h0, ((0, 0), (0, Bp - B), (0, 0)))
    yf, yb, hfin = _s5(u3, h0, lp['s5_wb'], lp['s5_wc'], lp['s5_a'], L, Bp, min(L, 256))

    tabs, hf = lp['hy'][L]
    if L <= HY_DIRECT_MAX:
        y_hy = _hy_direct(z, lp['hy_conv_w'], lp['hy_conv_b'], lp['hy_bias'], tabs, hf, L)
    else:
        hv, hx1, hx2 = _short_conv(z, lp['hy_conv_w'], lp['hy_conv_b'], L, tm)
        y1 = _hy_split(hv.reshape(B, L, BR), hx1.reshape(B, L, BR), lp['hy_bias'][0], hf[0], tabs, L)
        y_hy = _hy_split(y1, hx2.reshape(B, L, BR), lp['hy_bias'][1], hf[1], tabs, L).reshape(T, BR)

    q3 = q.reshape(B, L, BR)
    k3 = k.reshape(B, L, BR)
    v3 = v.reshape(B, L, BR)
    kt, vb = _attn_inputs(k3, v3, None if is_ctx else ctx)
    da = _attn(lp['da_lam'], q3, kt, vb, lp['da_subln'], min(L, 256), 1.0 - lp['lam_init'])

    x, h2, cb = _mixout(x, modg, u_tm, yf.reshape(L, Bp * BR), yb.reshape(L, Bp * BR), uv,
                        y_hy, da.reshape(T, BR), lp, B, L, tm, l % 2 == 1)
    tmf = min(T, 1024)
    if l % 2 == 0:
        x = _ffn(h2, x, modg, lp['ffn_w1'], lp['ffn_w3'], lp['ffn_w2'], l // 2, L, tmf,
                 lp['ffn_w1'].shape[2] // 2)
    else:
        tmm = min(T, 2 * MOE_TS)
        if modg.shape[0] > 1:
            tmm = min(tmm, L)
        x = _moe(h2, x, modg, cb, lp['moe_w1'], lp['moe_w3'], lp['moe_w2'], l // 2, L, tmm, 896)
    if is_ctx:
        fin = hfin[:, :B].reshape(2, B, 2, S5_G, S5_N)
        fin = jnp.transpose(fin, (1, 0, 3, 4, 2))
        return x, k3.reshape(B, L, HEADS, 2 * HEAD_DIM), v3.reshape(B, L, HEADS, VAL_DIM), fin
    return x


def kernel(x_prompt, x_sample, cache_k, cache_v, state_ssm, c, c_ctx, w_ada, b_ada, norm_mix, norm_ffn, w_in, w_out, branch_norm, s5_lam_re, s5_lam_im, s5_log_dt, s5_b_re, s5_b_im, s5_c_re, s5_c_im, s5_d, s5_w_glu, s5_b_glu, sgu_norm, sgu_w_s, sgu_b_s, hy_conv_w, hy_conv_b, hy_w1, hy_b1, hy_w2, hy_b2, hy_w3, hy_bias, da_lq1, da_lk1, da_lq2, da_lk2, da_subln, ffn_w1, ffn_w3, ffn_w2, moe_router, moe_w1, moe_w3, moe_w2, norm_final):
    depth = w_in.shape[0]
    Bc, Lc, _ = x_prompt.shape
    Bs, Ls, _ = x_sample.shape

    cond = jnp.concatenate([c_ctx[None, :], c], axis=0)
    cond8 = jnp.pad(cond, ((0, 8 - cond.shape[0]), (0, 0)))
    mod = _ada(cond8, w_ada, b_ada)

    def hyena_tables(L):
        return _hy_direct_tables(L) if L <= HY_DIRECT_MAX else _hy_split_tables(L)

    def hyena_spectrum(L, tabs, *hy_args):
        filt = _hy_filters(L, *hy_args)
        if L <= HY_DIRECT_MAX:
            return _hy_spec_direct(filt, tabs[0])
        return _hy_spec_split(filt.reshape(L, 2, BR).transpose(1, 0, 2), tabs, L)

    hy_tabs = {L: hyena_tables(L) for L in {Lc, Ls}}

    bf = {name: w.astype(BF16) for name, w in dict(
        w_in=w_in, w_out=w_out, ffn_w1=ffn_w1, ffn_w3=ffn_w3, ffn_w2=ffn_w2,
        moe_w1=moe_w1, moe_w3=moe_w3, moe_w2=moe_w2).items()}

    layers = []
    for l in range(depth):
        j = l // 2
        wb, wc, a = _s5_prep(s5_lam_re[l], s5_lam_im[l], s5_log_dt[l], s5_b_re[l], s5_b_im[l],
                             s5_c_re[l], s5_c_im[l])
        lam_init = 0.8 - 0.6 * math.exp(-0.3 * l)
        lam = (jnp.exp(jnp.sum(da_lq1[l] * da_lk1[l])) - jnp.exp(jnp.sum(da_lq2[l] * da_lk2[l]))
               + lam_init)
        hy_args = (hy_w1[l], hy_b1[l], hy_w2[l], hy_b2[l], hy_w3[l])
        lp = dict(
            l=l, norm_mix=norm_mix[l], norm_ffn=norm_ffn[l].reshape(1, D), branch_norm=branch_norm[l],
            s5_wb=wb, s5_wc=wc, s5_a=a, s5_d=s5_d[l].reshape(1, BR),
            s5_w_glu=s5_w_glu[l].astype(BF16), s5_b_glu=s5_b_glu[l].reshape(1, BR),
            sgu_norm=sgu_norm[l].reshape(1, BR), sgu_w_s=sgu_w_s[l].astype(BF16),
            sgu_b=jnp.repeat(sgu_b_s[l].T, BR // SGU_HEADS, axis=1),
            hy_conv_w=hy_conv_w[l], hy_conv_b=hy_conv_b[l], hy_bias=hy_bias[l],
            hy={L: (t, hyena_spectrum(L, t, *hy_args)) for L, t in hy_tabs.items()},
            da_lam=lam, lam_init=lam_init, da_subln=da_subln[l],
        )
        lp.update(bf)
        if l % 2 == 0:
            lp.update(router=jnp.zeros((D, 128), F32))
        else:
            lp.update(router=jnp.pad(moe_router[j], ((0, 0), (0, 128 - N_EXPERTS))))
        layers.append(lp)

    xc = x_prompt.reshape(Bc * Lc, D)
    ks, vs, ss = [], [], []
    for l in range(depth):
        xc, k_l, v_l, s_l = _layer(xc, mod[l, 0:1].reshape(1, 1, 6 * D), layers[l], l, Bc, Lc, None)
        ks.append(k_l)
        vs.append(v_l)
        ss.append(s_l)
    y_prompt = _final_norm(xc, norm_final, 512).reshape(Bc, Lc, D)

    xs = x_sample.reshape(Bs * Ls, D)
    for l in range(depth):
        xs = _layer(xs, mod[l, 1:1 + Bs].reshape(Bs, 1, 6 * D), layers[l], l, Bs, Ls,
                    (cache_k[:, l], cache_v[:, l], state_ssm[:, l]))
    y_sample = _final_norm(xs, norm_final, 512).reshape(Bs, Ls, D)
    return (y_prompt, y_sample, jnp.stack(ks, axis=1), jnp.stack(vs, axis=1), jnp.stack(ss, axis=1))
```

```python
import functools
import math

import jax
import jax.numpy as jnp
from jax import lax
from jax.experimental import pallas as pl
from jax.experimental.pallas import tpu as pltpu

F32 = jnp.float32
BF16 = jnp.bfloat16

D = 1024
BR = 256
PROJ = 9 * BR
S5_G, S5_N, S5_P = 16, 64, 16
S5_STATES = S5_G * S5_N
SGU_CHUNK, SGU_HEADS = 128, 4
HEADS, HEAD_DIM, VAL_DIM = 4, 32, 64
GRID_WIDTH = 64
ROPE_DIM = HEAD_DIM // 2
ROPE_BASE = 10000.0
HY_EMB, HY_BANDS = 33, 16
HY_DECAY_TARGET, HY_FAST, HY_SLOW = 1e-2, 0.3, 1.5
N_EXPERTS = 8
EPS = 1e-6
LOG2E = 1.4426950408889634

HY_DIRECT_MAX = 512
VMEM_LIMIT = 56 * 1024 * 1024


def _cparams(*sem):
    return pltpu.CompilerParams(dimension_semantics=sem, vmem_limit_bytes=VMEM_LIMIT)


def _dot(a, b):
    return jnp.dot(a, b, preferred_element_type=F32)


def _dot3(a, b):
    a_hi = a.astype(BF16)
    b_hi = b.astype(BF16)
    a_lo = (a - a_hi.astype(F32)).astype(BF16)
    b_lo = (b - b_hi.astype(F32)).astype(BF16)
    return _dot(a_hi, b_hi) + (_dot(a_hi, b_lo) + _dot(a_lo, b_hi))


def _rms(x, g):
    return x * lax.rsqrt(jnp.mean(x * x, axis=-1, keepdims=True) + EPS) * g


def _mod_spec(modg, tm, L):
    nmod = modg.shape[0]
    return pl.BlockSpec((None, 1, 6 * D), lambda i, *_: ((i * tm // L) % nmod, 0, 0))


def _ada_kernel(c_ref, w_ref, b_ref, o_ref):
    c = c_ref[...]
    s = (c * jax.nn.sigmoid(c)).astype(BF16)
    o_ref[0] = _dot(s, w_ref[0].astype(BF16)) + b_ref[0]


def _ada(cond8, w_ada, b_ada):
    depth = w_ada.shape[0]
    tn = 1536
    return pl.pallas_call(
        _ada_kernel,
        grid=(depth, 6 * D // tn),
        in_specs=[pl.BlockSpec((8, D), lambda l, j: (0, 0)),
                  pl.BlockSpec((1, D, tn), lambda l, j: (l, 0, j)),
                  pl.BlockSpec((1, 1, tn), lambda l, j: (l, 0, j))],
        out_specs=pl.BlockSpec((1, 8, tn), lambda l, j: (l, 0, j)),
        out_shape=jax.ShapeDtypeStruct((depth, 8, 6 * D), F32),
        compiler_params=_cparams("parallel", "parallel"),
        name="ada",
    )(cond8, w_ada, b_ada.reshape(depth, 1, 6 * D))


def _inproj_kernel(*refs, rope):
    if rope:
        (x_ref, mod_ref, g_ref, w_ref, cos_ref, sin_ref,
         u_ref, uv_ref, z_ref, q_ref, k_ref, v_ref) = refs
    else:
        x_ref, mod_ref, g_ref, w_ref, u_ref, uv_ref, z_ref, q_ref, k_ref, v_ref = refs
    x = x_ref[...]
    h = _rms(x, g_ref[...]) * (1.0 + mod_ref[:, D:2 * D]) + mod_ref[:, 0:D]
    p = _dot(h.astype(BF16), w_ref[...])
    u_ref[...] = p[:, 0:BR]
    uv_ref[...] = p[:, BR:3 * BR]
    z_ref[...] = p[:, 3 * BR:6 * BR]
    q = p[:, 6 * BR:7 * BR]
    k = p[:, 7 * BR:8 * BR]
    if rope:
        cs = cos_ref[...]
        sn = sin_ref[...]
        lane = lax.broadcasted_iota(jnp.int32, q.shape, 1)
        first = (lane % (2 * (ROPE_DIM // 2))) < (ROPE_DIM // 2)
        half = ROPE_DIM // 2

        def rot(t):
            return jnp.where(first, pltpu.roll(t, BR - half, 1), pltpu.roll(t, half, 1))

        q = q * cs + rot(q) * sn
        k = k * cs + rot(k) * sn
    q_ref[...] = q
    k_ref[...] = k
    v_ref[...] = p[:, 8 * BR:9 * BR]


def _inproj(x, modg, g, w_bf, l, B, L, tm, rope_tabs):
    T = B * L
    nt = L // tm
    rope = rope_tabs is not None
    in_specs = [pl.BlockSpec((tm, D), lambda i: (i, 0)),
                _mod_spec(modg, tm, L),
                pl.BlockSpec((1, D), lambda i: (0, 0)),
                pl.BlockSpec((None, D, PROJ), lambda i: (l, 0, 0))]
    args = [x, modg, g.reshape(1, D), w_bf]
    if rope:
        in_specs += [pl.BlockSpec((tm, BR), lambda i: (i % nt, 0))] * 2
        args += list(rope_tabs)
    tok = lambda n: pl.BlockSpec((tm, n), lambda i: (i, 0))
    out_specs = [pl.BlockSpec((tm, BR), lambda i: (i % nt, i // nt)),
                 tok(2 * BR), tok(3 * BR), tok(BR), tok(BR), tok(BR)]
    out_shape = [jax.ShapeDtypeStruct((L, B * BR), F32),
                 jax.ShapeDtypeStruct((T, 2 * BR), F32),
                 jax.ShapeDtypeStruct((T, 3 * BR), F32),
                 jax.ShapeDtypeStruct((T, BR), F32),
                 jax.ShapeDtypeStruct((T, BR), F32),
                 jax.ShapeDtypeStruct((T, BR), F32)]
    return pl.pallas_call(
        functools.partial(_inproj_kernel, rope=rope),
        grid=(T // tm,), in_specs=in_specs, out_specs=out_specs, out_shape=out_shape,
        compiler_params=_cparams("parallel"), name="inproj",
    )(*args)


def _s5_kernel(uf_ref, ub_ref, h0_ref, wb_ref, wc_ref, a_ref, yf_ref, yb_ref, hfin_ref,
               xs_ref, hc_ref, *, tt):
    i = pl.program_id(1)
    last = pl.num_programs(1) - 1
    half = S5_STATES // 2

    @pl.when(i == 0)
    def _():
        hc_ref[...] = h0_ref[...]

    for d in range(2):
        u_ref = uf_ref if d == 0 else ub_ref
        y_ref = yf_ref if d == 0 else yb_ref
        u2 = u_ref[...].reshape(tt * 8, BR).astype(BF16)
        xs_ref[...] = _dot(u2, wb_ref[d])
        for c in range(2):
            cr = slice(c * half, (c + 1) * half)
            ci = slice(S5_STATES + c * half, S5_STATES + (c + 1) * half)
            ar = jnp.broadcast_to(a_ref[d, 0:1, cr], (8, half))
            ai = jnp.broadcast_to(a_ref[d, 1:2, cr], (8, half))

            def body(s, carry, cr=cr, ci=ci, ar=ar, ai=ai, d=d):
                hr, hi = carry
                t = s if d == 0 else tt - 1 - s
                r0 = pl.multiple_of(t * 8, 8)
                nr = ar * hr - ai * hi + xs_ref[pl.ds(r0, 8), cr]
                ni = ar * hi + ai * hr + xs_ref[pl.ds(r0, 8), ci]
                xs_ref[pl.ds(r0, 8), cr] = nr
                xs_ref[pl.ds(r0, 8), ci] = ni
                return nr, ni

            hr, hi = lax.fori_loop(0, tt, body, (hc_ref[d, :, cr], hc_ref[d, :, ci]), unroll=4)
            hc_ref[d, :, cr] = hr
            hc_ref[d, :, ci] = hi
        y = _dot(xs_ref[...].astype(BF16), wc_ref[d])
        y_ref[...] = y.reshape(tt, 8, BR)

    @pl.when(i == last)
    def _():
        hfin_ref[...] = hc_ref[...]


def _s5(u_tm, h0, wb, wc, a, L, Bp, tt):
    nT = L // tt
    ng = Bp // 8
    blk = lambda f: pl.BlockSpec((tt, 8, BR), f)
    const = lambda shp: pl.BlockSpec(shp, lambda g, i: (0,) * len(shp))
    return pl.pallas_call(
        functools.partial(_s5_kernel, tt=tt),
        grid=(ng, nT),
        in_specs=[blk(lambda g, i: (i, g, 0)), blk(lambda g, i: (nT - 1 - i, g, 0)),
                  pl.BlockSpec((2, 8, 2 * S5_STATES), lambda g, i: (0, g, 0)),
                  const((2, BR, 2 * S5_STATES)), const((2, 2 * S5_STATES, BR)),
                  const((2, 2, S5_STATES))],
        out_specs=[blk(lambda g, i: (i, g, 0)), blk(lambda g, i: (nT - 1 - i, g, 0)),
                   pl.BlockSpec((2, 8, 2 * S5_STATES), lambda g, i: (0, g, 0))],
        out_shape=[jax.ShapeDtypeStruct((L, Bp, BR), F32),
                   jax.ShapeDtypeStruct((L, Bp, BR), F32),
                   jax.ShapeDtypeStruct((2, Bp, 2 * S5_STATES), F32)],
        scratch_shapes=[pltpu.VMEM((tt * 8, 2 * S5_STATES), F32),
                        pltpu.VMEM((2, 8, 2 * S5_STATES), F32)],
        compiler_params=_cparams("parallel", "arbitrary"), name="s5",
    )(u_tm, u_tm, h0, wb, wc, a)


def _s5_prep(lam_re, lam_im, log_dt, b_re, b_im, c_re, c_im):
    dt = jnp.exp(log_dt)[..., None]
    mag = jnp.exp(lam_re * dt)
    lb_re = mag * jnp.cos(lam_im * dt)
    lb_im = mag * jnp.sin(lam_im * dt)
    den = lam_re * lam_re + lam_im * lam_im
    nr = lb_re - 1.0
    coef_re = ((nr * lam_re + lb_im * lam_im) / den)[..., None]
    coef_im = ((lb_im * lam_re - nr * lam_im) / den)[..., None]
    bp_re = coef_re * b_re - coef_im * b_im
    bp_im = coef_re * b_im + coef_im * b_re
    eye = jnp.eye(S5_G, dtype=lam_re.dtype)

    def blockdiag_in(b):
        return jnp.einsum('dgnp,gh->dgphn', b, eye).reshape(2, BR, S5_STATES)

    def blockdiag_out(c):
        return jnp.einsum('dgpn,gh->dgnhp', c, eye).reshape(2, S5_STATES, BR)

    wb = jnp.concatenate([blockdiag_in(bp_re), blockdiag_in(bp_im)], axis=-1)
    wc = jnp.concatenate([blockdiag_out(c_re), -blockdiag_out(c_im)], axis=1)
    a = jnp.stack([lb_re.reshape(2, S5_STATES), lb_im.reshape(2, S5_STATES)], axis=1)
    return wb.astype(BF16), wc.astype(BF16), a


def _short_kernel(z_ref, zp_ref, zn_ref, w_ref, b_ref, v_ref, x1_ref, x2_ref, *, nt):
    j = pl.program_id(0) % nt
    z = z_ref[...]
    tm = z.shape[0]
    row = lax.broadcasted_iota(jnp.int32, z.shape, 0)
    prev_row = jnp.where(j > 0, zp_ref[7:8, :], 0.0)
    next_row = jnp.where(j < nt - 1, zn_ref[0:1, :], 0.0)
    zprev = jnp.where(row == 0, prev_row, pltpu.roll(z, 1, 0))
    znext = jnp.where(row == tm - 1, next_row, pltpu.roll(z, tm - 1, 0))
    y = zprev * w_ref[0:1, :] + z * w_ref[1:2, :] + znext * w_ref[2:3, :] + b_ref[...]
    v_ref[...] = y[:, 0:BR]
    x1_ref[...] = y[:, BR:2 * BR]
    x2_ref[...] = y[:, 2 * BR:3 * BR]


def _short_conv(z, w, b, L, tm):
    T = z.shape[0]
    nt = L // tm
    r8 = tm // 8
    nblk8 = T // 8
    out = jax.ShapeDtypeStruct((T, BR), F32)
    return pl.pallas_call(
        functools.partial(_short_kernel, nt=nt),
        grid=(T // tm,),
        in_specs=[pl.BlockSpec((tm, 3 * BR), lambda i: (i, 0)),
                  pl.BlockSpec((8, 3 * BR), lambda i: (jnp.maximum(i * r8 - 1, 0), 0)),
                  pl.BlockSpec((8, 3 * BR), lambda i: (jnp.minimum((i + 1) * r8, nblk8 - 1), 0)),
                  pl.BlockSpec((3, 3 * BR), lambda i: (0, 0)),
                  pl.BlockSpec((1, 3 * BR), lambda i: (0, 0))],
        out_specs=[pl.BlockSpec((tm, BR), lambda i: (i, 0))] * 3,
        out_shape=[out, out, out],
        compiler_params=_cparams("parallel"), name="short_conv",
    )(z, z, z, w, b.reshape(1, 3 * BR))


def _cs(num, den):
    th = (2.0 * math.pi / den) * (num % den).astype(F32)
    return jnp.cos(th), jnp.sin(th)


def _iota(m):
    return jnp.arange(m, dtype=jnp.int32)


def _hy_direct_tables(L):
    n = 2 * L
    c, s = _cs(_iota(n)[:, None] * _iota(L)[None, :], n)
    wf = jnp.concatenate([c, -s], axis=0)
    c, s = _cs((_iota(L)[:, None] + L // 2) * _iota(n)[None, :], n)
    wi = jnp.concatenate([c, -s], axis=1) * (1.0 / n)
    return wf.astype(BF16), wi.astype(BF16)


def _hy_spec_direct_kernel(f_ref, wf_ref, o_ref):
    o_ref[...] = _dot(wf_ref[...], f_ref[...].astype(BF16))


def _hy_spec_direct(filt, wf):
    L, C = filt.shape
    spec = pl.pallas_call(
        _hy_spec_direct_kernel,
        out_shape=jax.ShapeDtypeStruct((4 * L, C), F32), name="hy_spec_direct",
    )(filt, wf)
    return spec.reshape(2, 2 * L, 2, BR).transpose(2, 0, 1, 3)


def _hy_direct_kernel(z_ref, cw_ref, cb_ref, wf_ref, wi_ref, hf_ref, bias_ref, o_ref, *, L, nseq):
    n = 2 * L
    row = lax.broadcasted_iota(jnp.int32, (L, 3 * BR), 0)
    for s in range(nseq):
        z = z_ref[s * L:(s + 1) * L, :]
        zprev = jnp.where(row == 0, 0.0, pltpu.roll(z, 1, 0))
        znext = jnp.where(row == L - 1, 0.0, pltpu.roll(z, L - 1, 0))
        zc = zprev * cw_ref[0:1, :] + z * cw_ref[1:2, :] + znext * cw_ref[2:3, :] + cb_ref[...]
        y = zc[:, 0:BR]
        for o in range(2):
            x = _dot(wf_ref[...], y.astype(BF16))
            xr, xi = x[0:n], x[n:2 * n]
            hr, hi = hf_ref[o, 0], hf_ref[o, 1]
            yc = jnp.concatenate([xr * hr - xi * hi, xr * hi + xi * hr], axis=0).astype(BF16)
            y = zc[:, (o + 1) * BR:(o + 2) * BR] * (_dot(wi_ref[...], yc) + y * bias_ref[o:o + 1, :])
        o_ref[s * L:(s + 1) * L, :] = y


def _hy_direct(z, cw, cb, bias, tabs, hf, L):
    T = z.shape[0]
    wf, wi = tabs
    nseq = 4
    const = lambda shp: pl.BlockSpec(shp, lambda i: (0,) * len(shp))
    return pl.pallas_call(
        functools.partial(_hy_direct_kernel, L=L, nseq=nseq), grid=(T // (nseq * L),),
        in_specs=[pl.BlockSpec((nseq * L, 3 * BR), lambda i: (i, 0)),
                  const((3, 3 * BR)), const((1, 3 * BR)), const((4 * L, L)), const((L, 4 * L)),
                  const((2, 2, 2 * L, BR)), const((2, BR))],
        out_specs=pl.BlockSpec((nseq * L, BR), lambda i: (i, 0)),
        out_shape=jax.ShapeDtypeStruct((T, BR), F32),
        compiler_params=_cparams("parallel"), name="hy_direct",
    )(z, cw, cb.reshape(1, 3 * BR), wf, wi, hf, bias)


HY_N1 = 16
HY_K1 = HY_N1 // 2 + 1
HY_HALVES = 1


def _hy_split_tables(L):
    n = 2 * L
    n2 = n // HY_N1
    hk = n2 // HY_HALVES
    k1 = _iota(HY_K1)[:, None, None]
    k2 = _iota(n2)[None, :, None]
    j2 = _iota(n2)[None, None, :]
    c, s = _cs(j2 * k2 * HY_N1 + j2 * k1, n)
    top = jnp.concatenate([c, s], axis=2).reshape(HY_K1, HY_HALVES, hk, 2 * n2)
    bot = jnp.concatenate([-s, c], axis=2).reshape(HY_K1, HY_HALVES, hk, 2 * n2)
    wf = jnp.concatenate([top, bot], axis=2)
    ct = jnp.swapaxes(c, 1, 2).reshape(HY_K1, n2, HY_HALVES, hk).transpose(0, 2, 1, 3)
    st = jnp.swapaxes(s, 1, 2).reshape(HY_K1, n2, HY_HALVES, hk).transpose(0, 2, 1, 3)
    wi = jnp.concatenate([jnp.concatenate([ct, -st], axis=3),
                          jnp.concatenate([st, ct], axis=3)], axis=2)
    kk = _iota(HY_K1)[:, None]
    c1, s1 = _cs(kk * _iota(HY_N1 // 2)[None, :], HY_N1)
    wgt = jnp.where((kk == 0) | (kk == HY_N1 // 2), 1.0, 2.0) / n
    co, so = _cs(kk * (_iota(HY_N1 // 2)[None, :] + HY_N1 // 4), HY_N1)
    coef = jnp.concatenate([c1, -s1, wgt * co, -wgt * so], axis=1)
    return wf.astype(BF16), wi.astype(BF16), coef.astype(F32)


def _hy_split_stage1(coef_ref, v_ref, a_ref, k1, n2):
    nin = HY_N1 // 2
    ar = ai = None
    for j in range(nin):
        xj = v_ref[0, j * n2:(j + 1) * n2, :]
        tr, ti = coef_ref[k1, j] * xj, coef_ref[k1, nin + j] * xj
        ar, ai = (tr, ti) if ar is None else (ar + tr, ai + ti)
    a_ref[0:n2, :] = ar.astype(BF16)
    a_ref[n2:2 * n2, :] = ai.astype(BF16)


def _hy_spec_split_kernel(coef_ref, f_ref, wf_ref, o_ref, a_ref, *, L):
    n2 = 2 * L // HY_N1

    @pl.when(pl.program_id(2) == 0)
    def _():
        _hy_split_stage1(coef_ref, f_ref, a_ref, pl.program_id(1), n2)

    o_ref[0, 0, 0] = _dot(wf_ref[0, 0], a_ref[...])


def _hy_split_kernel(coef_ref, v_ref, xg_ref, bias_ref, wf_ref, wi_ref, hf_ref, o_ref, a_ref, *, L):
    k1, hh = pl.program_id(1), pl.program_id(2)
    n2 = 2 * L // HY_N1
    hk = n2 // HY_HALVES
    nin = HY_N1 // 2

    @pl.when(hh == 0)
    def _():
        _hy_split_stage1(coef_ref, v_ref, a_ref, k1, n2)

    @pl.when((k1 == 0) & (hh == 0))
    def _():
        o_ref[...] = jnp.zeros_like(o_ref)

    x = _dot(wf_ref[0, 0], a_ref[...])
    xr, xi = x[0:hk], x[hk:2 * hk]
    hr, hi = hf_ref[0, 0, 0:hk, :], hf_ref[0, 0, hk:2 * hk, :]
    yc = jnp.concatenate([xr * hr - xi * hi, xr * hi + xi * hr], axis=0).astype(BF16)
    b = _dot(wi_ref[0, 0], yc)
    for o in range(nin):
        o_ref[0, o * n2:(o + 1) * n2, :] += (coef_ref[k1, 2 * nin + o] * b[0:n2]
                                             + coef_ref[k1, 3 * nin + o] * b[n2:2 * n2])

    @pl.when((k1 == HY_K1 - 1) & (hh == HY_HALVES - 1))
    def _():
        o_ref[0] = xg_ref[0] * (o_ref[0] + v_ref[0] * bias_ref[...])


def _hy_spec_split(filt2, tabs, L):
    wf, _, coef = tabs
    n2 = 2 * L // HY_N1
    hk = n2 // HY_HALVES
    return pl.pallas_call(
        functools.partial(_hy_spec_split_kernel, L=L), grid=(2, HY_K1, HY_HALVES),
        in_specs=[pl.BlockSpec(memory_space=pltpu.SMEM),
                  pl.BlockSpec((1, L, BR), lambda o, k, h: (o, 0, 0)),
                  pl.BlockSpec((1, 1, 2 * hk, 2 * n2), lambda o, k, h: (k, h, 0, 0))],
        out_specs=pl.BlockSpec((1, 1, 1, 2 * hk, BR), lambda o, k, h: (o, k, h, 0, 0)),
        out_shape=jax.ShapeDtypeStruct((2, HY_K1, HY_HALVES, 2 * hk, BR), F32),
        scratch_shapes=[pltpu.VMEM((2 * n2, BR), BF16)],
        compiler_params=_cparams("parallel", "arbitrary", "arbitrary"), name="hy_spec_split",
    )(coef, filt2, wf)


def _hy_split(v, xg, bias, hf, tabs, L):
    wf, wi, coef = tabs
    B = v.shape[0]
    n2 = 2 * L // HY_N1
    hk = n2 // HY_HALVES
    seq = pl.BlockSpec((1, L, BR), lambda b, k, h: (b, 0, 0))
    return pl.pallas_call(
        functools.partial(_hy_split_kernel, L=L), grid=(B, HY_K1, HY_HALVES),
        in_specs=[pl.BlockSpec(memory_space=pltpu.SMEM), seq, seq,
                  pl.BlockSpec((1, BR), lambda b, k, h: (0, 0)),
                  pl.BlockSpec((1, 1, 2 * hk, 2 * n2), lambda b, k, h: (k, h, 0, 0)),
                  pl.BlockSpec((1, 1, 2 * n2, 2 * hk), lambda b, k, h: (k, h, 0, 0)),
                  pl.BlockSpec((1, 1, 2 * hk, BR), lambda b, k, h: (k, h, 0, 0))],
        out_specs=seq,
        out_shape=jax.ShapeDtypeStruct((B, L, BR), F32),
        scratch_shapes=[pltpu.VMEM((2 * n2, BR), BF16)],
        compiler_params=_cparams("parallel", "arbitrary", "arbitrary"), name="hy_split",
    )(coef, v, xg, bias.reshape(1, BR), wf, wi, hf)


def _hy_filters(L, w1, b1, w2, b2, w3):
    dt = w1.dtype
    pos = jnp.arange(L, dtype=dt)
    t01 = jnp.linspace(0.0, 1.0, L, dtype=dt)[:, None]
    w = (2.0 * math.pi / L) * pos[:, None]
    bands = jnp.linspace(1e-4, HY_BANDS - 1, HY_BANDS, dtype=dt)[None, :]
    feats = jnp.concatenate([t01, jnp.cos(bands * w), -jnp.sin(bands * w)], axis=-1)
    h = jnp.sin(feats @ w1 + b1)
    h = jnp.sin(h @ w2 + b2)
    h = h @ w3
    dist = jnp.abs(pos - (L // 2)) / L
    decay = jnp.abs(jnp.linspace(math.log(HY_DECAY_TARGET) / HY_SLOW,
                                 math.log(HY_DECAY_TARGET) / HY_FAST, 2 * BR, dtype=dt))
    return h * jnp.exp(-dist[:, None] * decay[None, :])


def _attn_kernel(lam_ref, q_ref, kt_ref, v_ref, sub_ref, o_ref, *, qscale, post):
    q = (q_ref[0] * qscale).astype(BF16)
    lam = lam_ref[0, 0]
    sub = sub_ref[...]
    for h in range(HEADS):
        vh = v_ref[0, :, h * VAL_DIM:(h + 1) * VAL_DIM]
        res = []
        for m in range(2):
            c0 = h * 2 * HEAD_DIM + m * HEAD_DIM
            s = _dot(q[:, c0:c0 + HEAD_DIM], kt_ref[0, c0:c0 + HEAD_DIM, :])
            p = jnp.exp2(s - jnp.max(s, axis=-1, keepdims=True))
            l = jnp.sum(p, axis=-1, keepdims=True)
            res.append(_dot(p.astype(BF16), vh) / l)
        o = res[0] - lam * res[1]
        o = o * lax.rsqrt(jnp.mean(o * o, axis=-1, keepdims=True) + EPS) * sub * post
        o_ref[0, :, h * VAL_DIM:(h + 1) * VAL_DIM] = o


def _attn(lam, q, kt, v, subln, tq, post):
    B, L, _ = q.shape
    Lk = kt.shape[2]
    return pl.pallas_call(
        functools.partial(_attn_kernel, qscale=HEAD_DIM ** -0.5 * LOG2E, post=post),
        grid=(B, L // tq),
        in_specs=[pl.BlockSpec(memory_space=pltpu.SMEM),
                  pl.BlockSpec((1, tq, BR), lambda b, i: (b, i, 0)),
                  pl.BlockSpec((1, BR, Lk), lambda b, i: (b, 0, 0)),
                  pl.BlockSpec((1, Lk, BR), lambda b, i: (b, 0, 0)),
                  pl.BlockSpec((1, VAL_DIM), lambda b, i: (0, 0))],
        out_specs=pl.BlockSpec((1, tq, BR), lambda b, i: (b, i, 0)),
        out_shape=jax.ShapeDtypeStruct((B, L, BR), F32),
        compiler_params=_cparams("parallel", "parallel"), name="diff_attn",
    )(lam.reshape(1, 1), q, kt, v, subln.reshape(1, VAL_DIM))


def _attn_inputs(k3, v3, ctx):
    B = k3.shape[0]
    if ctx is not None:
        k3 = jnp.concatenate([ctx[0].reshape(B, -1, BR), k3], axis=1)
        v3 = jnp.concatenate([ctx[1].reshape(B, -1, BR), v3], axis=1)
    return jnp.swapaxes(k3, 1, 2).astype(BF16), v3.astype(BF16)


def _rope_tables(L, dt):
    n_rows = L // GRID_WIDTH
    row = jnp.repeat(jnp.arange(n_rows, dtype=dt), GRID_WIDTH)
    col = (jnp.arange(L) % GRID_WIDTH).astype(dt)
    inv = ROPE_BASE ** (-jnp.arange(0, ROPE_DIM, 2, dtype=dt) / ROPE_DIM)
    ar = row[:, None] * inv[None, :]
    ac = col[:, None] * inv[None, :]
    ang = jnp.concatenate([ar, ar, ac, ac], axis=-1)
    sign = jnp.tile(jnp.concatenate([-jnp.ones(ROPE_DIM // 2, dt), jnp.ones(ROPE_DIM // 2, dt)]), 2)
    reps = BR // HEAD_DIM
    return jnp.tile(jnp.cos(ang), (1, reps)), jnp.tile(jnp.sin(ang) * sign, (1, reps))


def _mixout_kernel(x_ref, mod_ref, u_ref, yf_ref, yb_ref, uv_ref, hy_ref, da_ref,
                   sd_ref, wg_ref, bg_ref, sn_ref, ws_ref, bs_ref, bn_ref, wo_ref, gf_ref, rt_ref,
                   xo_ref, h2_ref, cb_ref, *, route):
    tm = x_ref.shape[0]
    u = u_ref[...]
    y = jax.nn.gelu(sd_ref[...] * u + yf_ref[...] + yb_ref[...])
    y = y * jax.nn.sigmoid(_dot(y.astype(BF16), wg_ref[...]) + bg_ref[...])
    acc = _dot(_rms(y, bn_ref[0:1, :]).astype(BF16), wo_ref[0:BR, :])
    guv = jax.nn.gelu(uv_ref[...])
    gu = guv[:, 0:BR]
    gv = _rms(guv[:, BR:2 * BR], sn_ref[...]).astype(BF16)
    lane = lax.broadcasted_iota(jnp.int32, (SGU_CHUNK, BR), 1)
    hd = BR // SGU_HEADS
    zs = []
    for c in range(tm // SGU_CHUNK):
        vc = gv[c * SGU_CHUNK:(c + 1) * SGU_CHUNK, :]
        z = bs_ref[...]
        for h in range(SGU_HEADS):
            z = z + jnp.where(lane // hd == h, _dot(ws_ref[h], vc), 0.0)
        zs.append(z)
    z = zs[0] if len(zs) == 1 else jnp.concatenate(zs, axis=0)
    acc = acc + _dot(_rms(gu * z, bn_ref[1:2, :]).astype(BF16), wo_ref[BR:2 * BR, :])
    acc = acc + _dot(_rms(hy_ref[...], bn_ref[2:3, :]).astype(BF16), wo_ref[2 * BR:3 * BR, :])
    acc = acc + _dot(da_ref[...].astype(BF16), wo_ref[3 * BR:4 * BR, :])
    xn = x_ref[...] + mod_ref[:, 2 * D:3 * D] * acc
    xo_ref[...] = xn
    h2 = _rms(xn, gf_ref[...]) * (1.0 + mod_ref[:, 4 * D:5 * D]) + mod_ref[:, 3 * D:4 * D]
    h2_ref[...] = h2.astype(BF16)
    if not route:
        cb_ref[...] = jnp.zeros_like(cb_ref)
        return
    logits = _dot3(h2, rt_ref[...])
    el = lax.broadcasted_iota(jnp.int32, logits.shape, 1)
    logits = jnp.where(el < N_EXPERTS, logits, -jnp.inf)
    e = jnp.exp(logits - jnp.max(logits, axis=-1, keepdims=True))
    probs = e / jnp.sum(e, axis=-1, keepdims=True)
    big = logits.shape[1]
    m1 = jnp.max(probs, axis=-1, keepdims=True)
    i1 = jnp.min(jnp.where(probs == m1, el, big), axis=-1, keepdims=True)
    p2 = jnp.where((el == i1) | (el >= N_EXPERTS), -1.0, probs)
    m2 = jnp.max(p2, axis=-1, keepdims=True)
    i2 = jnp.min(jnp.where(p2 == m2, el, big), axis=-1, keepdims=True)
    tot = m1 + m2
    cb_ref[...] = jnp.where(el == i1, m1 / tot, 0.0) + jnp.where(el == i2, m2 / tot, 0.0)


def _mixout(x, modg, u_tm2, yf2, yb2, uv, hy, da, lp, B, L, tm, route):
    T = B * L
    nt = L // tm
    tok = lambda n: pl.BlockSpec((tm, n), lambda i: (i, 0))
    tmaj = pl.BlockSpec((tm, BR), lambda i: (i % nt, i // nt))
    const = lambda shp: pl.BlockSpec(shp, lambda i: (0,) * len(shp))
    return pl.pallas_call(
        functools.partial(_mixout_kernel, route=route), grid=(T // tm,),
        in_specs=[tok(D), _mod_spec(modg, tm, L),
                  tmaj, tmaj, tmaj, tok(2 * BR), tok(BR), tok(BR),
                  const((1, BR)), const((BR, BR)), const((1, BR)), const((1, BR)),
                  const((SGU_HEADS, SGU_CHUNK, SGU_CHUNK)), const((SGU_CHUNK, BR)),
                  const((3, BR)), pl.BlockSpec((None, D, D), lambda i: (lp['l'], 0, 0)),
                  const((1, D)), const((D, 128))],
        out_specs=[tok(D), tok(D), tok(128)],
        out_shape=[jax.ShapeDtypeStruct((T, D), F32), jax.ShapeDtypeStruct((T, D), BF16),
                   jax.ShapeDtypeStruct((T, 128), F32)],
        compiler_params=_cparams("parallel"), name="mix_out",
    )(x, modg, u_tm2, yf2, yb2, uv, hy, da,
      lp['s5_d'], lp['s5_w_glu'], lp['s5_b_glu'], lp['sgu_norm'], lp['sgu_w_s'], lp['sgu_b'],
      lp['branch_norm'], lp['w_out'], lp['norm_ffn'], lp['router'])


def _ffn_kernel(h_ref, x_ref, mod_ref, w1_ref, w3_ref, w2_ref, o_ref, acc_ref):
    j = pl.program_id(1)
    h = h_ref[...]
    a = _dot(h, w1_ref[...])
    t = (a * jax.nn.sigmoid(a) * _dot(h, w3_ref[...])).astype(BF16)
    part = _dot(t, w2_ref[...])

    @pl.when(j == 0)
    def _():
        acc_ref[...] = part

    @pl.when(j > 0)
    def _():
        acc_ref[...] += part

    @pl.when(j == pl.num_programs(1) - 1)
    def _():
        o_ref[...] = x_ref[...] + mod_ref[:, 5 * D:6 * D] * acc_ref[...]


def _ffn(h2, x, modg, w1, w3, w2, jl, L, tm, tf):
    T = x.shape[0]
    dff = w1.shape[2]
    return pl.pallas_call(
        _ffn_kernel, grid=(T // tm, dff // tf),
        in_specs=[pl.BlockSpec((tm, D), lambda i, j: (i, 0)),
                  pl.BlockSpec((tm, D), lambda i, j: (i, 0)),
                  _mod_spec(modg, tm, L),
                  pl.BlockSpec((None, D, tf), lambda i, j: (jl, 0, j)),
                  pl.BlockSpec((None, D, tf), lambda i, j: (jl, 0, j)),
                  pl.BlockSpec((None, tf, D), lambda i, j: (jl, j, 0))],
        out_specs=pl.BlockSpec((tm, D), lambda i, j: (i, 0)),
        out_shape=jax.ShapeDtypeStruct((T, D), F32),
        scratch_shapes=[pltpu.VMEM((tm, D), F32)],
        compiler_params=_cparams("parallel", "arbitrary"), name="ffn",
    )(h2, x, modg, w1, w3, w2)


def _route_kernel(cb_ref, rk_ref, rkt_ref, cnt_ref):
    tm = cb_ref.shape[0]
    mask = cb_ref[...] > 0.0
    mf = jnp.where(mask, 1.0, 0.0)
    r = lax.broadcasted_iota(jnp.int32, (tm, tm), 0)
    c = lax.broadcasted_iota(jnp.int32, (tm, tm), 1)
    before = jnp.where(c < r, 1.0, 0.0).astype(BF16)
    rank = jnp.where(mask, _dot(before, mf.astype(BF16)), -1.0)
    rk_ref[...] = rank
    rkt_ref[...] = rank.T[0:N_EXPERTS, :]
    cnt_ref[...] = jnp.sum(mf, axis=0, keepdims=True)


def _route(cb, tm):
    T = cb.shape[0]
    nt = T // tm
    rk, rkt, cnt = pl.pallas_call(
        _route_kernel, grid=(nt,),
        in_specs=[pl.BlockSpec((tm, 128), lambda i: (i, 0))],
        out_specs=[pl.BlockSpec((tm, 128), lambda i: (i, 0)),
                   pl.BlockSpec((None, N_EXPERTS, tm), lambda i: (i, 0, 0)),
                   pl.BlockSpec((None, 1, 128), lambda i: (i, 0, 0))],
        out_shape=[jax.ShapeDtypeStruct((T, 128), F32),
                   jax.ShapeDtypeStruct((nt, N_EXPERTS, tm), F32),
                   jax.ShapeDtypeStruct((nt, 1, 128), F32)],
        compiler_params=_cparams("parallel"), name="moe_route",
    )(cb)
    return rk, rkt, cnt[:, 0, :N_EXPERTS].astype(jnp.int32).reshape(-1)


MOE_TS = 1024
MOE_CHUNK = 128


def _moe_kernel(cnt_ref, h_ref, x_ref, mod_ref, cb_ref, rk_ref, rkt_ref, w1_ref, w3_ref, w2_ref,
                o_ref, xg_ref, y_ref, *, nsub):
    i, e, j = pl.program_id(0), pl.program_id(1), pl.program_id(2)
    ne, nj = pl.num_programs(1), pl.num_programs(2)
    ts, ch = MOE_TS, MOE_CHUNK

    def rows(c):
        return pl.ds(pl.multiple_of(c * ch, ch), ch)

    @pl.when((e == 0) & (j == 0))
    def _():
        o_ref[...] = jnp.zeros_like(o_ref)
        y_ref[...] = jnp.zeros_like(y_ref)

    for s in range(nsub):
        tok = slice(s * ts, (s + 1) * ts)
        nch = (cnt_ref[(i * nsub + s) * ne + e] + (ch - 1)) // ch

        @pl.when(j == 0)
        def _(s=s, tok=tok, nch=nch):
            rid = lax.broadcasted_iota(jnp.int32, (ch, ts), 0).astype(F32)
            rrow = rkt_ref[s, pl.ds(e, 1), :]

            def gather(c, _):
                sel = jnp.where(rid == rrow - (c * ch).astype(F32), 1.0, 0.0).astype(BF16)
                xg_ref[s, rows(c), :] = _dot(sel, h_ref[tok, :]).astype(BF16)
                return 0

            lax.fori_loop(0, nch, gather, 0)

        def expert(c, _, s=s):
            xg = xg_ref[s, rows(c), :]
            a = _dot(xg, w1_ref[0])
            t = (a * jax.nn.sigmoid(a) * _dot(xg, w3_ref[0])).astype(BF16)
            part = _dot(t, w2_ref[0])

            @pl.when(j == 0)
            def _():
                y_ref[s, rows(c), :] = part

            @pl.when(j > 0)
            def _():
                y_ref[s, rows(c), :] += part

            return 0

        lax.fori_loop(0, nch, expert, 0)

        @pl.when(j == nj - 1)
        def _(s=s, tok=tok, nch=nch):
            el = lax.broadcasted_iota(jnp.int32, (ts, 128), 1)
            rcol = jnp.sum(jnp.where(el == e, rk_ref[tok, :], 0.0), axis=-1, keepdims=True)
            wcol = jnp.sum(jnp.where(el == e, cb_ref[tok, :], 0.0), axis=-1, keepdims=True)
            cid = lax.broadcasted_iota(jnp.int32, (ts, 2 * ch), 1).astype(F32)

            def scatter(c, _):
                r0 = pl.multiple_of(c * (2 * ch), 2 * ch)
                selt = jnp.where(cid == rcol - r0.astype(F32), 1.0, 0.0).astype(BF16)
                o_ref[tok, :] += wcol * _dot(selt, y_ref[s, pl.ds(r0, 2 * ch), :].astype(BF16))
                return 0

            lax.fori_loop(0, (nch + 1) // 2, scatter, 0)

    @pl.when((e == ne - 1) & (j == nj - 1))
    def _():
        o_ref[...] = x_ref[...] + mod_ref[:, 5 * D:6 * D] * o_ref[...]


def _moe(h2, x, modg, cb, w1, w3, w2, jl, L, tm, tf):
    T = x.shape[0]
    _, ne, _, dff = w1.shape
    ts = MOE_TS
    nsub = tm // ts
    rk, rkt, cnt = _route(cb, ts)
    nmod = modg.shape[0]
    rows = -(-ts // (2 * MOE_CHUNK)) * 2 * MOE_CHUNK
    grid_spec = pltpu.PrefetchScalarGridSpec(
        num_scalar_prefetch=1, grid=(T // tm, ne, dff // tf),
        in_specs=[pl.BlockSpec((tm, D), lambda i, e, j, c: (i, 0)),
                  pl.BlockSpec((tm, D), lambda i, e, j, c: (i, 0)),
                  pl.BlockSpec((None, 1, 6 * D), lambda i, e, j, c: ((i * tm // L) % nmod, 0, 0)),
                  pl.BlockSpec((tm, 128), lambda i, e, j, c: (i, 0)),
                  pl.BlockSpec((tm, 128), lambda i, e, j, c: (i, 0)),
                  pl.BlockSpec((nsub, N_EXPERTS, ts), lambda i, e, j, c: (i, 0, 0)),
                  pl.BlockSpec((None, 1, D, tf), lambda i, e, j, c: (jl, e, 0, j)),
                  pl.BlockSpec((None, 1, D, tf), lambda i, e, j, c: (jl, e, 0, j)),
                  pl.BlockSpec((None, 1, tf, D), lambda i, e, j, c: (jl, e, j, 0))],
        out_specs=pl.BlockSpec((tm, D), lambda i, e, j, c: (i, 0)),
        scratch_shapes=[pltpu.VMEM((nsub, rows, D), BF16), pltpu.VMEM((nsub, rows, D), F32)])
    return pl.pallas_call(
        functools.partial(_moe_kernel, nsub=nsub), grid_spec=grid_spec,
        out_shape=jax.ShapeDtypeStruct((T, D), F32),
        compiler_params=_cparams("parallel", "arbitrary", "arbitrary"), name="moe",
    )(cnt, h2, x, modg, cb, rk, rkt, w1, w3, w2)


def _final_kernel(x_ref, g_ref, o_ref):
    o_ref[...] = _rms(x_ref[...], g_ref[...])


def _final_norm(x, g, tm):
    T = x.shape[0]
    return pl.pallas_call(
        _final_kernel, grid=(T // tm,),
        in_specs=[pl.BlockSpec((tm, D), lambda i: (i, 0)), pl.BlockSpec((1, D), lambda i: (0, 0))],
        out_specs=pl.BlockSpec((tm, D), lambda i: (i, 0)),
        out_shape=jax.ShapeDtypeStruct((T, D), F32),
        compiler_params=_cparams("parallel"), name="final_norm",
    )(x, g.reshape(1, D))


def _layer(x, modg, lp, l, B, L, ctx):
    is_ctx = ctx is None
    T = B * L
    Bp = -(-B // 8) * 8
    tm = min(L, 512)
    rope_tabs = None if is_ctx else _rope_tables(L, F32)
    u_tm, uv, z, q, k, v = _inproj(x, modg, lp['norm_mix'], lp['w_in'], l, B, L, tm, rope_tabs)

    u3 = jnp.pad(u_tm.reshape(L, B, BR), ((0, 0), (0, Bp - B), (0, 0)))
    if is_ctx:
        h0 = jnp.zeros((2, Bp, 2 * S5_STATES), F32)
    else:
        h0 = ctx[2]
        h0 = jnp.transpose(h0, (1, 0, 4, 2, 3)).reshape(2, B, 2 * S5_STATES)
        h0 = jnp.pad(h0, ((0, 0), (0, Bp - B), (0, 0)))
    yf, yb, hfin = _s5(u3, h0, lp['s5_wb'], lp['s5_wc'], lp['s5_a'], L, Bp, min(L, 256))

    tabs, hf = lp['hy'][L]
    if L <= HY_DIRECT_MAX:
        y_hy = _hy_direct(z, lp['hy_conv_w'], lp['hy_conv_b'], lp['hy_bias'], tabs, hf, L)
    else:
        hv, hx1, hx2 = _short_conv(z, lp['hy_conv_w'], lp['hy_conv_b'], L, tm)
        y1 = _hy_split(hv.reshape(B, L, BR), hx1.reshape(B, L, BR), lp['hy_bias'][0], hf[0], tabs, L)
        y_hy = _hy_split(y1, hx2.reshape(B, L, BR), lp['hy_bias'][1], hf[1], tabs, L).reshape(T, BR)

    q3 = q.reshape(B, L, BR)
    k3 = k.reshape(B, L, BR)
    v3 = v.reshape(B, L, BR)
    kt, vb = _attn_inputs(k3, v3, None if is_ctx else ctx)
    da = _attn(lp['da_lam'], q3, kt, vb, lp['da_subln'], min(L, 256), 1.0 - lp['lam_init'])

    x, h2, cb = _mixout(x, modg, u_tm, yf.reshape(L, Bp * BR), yb.reshape(L, Bp * BR), uv,
                        y_hy, da.reshape(T, BR), lp, B, L, tm, l % 2 == 1)
    tmf = min(T, 1024)
    if l % 2 == 0:
        x = _ffn(h2, x, modg, lp['ffn_w1'], lp['ffn_w3'], lp['ffn_w2'], l // 2, L, tmf,
                 lp['ffn_w1'].shape[2] // 2)
    else:
        tmm = min(T, MOE_TS)
        if modg.shape[0] > 1:
            tmm = min(tmm, L)
        x = _moe(h2, x, modg, cb, lp['moe_w1'], lp['moe_w3'], lp['moe_w2'], l // 2, L, tmm, 1792)
    if is_ctx:
        fin = hfin[:, :B].reshape(2, B, 2, S5_G, S5_N)
        fin = jnp.transpose(fin, (1, 0, 3, 4, 2))
        return x, k3.reshape(B, L, HEADS, 2 * HEAD_DIM), v3.reshape(B, L, HEADS, VAL_DIM), fin
    return x


def kernel(x_prompt, x_sample, cache_k, cache_v, state_ssm, c, c_ctx, w_ada, b_ada, norm_mix, norm_ffn, w_in, w_out, branch_norm, s5_lam_re, s5_lam_im, s5_log_dt, s5_b_re, s5_b_im, s5_c_re, s5_c_im, s5_d, s5_w_glu, s5_b_glu, sgu_norm, sgu_w_s, sgu_b_s, hy_conv_w, hy_conv_b, hy_w1, hy_b1, hy_w2, hy_b2, hy_w3, hy_bias, da_lq1, da_lk1, da_lq2, da_lk2, da_subln, ffn_w1, ffn_w3, ffn_w2, moe_router, moe_w1, moe_w3, moe_w2, norm_final):
    depth = w_in.shape[0]
    Bc, Lc, _ = x_prompt.shape
    Bs, Ls, _ = x_sample.shape

    cond = jnp.concatenate([c_ctx[None, :], c], axis=0)
    cond8 = jnp.pad(cond, ((0, 8 - cond.shape[0]), (0, 0)))
    mod = _ada(cond8, w_ada, b_ada)

    def hyena_tables(L):
        return _hy_direct_tables(L) if L <= HY_DIRECT_MAX else _hy_split_tables(L)

    def hyena_spectrum(L, tabs, *hy_args):
        filt = _hy_filters(L, *hy_args)
        if L <= HY_DIRECT_MAX:
            return _hy_spec_direct(filt, tabs[0])
        return _hy_spec_split(filt.reshape(L, 2, BR).transpose(1, 0, 2), tabs, L)

    hy_tabs = {L: hyena_tables(L) for L in {Lc, Ls}}

    bf = {name: w.astype(BF16) for name, w in dict(
        w_in=w_in, w_out=w_out, ffn_w1=ffn_w1, ffn_w3=ffn_w3, ffn_w2=ffn_w2,
        moe_w1=moe_w1, moe_w3=moe_w3, moe_w2=moe_w2).items()}

    layers = []
    for l in range(depth):
        j = l // 2
        wb, wc, a = _s5_prep(s5_lam_re[l], s5_lam_im[l], s5_log_dt[l], s5_b_re[l], s5_b_im[l],
                             s5_c_re[l], s5_c_im[l])
        lam_init = 0.8 - 0.6 * math.exp(-0.3 * l)
        lam = (jnp.exp(jnp.sum(da_lq1[l] * da_lk1[l])) - jnp.exp(jnp.sum(da_lq2[l] * da_lk2[l]))
               + lam_init)
        hy_args = (hy_w1[l], hy_b1[l], hy_w2[l], hy_b2[l], hy_w3[l])
        lp = dict(
            l=l, norm_mix=norm_mix[l], norm_ffn=norm_ffn[l].reshape(1, D), branch_norm=branch_norm[l],
            s5_wb=wb, s5_wc=wc, s5_a=a, s5_d=s5_d[l].reshape(1, BR),
            s5_w_glu=s5_w_glu[l].astype(BF16), s5_b_glu=s5_b_glu[l].reshape(1, BR),
            sgu_norm=sgu_norm[l].reshape(1, BR), sgu_w_s=sgu_w_s[l].astype(BF16),
            sgu_b=jnp.repeat(sgu_b_s[l].T, BR // SGU_HEADS, axis=1),
            hy_conv_w=hy_conv_w[l], hy_conv_b=hy_conv_b[l], hy_bias=hy_bias[l],
            hy={L: (t, hyena_spectrum(L, t, *hy_args)) for L, t in hy_tabs.items()},
            da_lam=lam, lam_init=lam_init, da_subln=da_subln[l],
        )
        lp.update(bf)
        if l % 2 == 0:
            lp.update(router=jnp.zeros((D, 128), F32))
        else:
            lp.update(router=jnp.pad(moe_router[j], ((0, 0), (0, 128 - N_EXPERTS))))
        layers.append(lp)

    xc = x_prompt.reshape(Bc * Lc, D)
    ks, vs, ss = [], [], []
    for l in range(depth):
        xc, k_l, v_l, s_l = _layer(xc, mod[l, 0:1].reshape(1, 1, 6 * D), layers[l], l, Bc, Lc, None)
        ks.append(k_l)
        vs.append(v_l)
        ss.append(s_l)
    y_prompt = _final_norm(xc, norm_final, 512).reshape(Bc, Lc, D)

    xs = x_sample.reshape(Bs * Ls, D)
    for l in range(depth):
        xs = _layer(xs, mod[l, 1:1 + Bs].reshape(Bs, 1, 6 * D), layers[l], l, Bs, Ls,
                    (cache_k[:, l], cache_v[:, l], state_ssm[:, l]))
    y_sample = _final_norm(xs, norm_final, 512).reshape(Bs, Ls, D)
    return (y_prompt, y_sample, jnp.stack(ks, axis=1), jnp.stack(vs, axis=1), jnp.stack(ss, axis=1))
```

```python
import functools
import math

import jax
import jax.numpy as jnp
from jax import lax
from jax.experimental import pallas as pl
from jax.experimental.pallas import tpu as pltpu

F32 = jnp.float32
BF16 = jnp.bfloat16

D = 1024
BR = 256
PROJ = 9 * BR
S5_G, S5_N, S5_P = 16, 64, 16
S5_STATES = S5_G * S5_N
SGU_CHUNK, SGU_HEADS = 128, 4
HEADS, HEAD_DIM, VAL_DIM = 4, 32, 64
GRID_WIDTH = 64
ROPE_DIM = HEAD_DIM // 2
ROPE_BASE = 10000.0
HY_EMB, HY_BANDS = 33, 16
HY_DECAY_TARGET, HY_FAST, HY_SLOW = 1e-2, 0.3, 1.5
N_EXPERTS = 8
EPS = 1e-6
LOG2E = 1.4426950408889634

HY_DIRECT_MAX = 512
VMEM_LIMIT = 56 * 1024 * 1024


def _cparams(*sem):
    return pltpu.CompilerParams(dimension_semantics=sem, vmem_limit_bytes=VMEM_LIMIT)


def _dot(a, b):
    return jnp.dot(a, b, preferred_element_type=F32)


def _dot3(a, b):
    a_hi = a.astype(BF16)
    b_hi = b.astype(BF16)
    a_lo = (a - a_hi.astype(F32)).astype(BF16)
    b_lo = (b - b_hi.astype(F32)).astype(BF16)
    return _dot(a_hi, b_hi) + (_dot(a_hi, b_lo) + _dot(a_lo, b_hi))


def _rms(x, g):
    return x * lax.rsqrt(jnp.mean(x * x, axis=-1, keepdims=True) + EPS) * g


def _mod_spec(modg, tm, L):
    nmod = modg.shape[0]
    return pl.BlockSpec((None, 1, 6 * D), lambda i, *_: ((i * tm // L) % nmod, 0, 0))


def _ada_kernel(c_ref, w_ref, b_ref, o_ref):
    c = c_ref[...]
    s = (c * jax.nn.sigmoid(c)).astype(BF16)
    o_ref[0] = _dot(s, w_ref[0].astype(BF16)) + b_ref[0]


def _ada(cond8, w_ada, b_ada):
    depth = w_ada.shape[0]
    tn = 1536
    return pl.pallas_call(
        _ada_kernel,
        grid=(depth, 6 * D // tn),
        in_specs=[pl.BlockSpec((8, D), lambda l, j: (0, 0)),
                  pl.BlockSpec((1, D, tn), lambda l, j: (l, 0, j)),
                  pl.BlockSpec((1, 1, tn), lambda l, j: (l, 0, j))],
        out_specs=pl.BlockSpec((1, 8, tn), lambda l, j: (l, 0, j)),
        out_shape=jax.ShapeDtypeStruct((depth, 8, 6 * D), F32),
        compiler_params=_cparams("parallel", "parallel"),
        name="ada",
    )(cond8, w_ada, b_ada.reshape(depth, 1, 6 * D))


def _inproj_kernel(*refs, rope):
    if rope:
        (x_ref, mod_ref, g_ref, w_ref, cos_ref, sin_ref,
         u_ref, uv_ref, z_ref, q_ref, k_ref, v_ref) = refs
    else:
        x_ref, mod_ref, g_ref, w_ref, u_ref, uv_ref, z_ref, q_ref, k_ref, v_ref = refs
    x = x_ref[...]
    h = _rms(x, g_ref[...]) * (1.0 + mod_ref[:, D:2 * D]) + mod_ref[:, 0:D]
    p = _dot(h.astype(BF16), w_ref[...])
    u_ref[...] = p[:, 0:BR]
    uv_ref[...] = p[:, BR:3 * BR]
    z_ref[...] = p[:, 3 * BR:6 * BR]
    q = p[:, 6 * BR:7 * BR]
    k = p[:, 7 * BR:8 * BR]
    if rope:
        cs = cos_ref[...]
        sn = sin_ref[...]
        lane = lax.broadcasted_iota(jnp.int32, q.shape, 1)
        first = (lane % (2 * (ROPE_DIM // 2))) < (ROPE_DIM // 2)
        half = ROPE_DIM // 2

        def rot(t):
            return jnp.where(first, pltpu.roll(t, BR - half, 1), pltpu.roll(t, half, 1))

        q = q * cs + rot(q) * sn
        k = k * cs + rot(k) * sn
    q_ref[...] = q
    k_ref[...] = k
    v_ref[...] = p[:, 8 * BR:9 * BR]


def _inproj(x, modg, g, w_bf, l, B, L, tm, rope_tabs):
    T = B * L
    nt = L // tm
    rope = rope_tabs is not None
    in_specs = [pl.BlockSpec((tm, D), lambda i: (i, 0)),
                _mod_spec(modg, tm, L),
                pl.BlockSpec((1, D), lambda i: (0, 0)),
                pl.BlockSpec((None, D, PROJ), lambda i: (l, 0, 0))]
    args = [x, modg, g.reshape(1, D), w_bf]
    if rope:
        in_specs += [pl.BlockSpec((tm, BR), lambda i: (i % nt, 0))] * 2
        args += list(rope_tabs)
    tok = lambda n: pl.BlockSpec((tm, n), lambda i: (i, 0))
    out_specs = [pl.BlockSpec((tm, BR), lambda i: (i % nt, i // nt)),
                 tok(2 * BR), tok(3 * BR), tok(BR), tok(BR), tok(BR)]
    out_shape = [jax.ShapeDtypeStruct((L, B * BR), F32),
                 jax.ShapeDtypeStruct((T, 2 * BR), F32),
                 jax.ShapeDtypeStruct((T, 3 * BR), F32),
                 jax.ShapeDtypeStruct((T, BR), F32),
                 jax.ShapeDtypeStruct((T, BR), F32),
                 jax.ShapeDtypeStruct((T, BR), F32)]
    return pl.pallas_call(
        functools.partial(_inproj_kernel, rope=rope),
        grid=(T // tm,), in_specs=in_specs, out_specs=out_specs, out_shape=out_shape,
        compiler_params=_cparams("parallel"), name="inproj",
    )(*args)


def _s5_kernel(uf_ref, ub_ref, h0_ref, wb_ref, wc_ref, a_ref, yf_ref, yb_ref, hfin_ref,
               xs_ref, hc_ref, *, tt):
    i = pl.program_id(1)
    last = pl.num_programs(1) - 1
    half = S5_STATES // 2

    @pl.when(i == 0)
    def _():
        hc_ref[...] = h0_ref[...]

    for d in range(2):
        u_ref = uf_ref if d == 0 else ub_ref
        y_ref = yf_ref if d == 0 else yb_ref
        u2 = u_ref[...].reshape(tt * 8, BR).astype(BF16)
        xs_ref[...] = _dot(u2, wb_ref[d])
        for c in range(2):
            cr = slice(c * half, (c + 1) * half)
            ci = slice(S5_STATES + c * half, S5_STATES + (c + 1) * half)
            ar = jnp.broadcast_to(a_ref[d, 0:1, cr], (8, half))
            ai = jnp.broadcast_to(a_ref[d, 1:2, cr], (8, half))

            def body(s, carry, cr=cr, ci=ci, ar=ar, ai=ai, d=d):
                hr, hi = carry
                t = s if d == 0 else tt - 1 - s
                r0 = pl.multiple_of(t * 8, 8)
                nr = ar * hr - ai * hi + xs_ref[pl.ds(r0, 8), cr]
                ni = ar * hi + ai * hr + xs_ref[pl.ds(r0, 8), ci]
                xs_ref[pl.ds(r0, 8), cr] = nr
                xs_ref[pl.ds(r0, 8), ci] = ni
                return nr, ni

            hr, hi = lax.fori_loop(0, tt, body, (hc_ref[d, :, cr], hc_ref[d, :, ci]), unroll=4)
            hc_ref[d, :, cr] = hr
            hc_ref[d, :, ci] = hi
        parts = []
        for ob in range(2):
            hre = xs_ref[:, ob * half:(ob + 1) * half].astype(BF16)
            him = xs_ref[:, S5_STATES + ob * half:S5_STATES + (ob + 1) * half].astype(BF16)
            parts.append(_dot(hre, wc_ref[d, 0, ob]) + _dot(him, wc_ref[d, 1, ob]))
        y_ref[...] = jnp.concatenate(parts, axis=-1).reshape(tt, 8, BR)

    @pl.when(i == last)
    def _():
        hfin_ref[...] = hc_ref[...]


def _s5(u_tm, h0, wb, wc, a, L, Bp, tt):
    nT = L // tt
    ng = Bp // 8
    blk = lambda f: pl.BlockSpec((tt, 8, BR), f)
    const = lambda shp: pl.BlockSpec(shp, lambda g, i: (0,) * len(shp))
    return pl.pallas_call(
        functools.partial(_s5_kernel, tt=tt),
        grid=(ng, nT),
        in_specs=[blk(lambda g, i: (i, g, 0)), blk(lambda g, i: (nT - 1 - i, g, 0)),
                  pl.BlockSpec((2, 8, 2 * S5_STATES), lambda g, i: (0, g, 0)),
                  const((2, BR, 2 * S5_STATES)), const((2, 2, 2, S5_STATES // 2, BR // 2)),
                  const((2, 2, S5_STATES))],
        out_specs=[blk(lambda g, i: (i, g, 0)), blk(lambda g, i: (nT - 1 - i, g, 0)),
                   pl.BlockSpec((2, 8, 2 * S5_STATES), lambda g, i: (0, g, 0))],
        out_shape=[jax.ShapeDtypeStruct((L, Bp, BR), F32),
                   jax.ShapeDtypeStruct((L, Bp, BR), F32),
                   jax.ShapeDtypeStruct((2, Bp, 2 * S5_STATES), F32)],
        scratch_shapes=[pltpu.VMEM((tt * 8, 2 * S5_STATES), F32),
                        pltpu.VMEM((2, 8, 2 * S5_STATES), F32)],
        compiler_params=_cparams("parallel", "arbitrary"), name="s5",
    )(u_tm, u_tm, h0, wb, wc, a)


def _s5_prep(lam_re, lam_im, log_dt, b_re, b_im, c_re, c_im):
    dt = jnp.exp(log_dt)[..., None]
    mag = jnp.exp(lam_re * dt)
    lb_re = mag * jnp.cos(lam_im * dt)
    lb_im = mag * jnp.sin(lam_im * dt)
    den = lam_re * lam_re + lam_im * lam_im
    nr = lb_re - 1.0
    coef_re = ((nr * lam_re + lb_im * lam_im) / den)[..., None]
    coef_im = ((lb_im * lam_re - nr * lam_im) / den)[..., None]
    bp_re = coef_re * b_re - coef_im * b_im
    bp_im = coef_re * b_im + coef_im * b_re
    eye = jnp.eye(S5_G, dtype=lam_re.dtype)

    def blockdiag_in(b):
        return jnp.einsum('dgnp,gh->dgphn', b, eye).reshape(2, BR, S5_STATES)

    def blockdiag_out(c):
        return jnp.einsum('dgpn,gh->dgnhp', c, eye).reshape(2, S5_STATES, BR)

    wb = jnp.concatenate([blockdiag_in(bp_re), blockdiag_in(bp_im)], axis=-1)
    def halves(m):
        hs, hb = S5_STATES // 2, BR // 2
        return jnp.stack([m[:, ob * hs:(ob + 1) * hs, ob * hb:(ob + 1) * hb] for ob in range(2)],
                         axis=1)

    wc = jnp.stack([halves(blockdiag_out(c_re)), -halves(blockdiag_out(c_im))], axis=1)
    a = jnp.stack([lb_re.reshape(2, S5_STATES), lb_im.reshape(2, S5_STATES)], axis=1)
    return wb.astype(BF16), wc.astype(BF16), a


def _short_kernel(z_ref, zp_ref, zn_ref, w_ref, b_ref, v_ref, x1_ref, x2_ref, *, nt):
    j = pl.program_id(0) % nt
    z = z_ref[...]
    tm = z.shape[0]
    row = lax.broadcasted_iota(jnp.int32, z.shape, 0)
    prev_row = jnp.where(j > 0, zp_ref[7:8, :], 0.0)
    next_row = jnp.where(j < nt - 1, zn_ref[0:1, :], 0.0)
    zprev = jnp.where(row == 0, prev_row, pltpu.roll(z, 1, 0))
    znext = jnp.where(row == tm - 1, next_row, pltpu.roll(z, tm - 1, 0))
    y = zprev * w_ref[0:1, :] + z * w_ref[1:2, :] + znext * w_ref[2:3, :] + b_ref[...]
    v_ref[...] = y[:, 0:BR]
    x1_ref[...] = y[:, BR:2 * BR]
    x2_ref[...] = y[:, 2 * BR:3 * BR]


def _short_conv(z, w, b, L, tm):
    T = z.shape[0]
    nt = L // tm
    r8 = tm // 8
    nblk8 = T // 8
    out = jax.ShapeDtypeStruct((T, BR), F32)
    return pl.pallas_call(
        functools.partial(_short_kernel, nt=nt),
        grid=(T // tm,),
        in_specs=[pl.BlockSpec((tm, 3 * BR), lambda i: (i, 0)),
                  pl.BlockSpec((8, 3 * BR), lambda i: (jnp.maximum(i * r8 - 1, 0), 0)),
                  pl.BlockSpec((8, 3 * BR), lambda i: (jnp.minimum((i + 1) * r8, nblk8 - 1), 0)),
                  pl.BlockSpec((3, 3 * BR), lambda i: (0, 0)),
                  pl.BlockSpec((1, 3 * BR), lambda i: (0, 0))],
        out_specs=[pl.BlockSpec((tm, BR), lambda i: (i, 0))] * 3,
        out_shape=[out, out, out],
        compiler_params=_cparams("parallel"), name="short_conv",
    )(z, z, z, w, b.reshape(1, 3 * BR))


def _cs(num, den):
    th = (2.0 * math.pi / den) * (num % den).astype(F32)
    return jnp.cos(th), jnp.sin(th)


def _iota(m):
    return jnp.arange(m, dtype=jnp.int32)


def _hy_direct_tables(L):
    n = 2 * L
    c, s = _cs(_iota(n)[:, None] * _iota(L)[None, :], n)
    wf = jnp.concatenate([c, -s], axis=0)
    c, s = _cs((_iota(L)[:, None] + L // 2) * _iota(n)[None, :], n)
    wi = jnp.concatenate([c, -s], axis=1) * (1.0 / n)
    return wf.astype(BF16), wi.astype(BF16)


def _hy_spec_direct_kernel(f_ref, wf_ref, o_ref):
    o_ref[...] = _dot(wf_ref[...], f_ref[...].astype(BF16))


def _hy_spec_direct(filt, wf):
    L, C = filt.shape
    spec = pl.pallas_call(
        _hy_spec_direct_kernel,
        out_shape=jax.ShapeDtypeStruct((4 * L, C), F32), name="hy_spec_direct",
    )(filt, wf)
    return spec.reshape(2, 2 * L, 2, BR).transpose(2, 0, 1, 3)


def _hy_direct_kernel(z_ref, cw_ref, cb_ref, wf_ref, wi_ref, hf_ref, bias_ref, o_ref, *, L, nseq):
    n = 2 * L
    row = lax.broadcasted_iota(jnp.int32, (L, 3 * BR), 0)
    for s in range(nseq):
        z = z_ref[s * L:(s + 1) * L, :]
        zprev = jnp.where(row == 0, 0.0, pltpu.roll(z, 1, 0))
        znext = jnp.where(row == L - 1, 0.0, pltpu.roll(z, L - 1, 0))
        zc = zprev * cw_ref[0:1, :] + z * cw_ref[1:2, :] + znext * cw_ref[2:3, :] + cb_ref[...]
        y = zc[:, 0:BR]
        for o in range(2):
            x = _dot(wf_ref[...], y.astype(BF16))
            xr, xi = x[0:n], x[n:2 * n]
            hr, hi = hf_ref[o, 0], hf_ref[o, 1]
            yc = jnp.concatenate([xr * hr - xi * hi, xr * hi + xi * hr], axis=0).astype(BF16)
            y = zc[:, (o + 1) * BR:(o + 2) * BR] * (_dot(wi_ref[...], yc) + y * bias_ref[o:o + 1, :])
        o_ref[s * L:(s + 1) * L, :] = y


def _hy_direct(z, cw, cb, bias, tabs, hf, L):
    T = z.shape[0]
    wf, wi = tabs
    nseq = 4
    const = lambda shp: pl.BlockSpec(shp, lambda i: (0,) * len(shp))
    return pl.pallas_call(
        functools.partial(_hy_direct_kernel, L=L, nseq=nseq), grid=(T // (nseq * L),),
        in_specs=[pl.BlockSpec((nseq * L, 3 * BR), lambda i: (i, 0)),
                  const((3, 3 * BR)), const((1, 3 * BR)), const((4 * L, L)), const((L, 4 * L)),
                  const((2, 2, 2 * L, BR)), const((2, BR))],
        out_specs=pl.BlockSpec((nseq * L, BR), lambda i: (i, 0)),
        out_shape=jax.ShapeDtypeStruct((T, BR), F32),
        compiler_params=_cparams("parallel"), name="hy_direct",
    )(z, cw, cb.reshape(1, 3 * BR), wf, wi, hf, bias)


HY_N1 = 16
HY_K1 = HY_N1 // 2 + 1
HY_HALVES = 1


def _hy_split_tables(L):
    n = 2 * L
    n2 = n // HY_N1
    hk = n2 // HY_HALVES
    k1 = _iota(HY_K1)[:, None, None]
    k2 = _iota(n2)[None, :, None]
    j2 = _iota(n2)[None, None, :]
    c, s = _cs(j2 * k2 * HY_N1 + j2 * k1, n)
    top = jnp.concatenate([c, s], axis=2).reshape(HY_K1, HY_HALVES, hk, 2 * n2)
    bot = jnp.concatenate([-s, c], axis=2).reshape(HY_K1, HY_HALVES, hk, 2 * n2)
    wf = jnp.concatenate([top, bot], axis=2)
    ct = jnp.swapaxes(c, 1, 2).reshape(HY_K1, n2, HY_HALVES, hk).transpose(0, 2, 1, 3)
    st = jnp.swapaxes(s, 1, 2).reshape(HY_K1, n2, HY_HALVES, hk).transpose(0, 2, 1, 3)
    wi = jnp.concatenate([jnp.concatenate([ct, -st], axis=3),
                          jnp.concatenate([st, ct], axis=3)], axis=2)
    kk = _iota(HY_K1)[:, None]
    c1, s1 = _cs(kk * _iota(HY_N1 // 2)[None, :], HY_N1)
    wgt = jnp.where((kk == 0) | (kk == HY_N1 // 2), 1.0, 2.0) / n
    co, so = _cs(kk * (_iota(HY_N1 // 2)[None, :] + HY_N1 // 4), HY_N1)
    coef = jnp.concatenate([c1, -s1, wgt * co, -wgt * so], axis=1)
    return wf.astype(BF16), wi.astype(BF16), coef.astype(F32)


def _hy_split_stage1(coef_ref, v_ref, a_ref, k1, n2):
    nin = HY_N1 // 2
    ar = ai = None
    for j in range(nin):
        xj = v_ref[0, j * n2:(j + 1) * n2, :]
        tr, ti = coef_ref[k1, j] * xj, coef_ref[k1, nin + j] * xj
        ar, ai = (tr, ti) if ar is None else (ar + tr, ai + ti)
    a_ref[0:n2, :] = ar.astype(BF16)
    a_ref[n2:2 * n2, :] = ai.astype(BF16)


def _hy_spec_split_kernel(coef_ref, f_ref, wf_ref, o_ref, a_ref, *, L):
    n2 = 2 * L // HY_N1

    @pl.when(pl.program_id(2) == 0)
    def _():
        _hy_split_stage1(coef_ref, f_ref, a_ref, pl.program_id(1), n2)

    o_ref[0, 0, 0] = _dot(wf_ref[0, 0], a_ref[...])


def _hy_split_kernel(coef_ref, v_ref, xg_ref, bias_ref, wf_ref, wi_ref, hf_ref, o_ref, a_ref, *, L):
    k1, hh = pl.program_id(1), pl.program_id(2)
    n2 = 2 * L // HY_N1
    hk = n2 // HY_HALVES
    nin = HY_N1 // 2

    @pl.when(hh == 0)
    def _():
        _hy_split_stage1(coef_ref, v_ref, a_ref, k1, n2)

    @pl.when((k1 == 0) & (hh == 0))
    def _():
        o_ref[...] = jnp.zeros_like(o_ref)

    x = _dot(wf_ref[0, 0], a_ref[...])
    xr, xi = x[0:hk], x[hk:2 * hk]
    hr, hi = hf_ref[0, 0, 0:hk, :], hf_ref[0, 0, hk:2 * hk, :]
    yc = jnp.concatenate([xr * hr - xi * hi, xr * hi + xi * hr], axis=0).astype(BF16)
    b = _dot(wi_ref[0, 0], yc)
    for o in range(nin):
        o_ref[0, o * n2:(o + 1) * n2, :] += (coef_ref[k1, 2 * nin + o] * b[0:n2]
                                             + coef_ref[k1, 3 * nin + o] * b[n2:2 * n2])

    @pl.when((k1 == HY_K1 - 1) & (hh == HY_HALVES - 1))
    def _():
        o_ref[0] = xg_ref[0] * (o_ref[0] + v_ref[0] * bias_ref[...])


def _hy_spec_split(filt2, tabs, L):
    wf, _, coef = tabs
    n2 = 2 * L // HY_N1
    hk = n2 // HY_HALVES
    return pl.pallas_call(
        functools.partial(_hy_spec_split_kernel, L=L), grid=(2, HY_K1, HY_HALVES),
        in_specs=[pl.BlockSpec(memory_space=pltpu.SMEM),
                  pl.BlockSpec((1, L, BR), lambda o, k, h: (o, 0, 0)),
                  pl.BlockSpec((1, 1, 2 * hk, 2 * n2), lambda o, k, h: (k, h, 0, 0))],
        out_specs=pl.BlockSpec((1, 1, 1, 2 * hk, BR), lambda o, k, h: (o, k, h, 0, 0)),
        out_shape=jax.ShapeDtypeStruct((2, HY_K1, HY_HALVES, 2 * hk, BR), F32),
        scratch_shapes=[pltpu.VMEM((2 * n2, BR), BF16)],
        compiler_params=_cparams("parallel", "arbitrary", "arbitrary"), name="hy_spec_split",
    )(coef, filt2, wf)


def _hy_split(v, xg, bias, hf, tabs, L):
    wf, wi, coef = tabs
    B = v.shape[0]
    n2 = 2 * L // HY_N1
    hk = n2 // HY_HALVES
    seq = pl.BlockSpec((1, L, BR), lambda b, k, h: (b, 0, 0))
    return pl.pallas_call(
        functools.partial(_hy_split_kernel, L=L), grid=(B, HY_K1, HY_HALVES),
        in_specs=[pl.BlockSpec(memory_space=pltpu.SMEM), seq, seq,
                  pl.BlockSpec((1, BR), lambda b, k, h: (0, 0)),
                  pl.BlockSpec((1, 1, 2 * hk, 2 * n2), lambda b, k, h: (k, h, 0, 0)),
                  pl.BlockSpec((1, 1, 2 * n2, 2 * hk), lambda b, k, h: (k, h, 0, 0)),
                  pl.BlockSpec((1, 1, 2 * hk, BR), lambda b, k, h: (k, h, 0, 0))],
        out_specs=seq,
        out_shape=jax.ShapeDtypeStruct((B, L, BR), F32),
        scratch_shapes=[pltpu.VMEM((2 * n2, BR), BF16)],
        compiler_params=_cparams("parallel", "arbitrary", "arbitrary"), name="hy_split",
    )(coef, v, xg, bias.reshape(1, BR), wf, wi, hf)


def _hy_filters(L, w1, b1, w2, b2, w3):
    dt = w1.dtype
    pos = jnp.arange(L, dtype=dt)
    t01 = jnp.linspace(0.0, 1.0, L, dtype=dt)[:, None]
    w = (2.0 * math.pi / L) * pos[:, None]
    bands = jnp.linspace(1e-4, HY_BANDS - 1, HY_BANDS, dtype=dt)[None, :]
    feats = jnp.concatenate([t01, jnp.cos(bands * w), -jnp.sin(bands * w)], axis=-1)
    h = jnp.sin(feats @ w1 + b1)
    h = jnp.sin(h @ w2 + b2)
    h = h @ w3
    dist = jnp.abs(pos - (L // 2)) / L
    decay = jnp.abs(jnp.linspace(math.log(HY_DECAY_TARGET) / HY_SLOW,
                                 math.log(HY_DECAY_TARGET) / HY_FAST, 2 * BR, dtype=dt))
    return h * jnp.exp(-dist[:, None] * decay[None, :])


def _attn_kernel(lam_ref, q_ref, kt_ref, v_ref, sub_ref, o_ref, *, qscale, post):
    q = (q_ref[0] * qscale).astype(BF16)
    lam = lam_ref[0, 0]
    sub = sub_ref[...]
    for h in range(HEADS):
        vh = v_ref[0, :, h * VAL_DIM:(h + 1) * VAL_DIM]
        res = []
        for m in range(2):
            c0 = h * 2 * HEAD_DIM + m * HEAD_DIM
            s = _dot(q[:, c0:c0 + HEAD_DIM], kt_ref[0, c0:c0 + HEAD_DIM, :])
            p = jnp.exp2(s - jnp.max(s, axis=-1, keepdims=True))
            l = jnp.sum(p, axis=-1, keepdims=True)
            res.append(_dot(p.astype(BF16), vh) / l)
        o = res[0] - lam * res[1]
        o = o * lax.rsqrt(jnp.mean(o * o, axis=-1, keepdims=True) + EPS) * sub * post
        o_ref[0, :, h * VAL_DIM:(h + 1) * VAL_DIM] = o


def _attn(lam, q, kt, v, subln, tq, post):
    B, L, _ = q.shape
    Lk = kt.shape[2]
    return pl.pallas_call(
        functools.partial(_attn_kernel, qscale=HEAD_DIM ** -0.5 * LOG2E, post=post),
        grid=(B, L // tq),
        in_specs=[pl.BlockSpec(memory_space=pltpu.SMEM),
                  pl.BlockSpec((1, tq, BR), lambda b, i: (b, i, 0)),
                  pl.BlockSpec((1, BR, Lk), lambda b, i: (b, 0, 0)),
                  pl.BlockSpec((1, Lk, BR), lambda b, i: (b, 0, 0)),
                  pl.BlockSpec((1, VAL_DIM), lambda b, i: (0, 0))],
        out_specs=pl.BlockSpec((1, tq, BR), lambda b, i: (b, i, 0)),
        out_shape=jax.ShapeDtypeStruct((B, L, BR), F32),
        compiler_params=_cparams("parallel", "parallel"), name="diff_attn",
    )(lam.reshape(1, 1), q, kt, v, subln.reshape(1, VAL_DIM))


def _attn_inputs(k3, v3, ctx):
    B = k3.shape[0]
    if ctx is not None:
        k3 = jnp.concatenate([ctx[0].reshape(B, -1, BR), k3], axis=1)
        v3 = jnp.concatenate([ctx[1].reshape(B, -1, BR), v3], axis=1)
    return jnp.swapaxes(k3, 1, 2).astype(BF16), v3.astype(BF16)


def _rope_tables(L, dt):
    n_rows = L // GRID_WIDTH
    row = jnp.repeat(jnp.arange(n_rows, dtype=dt), GRID_WIDTH)
    col = (jnp.arange(L) % GRID_WIDTH).astype(dt)
    inv = ROPE_BASE ** (-jnp.arange(0, ROPE_DIM, 2, dtype=dt) / ROPE_DIM)
    ar = row[:, None] * inv[None, :]
    ac = col[:, None] * inv[None, :]
    ang = jnp.concatenate([ar, ar, ac, ac], axis=-1)
    sign = jnp.tile(jnp.concatenate([-jnp.ones(ROPE_DIM // 2, dt), jnp.ones(ROPE_DIM // 2, dt)]), 2)
    reps = BR // HEAD_DIM
    return jnp.tile(jnp.cos(ang), (1, reps)), jnp.tile(jnp.sin(ang) * sign, (1, reps))


def _mixout_kernel(x_ref, mod_ref, u_ref, yf_ref, yb_ref, uv_ref, hy_ref, da_ref,
                   sd_ref, wg_ref, bg_ref, sn_ref, ws_ref, bs_ref, bn_ref, wo_ref, gf_ref, rt_ref,
                   xo_ref, h2_ref, cb_ref, *, route):
    tm = x_ref.shape[0]
    u = u_ref[...]
    y = jax.nn.gelu(sd_ref[...] * u + yf_ref[...] + yb_ref[...])
    y = y * jax.nn.sigmoid(_dot(y.astype(BF16), wg_ref[...]) + bg_ref[...])
    acc = _dot(_rms(y, bn_ref[0:1, :]).astype(BF16), wo_ref[0:BR, :])
    guv = jax.nn.gelu(uv_ref[...])
    gu = guv[:, 0:BR]
    gv = _rms(guv[:, BR:2 * BR], sn_ref[...]).astype(BF16)
    lane = lax.broadcasted_iota(jnp.int32, (SGU_CHUNK, BR), 1)
    hd = BR // SGU_HEADS
    zs = []
    for c in range(tm // SGU_CHUNK):
        vc = gv[c * SGU_CHUNK:(c + 1) * SGU_CHUNK, :]
        z = bs_ref[...]
        for h in range(SGU_HEADS):
            z = z + jnp.where(lane // hd == h, _dot(ws_ref[h], vc), 0.0)
        zs.append(z)
    z = zs[0] if len(zs) == 1 else jnp.concatenate(zs, axis=0)
    acc = acc + _dot(_rms(gu * z, bn_ref[1:2, :]).astype(BF16), wo_ref[BR:2 * BR, :])
    acc = acc + _dot(_rms(hy_ref[...], bn_ref[2:3, :]).astype(BF16), wo_ref[2 * BR:3 * BR, :])
    acc = acc + _dot(da_ref[...].astype(BF16), wo_ref[3 * BR:4 * BR, :])
    xn = x_ref[...] + mod_ref[:, 2 * D:3 * D] * acc
    xo_ref[...] = xn
    h2 = _rms(xn, gf_ref[...]) * (1.0 + mod_ref[:, 4 * D:5 * D]) + mod_ref[:, 3 * D:4 * D]
    h2_ref[...] = h2.astype(BF16)
    if not route:
        cb_ref[...] = jnp.zeros_like(cb_ref)
        return
    logits = _dot3(h2, rt_ref[...])
    el = lax.broadcasted_iota(jnp.int32, logits.shape, 1)
    logits = jnp.where(el < N_EXPERTS, logits, -jnp.inf)
    e = jnp.exp(logits - jnp.max(logits, axis=-1, keepdims=True))
    probs = e / jnp.sum(e, axis=-1, keepdims=True)
    big = logits.shape[1]
    m1 = jnp.max(probs, axis=-1, keepdims=True)
    i1 = jnp.min(jnp.where(probs == m1, el, big), axis=-1, keepdims=True)
    p2 = jnp.where((el == i1) | (el >= N_EXPERTS), -1.0, probs)
    m2 = jnp.max(p2, axis=-1, keepdims=True)
    i2 = jnp.min(jnp.where(p2 == m2, el, big), axis=-1, keepdims=True)
    tot = m1 + m2
    cb_ref[...] = jnp.where(el == i1, m1 / tot, 0.0) + jnp.where(el == i2, m2 / tot, 0.0)


def _mixout(x, modg, u_tm2, yf2, yb2, uv, hy, da, lp, B, L, tm, route):
    T = B * L
    nt = L // tm
    tok = lambda n: pl.BlockSpec((tm, n), lambda i: (i, 0))
    tmaj = pl.BlockSpec((tm, BR), lambda i: (i % nt, i // nt))
    const = lambda shp: pl.BlockSpec(shp, lambda i: (0,) * len(shp))
    return pl.pallas_call(
        functools.partial(_mixout_kernel, route=route), grid=(T // tm,),
        in_specs=[tok(D), _mod_spec(modg, tm, L),
                  tmaj, tmaj, tmaj, tok(2 * BR), tok(BR), tok(BR),
                  const((1, BR)), const((BR, BR)), const((1, BR)), const((1, BR)),
                  const((SGU_HEADS, SGU_CHUNK, SGU_CHUNK)), const((SGU_CHUNK, BR)),
                  const((3, BR)), pl.BlockSpec((None, D, D), lambda i: (lp['l'], 0, 0)),
                  const((1, D)), const((D, 128))],
        out_specs=[tok(D), tok(D), tok(128)],
        out_shape=[jax.ShapeDtypeStruct((T, D), F32), jax.ShapeDtypeStruct((T, D), BF16),
                   jax.ShapeDtypeStruct((T, 128), F32)],
        compiler_params=_cparams("parallel"), name="mix_out",
    )(x, modg, u_tm2, yf2, yb2, uv, hy, da,
      lp['s5_d'], lp['s5_w_glu'], lp['s5_b_glu'], lp['sgu_norm'], lp['sgu_w_s'], lp['sgu_b'],
      lp['branch_norm'], lp['w_out'], lp['norm_ffn'], lp['router'])


def _ffn_kernel(h_ref, x_ref, mod_ref, w1_ref, w3_ref, w2_ref, o_ref, acc_ref):
    j = pl.program_id(1)
    h = h_ref[...]
    a = _dot(h, w1_ref[...])
    t = (a * jax.nn.sigmoid(a) * _dot(h, w3_ref[...])).astype(BF16)
    part = _dot(t, w2_ref[...])

    @pl.when(j == 0)
    def _():
        acc_ref[...] = part

    @pl.when(j > 0)
    def _():
        acc_ref[...] += part

    @pl.when(j == pl.num_programs(1) - 1)
    def _():
        o_ref[...] = x_ref[...] + mod_ref[:, 5 * D:6 * D] * acc_ref[...]


def _ffn(h2, x, modg, w1, w3, w2, jl, L, tm, tf):
    T = x.shape[0]
    dff = w1.shape[2]
    return pl.pallas_call(
        _ffn_kernel, grid=(T // tm, dff // tf),
        in_specs=[pl.BlockSpec((tm, D), lambda i, j: (i, 0)),
                  pl.BlockSpec((tm, D), lambda i, j: (i, 0)),
                  _mod_spec(modg, tm, L),
                  pl.BlockSpec((None, D, tf), lambda i, j: (jl, 0, j)),
                  pl.BlockSpec((None, D, tf), lambda i, j: (jl, 0, j)),
                  pl.BlockSpec((None, tf, D), lambda i, j: (jl, j, 0))],
        out_specs=pl.BlockSpec((tm, D), lambda i, j: (i, 0)),
        out_shape=jax.ShapeDtypeStruct((T, D), F32),
        scratch_shapes=[pltpu.VMEM((tm, D), F32)],
        compiler_params=_cparams("parallel", "arbitrary"), name="ffn",
    )(h2, x, modg, w1, w3, w2)


def _route_kernel(cb_ref, rk_ref, rkt_ref, cnt_ref):
    tm = cb_ref.shape[0]
    mask = cb_ref[...] > 0.0
    mf = jnp.where(mask, 1.0, 0.0)
    r = lax.broadcasted_iota(jnp.int32, (tm, tm), 0)
    c = lax.broadcasted_iota(jnp.int32, (tm, tm), 1)
    before = jnp.where(c < r, 1.0, 0.0).astype(BF16)
    rank = jnp.where(mask, _dot(before, mf.astype(BF16)), -1.0)
    rk_ref[...] = rank
    rkt_ref[...] = rank.T[0:N_EXPERTS, :]
    cnt_ref[...] = jnp.sum(mf, axis=0, keepdims=True)


def _route(cb, tm):
    T = cb.shape[0]
    nt = T // tm
    rk, rkt, cnt = pl.pallas_call(
        _route_kernel, grid=(nt,),
        in_specs=[pl.BlockSpec((tm, 128), lambda i: (i, 0))],
        out_specs=[pl.BlockSpec((tm, 128), lambda i: (i, 0)),
                   pl.BlockSpec((None, N_EXPERTS, tm), lambda i: (i, 0, 0)),
                   pl.BlockSpec((None, 1, 128), lambda i: (i, 0, 0))],
        out_shape=[jax.ShapeDtypeStruct((T, 128), F32),
                   jax.ShapeDtypeStruct((nt, N_EXPERTS, tm), F32),
                   jax.ShapeDtypeStruct((nt, 1, 128), F32)],
        compiler_params=_cparams("parallel"), name="moe_route",
    )(cb)
    return rk, rkt, cnt[:, 0, :N_EXPERTS].astype(jnp.int32).reshape(-1)


MOE_TS = 1024
MOE_CHUNK = 128


def _moe_kernel(cnt_ref, h_ref, x_ref, mod_ref, cb_ref, rk_ref, rkt_ref, w1_ref, w3_ref, w2_ref,
                o_ref, xg_ref, y_ref, *, nsub):
    i, e, j = pl.program_id(0), pl.program_id(1), pl.program_id(2)
    ne, nj = pl.num_programs(1), pl.num_programs(2)
    ts, ch = MOE_TS, MOE_CHUNK

    def rows(c):
        return pl.ds(pl.multiple_of(c * ch, ch), ch)

    @pl.when((e == 0) & (j == 0))
    def _():
        o_ref[...] = jnp.zeros_like(o_ref)
        y_ref[...] = jnp.zeros_like(y_ref)

    for s in range(nsub):
        tok = slice(s * ts, (s + 1) * ts)
        nch = (cnt_ref[(i * nsub + s) * ne + e] + (ch - 1)) // ch

        @pl.when(j == 0)
        def _(s=s, tok=tok, nch=nch):
            rid = lax.broadcasted_iota(jnp.int32, (ch, ts), 0).astype(F32)
            rrow = rkt_ref[s, pl.ds(e, 1), :]

            def gather(c, _):
                sel = jnp.where(rid == rrow - (c * ch).astype(F32), 1.0, 0.0).astype(BF16)
                xg_ref[s, rows(c), :] = _dot(sel, h_ref[tok, :]).astype(BF16)
                return 0

            lax.fori_loop(0, nch, gather, 0)

        def expert(c, _, s=s):
            xg = xg_ref[s, rows(c), :]
            a = _dot(xg, w1_ref[0])
            t = (a * jax.nn.sigmoid(a) * _dot(xg, w3_ref[0])).astype(BF16)
            part = _dot(t, w2_ref[0])

            @pl.when(j == 0)
            def _():
                y_ref[s, rows(c), :] = part

            @pl.when(j > 0)
            def _():
                y_ref[s, rows(c), :] += part

            return 0

        lax.fori_loop(0, nch, expert, 0)

        @pl.when(j == nj - 1)
        def _(s=s, tok=tok, nch=nch):
            el = lax.broadcasted_iota(jnp.int32, (ts, 128), 1)
            rcol = jnp.sum(jnp.where(el == e, rk_ref[tok, :], 0.0), axis=-1, keepdims=True)
            wcol = jnp.sum(jnp.where(el == e, cb_ref[tok, :], 0.0), axis=-1, keepdims=True)
            cid = lax.broadcasted_iota(jnp.int32, (ts, 2 * ch), 1).astype(F32)

            def scatter(c, _):
                r0 = pl.multiple_of(c * (2 * ch), 2 * ch)
                selt = jnp.where(cid == rcol - r0.astype(F32), 1.0, 0.0).astype(BF16)
                o_ref[tok, :] += wcol * _dot(selt, y_ref[s, pl.ds(r0, 2 * ch), :].astype(BF16))
                return 0

            lax.fori_loop(0, (nch + 1) // 2, scatter, 0)

    @pl.when((e == ne - 1) & (j == nj - 1))
    def _():
        o_ref[...] = x_ref[...] + mod_ref[:, 5 * D:6 * D] * o_ref[...]


def _moe(h2, x, modg, cb, w1, w3, w2, jl, L, tm, tf):
    T = x.shape[0]
    _, ne, _, dff = w1.shape
    ts = MOE_TS
    nsub = tm // ts
    rk, rkt, cnt = _route(cb, ts)
    nmod = modg.shape[0]
    rows = -(-ts // (2 * MOE_CHUNK)) * 2 * MOE_CHUNK
    grid_spec = pltpu.PrefetchScalarGridSpec(
        num_scalar_prefetch=1, grid=(T // tm, ne, dff // tf),
        in_specs=[pl.BlockSpec((tm, D), lambda i, e, j, c: (i, 0)),
                  pl.BlockSpec((tm, D), lambda i, e, j, c: (i, 0)),
                  pl.BlockSpec((None, 1, 6 * D), lambda i, e, j, c: ((i * tm // L) % nmod, 0, 0)),
                  pl.BlockSpec((tm, 128), lambda i, e, j, c: (i, 0)),
                  pl.BlockSpec((tm, 128), lambda i, e, j, c: (i, 0)),
                  pl.BlockSpec((nsub, N_EXPERTS, ts), lambda i, e, j, c: (i, 0, 0)),
                  pl.BlockSpec((None, 1, D, tf), lambda i, e, j, c: (jl, e, 0, j)),
                  pl.BlockSpec((None, 1, D, tf), lambda i, e, j, c: (jl, e, 0, j)),
                  pl.BlockSpec((None, 1, tf, D), lambda i, e, j, c: (jl, e, j, 0))],
        out_specs=pl.BlockSpec((tm, D), lambda i, e, j, c: (i, 0)),
        scratch_shapes=[pltpu.VMEM((nsub, rows, D), BF16), pltpu.VMEM((nsub, rows, D), F32)])
    return pl.pallas_call(
        functools.partial(_moe_kernel, nsub=nsub), grid_spec=grid_spec,
        out_shape=jax.ShapeDtypeStruct((T, D), F32),
        compiler_params=_cparams("parallel", "arbitrary", "arbitrary"), name="moe",
    )(cnt, h2, x, modg, cb, rk, rkt, w1, w3, w2)


def _final_kernel(x_ref, g_ref, o_ref):
    o_ref[...] = _rms(x_ref[...], g_ref[...])


def _final_norm(x, g, tm):
    T = x.shape[0]
    return pl.pallas_call(
        _final_kernel, grid=(T // tm,),
        in_specs=[pl.BlockSpec((tm, D), lambda i: (i, 0)), pl.BlockSpec((1, D), lambda i: (0, 0))],
        out_specs=pl.BlockSpec((tm, D), lambda i: (i, 0)),
        out_shape=jax.ShapeDtypeStruct((T, D), F32),
        compiler_params=_cparams("parallel"), name="final_norm",
    )(x, g.reshape(1, D))


def _layer(x, modg, lp, l, B, L, ctx):
    is_ctx = ctx is None
    T = B * L
    Bp = -(-B // 8) * 8
    tm = min(L, 512)
    rope_tabs = None if is_ctx else _rope_tables(L, F32)
    u_tm, uv, z, q, k, v = _inproj(x, modg, lp['norm_mix'], lp['w_in'], l, B, L, tm, rope_tabs)

    u3 = jnp.pad(u_tm.reshape(L, B, BR), ((0, 0), (0, Bp - B), (0, 0)))
    if is_ctx:
        h0 = jnp.zeros((2, Bp, 2 * S5_STATES), F32)
    else:
        h0 = ctx[2]
        h0 = jnp.transpose(h0, (1, 0, 4, 2, 3)).reshape(2, B, 2 * S5_STATES)
        h0 = jnp.pad(h0, ((0, 0), (0, Bp - B), (0, 0)))
    yf, yb, hfin = _s5(u3, h0, lp['s5_wb'], lp['s5_wc'], lp['s5_a'], L, Bp, min(L, 256))

    tabs, hf = lp['hy'][L]
    if L <= HY_DIRECT_MAX:
        y_hy = _hy_direct(z, lp['hy_conv_w'], lp['hy_conv_b'], lp['hy_bias'], tabs, hf, L)
    else:
        hv, hx1, hx2 = _short_conv(z, lp['hy_conv_w'], lp['hy_conv_b'], L, tm)
        y1 = _hy_split(hv.reshape(B, L, BR), hx1.reshape(B, L, BR), lp['hy_bias'][0], hf[0], tabs, L)
        y_hy = _hy_split(y1, hx2.reshape(B, L, BR), lp['hy_bias'][1], hf[1], tabs, L).reshape(T, BR)

    q3 = q.reshape(B, L, BR)
    k3 = k.reshape(B, L, BR)
    v3 = v.reshape(B, L, BR)
    kt, vb = _attn_inputs(k3, v3, None if is_ctx else ctx)
    da = _attn(lp['da_lam'], q3, kt, vb, lp['da_subln'], min(L, 256), 1.0 - lp['lam_init'])

    x, h2, cb = _mixout(x, modg, u_tm, yf.reshape(L, Bp * BR), yb.reshape(L, Bp * BR), uv,
                        y_hy, da.reshape(T, BR), lp, B, L, tm, l % 2 == 1)
    tmf = min(T, 1024)
    if l % 2 == 0:
        x = _ffn(h2, x, modg, lp['ffn_w1'], lp['ffn_w3'], lp['ffn_w2'], l // 2, L, tmf,
                 lp['ffn_w1'].shape[2] // 2)
    else:
        tmm = min(T, MOE_TS)
        if modg.shape[0] > 1:
            tmm = min(tmm, L)
        x = _moe(h2, x, modg, cb, lp['moe_w1'], lp['moe_w3'], lp['moe_w2'], l // 2, L, tmm, 1792)
    if is_ctx:
        fin = hfin[:, :B].reshape(2, B, 2, S5_G, S5_N)
        fin = jnp.transpose(fin, (1, 0, 3, 4, 2))
        return x, k3.reshape(B, L, HEADS, 2 * HEAD_DIM), v3.reshape(B, L, HEADS, VAL_DIM), fin
    return x


def kernel(x_prompt, x_sample, cache_k, cache_v, state_ssm, c, c_ctx, w_ada, b_ada, norm_mix, norm_ffn, w_in, w_out, branch_norm, s5_lam_re, s5_lam_im, s5_log_dt, s5_b_re, s5_b_im, s5_c_re, s5_c_im, s5_d, s5_w_glu, s5_b_glu, sgu_norm, sgu_w_s, sgu_b_s, hy_conv_w, hy_conv_b, hy_w1, hy_b1, hy_w2, hy_b2, hy_w3, hy_bias, da_lq1, da_lk1, da_lq2, da_lk2, da_subln, ffn_w1, ffn_w3, ffn_w2, moe_router, moe_w1, moe_w3, moe_w2, norm_final):
    depth = w_in.shape[0]
    Bc, Lc, _ = x_prompt.shape
    Bs, Ls, _ = x_sample.shape

    cond = jnp.concatenate([c_ctx[None, :], c], axis=0)
    cond8 = jnp.pad(cond, ((0, 8 - cond.shape[0]), (0, 0)))
    mod = _ada(cond8, w_ada, b_ada)

    def hyena_tables(L):
        return _hy_direct_tables(L) if L <= HY_DIRECT_MAX else _hy_split_tables(L)

    def hyena_spectrum(L, tabs, *hy_args):
        filt = _hy_filters(L, *hy_args)
        if L <= HY_DIRECT_MAX:
            return _hy_spec_direct(filt, tabs[0])
        return _hy_spec_split(filt.reshape(L, 2, BR).transpose(1, 0, 2), tabs, L)

    hy_tabs = {L: hyena_tables(L) for L in {Lc, Ls}}

    bf = {name: w.astype(BF16) for name, w in dict(
        w_in=w_in, w_out=w_out, ffn_w1=ffn_w1, ffn_w3=ffn_w3, ffn_w2=ffn_w2,
        moe_w1=moe_w1, moe_w3=moe_w3, moe_w2=moe_w2).items()}

    layers = []
    for l in range(depth):
        j = l // 2
        wb, wc, a = _s5_prep(s5_lam_re[l], s5_lam_im[l], s5_log_dt[l], s5_b_re[l], s5_b_im[l],
                             s5_c_re[l], s5_c_im[l])
        lam_init = 0.8 - 0.6 * math.exp(-0.3 * l)
        lam = (jnp.exp(jnp.sum(da_lq1[l] * da_lk1[l])) - jnp.exp(jnp.sum(da_lq2[l] * da_lk2[l]))
               + lam_init)
        hy_args = (hy_w1[l], hy_b1[l], hy_w2[l], hy_b2[l], hy_w3[l])
        lp = dict(
            l=l, norm_mix=norm_mix[l], norm_ffn=norm_ffn[l].reshape(1, D), branch_norm=branch_norm[l],
            s5_wb=wb, s5_wc=wc, s5_a=a, s5_d=s5_d[l].reshape(1, BR),
            s5_w_glu=s5_w_glu[l].astype(BF16), s5_b_glu=s5_b_glu[l].reshape(1, BR),
            sgu_norm=sgu_norm[l].reshape(1, BR), sgu_w_s=sgu_w_s[l].astype(BF16),
            sgu_b=jnp.repeat(sgu_b_s[l].T, BR // SGU_HEADS, axis=1),
            hy_conv_w=hy_conv_w[l], hy_conv_b=hy_conv_b[l], hy_bias=hy_bias[l],
            hy={L: (t, hyena_spectrum(L, t, *hy_args)) for L, t in hy_tabs.items()},
            da_lam=lam, lam_init=lam_init, da_subln=da_subln[l],
        )
        lp.update(bf)
        if l % 2 == 0:
            lp.update(router=jnp.zeros((D, 128), F32))
        else:
            lp.update(router=jnp.pad(moe_router[j], ((0, 0), (0, 128 - N_EXPERTS))))
        layers.append(lp)

    xc = x_prompt.reshape(Bc * Lc, D)
    ks, vs, ss = [], [], []
    for l in range(depth):
        xc, k_l, v_l, s_l = _layer(xc, mod[l, 0:1].reshape(1, 1, 6 * D), layers[l], l, Bc, Lc, None)
        ks.append(k_l)
        vs.append(v_l)
        ss.append(s_l)
    y_prompt = _final_norm(xc, norm_final, 512).reshape(Bc, Lc, D)

    xs = x_sample.reshape(Bs * Ls, D)
    for l in range(depth):
        xs = _layer(xs, mod[l, 1:1 + Bs].reshape(Bs, 1, 6 * D), layers[l], l, Bs, Ls,
                    (cache_k[:, l], cache_v[:, l], state_ssm[:, l]))
    y_sample = _final_norm(xs, norm_final, 512).reshape(Bs, Ls, D)
    return (y_prompt, y_sample, jnp.stack(ks, axis=1), jnp.stack(vs, axis=1), jnp.stack(ss, axis=1))
```

```python
import functools
import math

import jax
import jax.numpy as jnp
from jax import lax
from jax.experimental import pallas as pl
from jax.experimental.pallas import tpu as pltpu

F32 = jnp.float32
BF16 = jnp.bfloat16

D = 1024
BR = 256
PROJ = 9 * BR
S5_G, S5_N, S5_P = 16, 64, 16
S5_STATES = S5_G * S5_N
SGU_CHUNK, SGU_HEADS = 128, 4
HEADS, HEAD_DIM, VAL_DIM = 4, 32, 64
GRID_WIDTH = 64
ROPE_DIM = HEAD_DIM // 2
ROPE_BASE = 10000.0
HY_EMB, HY_BANDS = 33, 16
HY_DECAY_TARGET, HY_FAST, HY_SLOW = 1e-2, 0.3, 1.5
N_EXPERTS = 8
EPS = 1e-6
LOG2E = 1.4426950408889634

HY_DIRECT_MAX = 512
VMEM_LIMIT = 56 * 1024 * 1024


def _cparams(*sem):
    return pltpu.CompilerParams(dimension_semantics=sem, vmem_limit_bytes=VMEM_LIMIT)


def _dot(a, b):
    return jnp.dot(a, b, preferred_element_type=F32)


def _dot3(a, b):
    a_hi = a.astype(BF16)
    b_hi = b.astype(BF16)
    a_lo = (a - a_hi.astype(F32)).astype(BF16)
    b_lo = (b - b_hi.astype(F32)).astype(BF16)
    return _dot(a_hi, b_hi) + (_dot(a_hi, b_lo) + _dot(a_lo, b_hi))


def _rms(x, g):
    return x * lax.rsqrt(jnp.mean(x * x, axis=-1, keepdims=True) + EPS) * g


def _mod_spec(modg, tm, L):
    nmod = modg.shape[0]
    return pl.BlockSpec((None, 1, 6 * D), lambda i, *_: ((i * tm // L) % nmod, 0, 0))


def _ada_kernel(c_ref, w_ref, b_ref, o_ref):
    c = c_ref[...]
    s = (c * jax.nn.sigmoid(c)).astype(BF16)
    o_ref[0] = _dot(s, w_ref[0].astype(BF16)) + b_ref[0]


def _ada(cond8, w_ada, b_ada):
    depth = w_ada.shape[0]
    tn = 1536
    return pl.pallas_call(
        _ada_kernel,
        grid=(depth, 6 * D // tn),
        in_specs=[pl.BlockSpec((8, D), lambda l, j: (0, 0)),
                  pl.BlockSpec((1, D, tn), lambda l, j: (l, 0, j)),
                  pl.BlockSpec((1, 1, tn), lambda l, j: (l, 0, j))],
        out_specs=pl.BlockSpec((1, 8, tn), lambda l, j: (l, 0, j)),
        out_shape=jax.ShapeDtypeStruct((depth, 8, 6 * D), F32),
        compiler_params=_cparams("parallel", "parallel"),
        name="ada",
    )(cond8, w_ada, b_ada.reshape(depth, 1, 6 * D))


def _inproj_kernel(*refs, rope):
    if rope:
        (x_ref, mod_ref, g_ref, w_ref, cos_ref, sin_ref,
         u_ref, uv_ref, z_ref, q_ref, k_ref, v_ref) = refs
    else:
        x_ref, mod_ref, g_ref, w_ref, u_ref, uv_ref, z_ref, q_ref, k_ref, v_ref = refs
    x = x_ref[...]
    h = _rms(x, g_ref[...]) * (1.0 + mod_ref[:, D:2 * D]) + mod_ref[:, 0:D]
    p = _dot(h.astype(BF16), w_ref[...])
    u_ref[...] = p[:, 0:BR]
    uv_ref[...] = p[:, BR:3 * BR]
    z_ref[...] = p[:, 3 * BR:6 * BR]
    q = p[:, 6 * BR:7 * BR]
    k = p[:, 7 * BR:8 * BR]
    if rope:
        cs = cos_ref[...]
        sn = sin_ref[...]
        lane = lax.broadcasted_iota(jnp.int32, q.shape, 1)
        first = (lane % (2 * (ROPE_DIM // 2))) < (ROPE_DIM // 2)
        half = ROPE_DIM // 2

        def rot(t):
            return jnp.where(first, pltpu.roll(t, BR - half, 1), pltpu.roll(t, half, 1))

        q = q * cs + rot(q) * sn
        k = k * cs + rot(k) * sn
    q_ref[...] = q
    k_ref[...] = k
    v_ref[...] = p[:, 8 * BR:9 * BR]


def _inproj(x, modg, g, w_bf, l, B, L, tm, rope_tabs):
    T = B * L
    nt = L // tm
    rope = rope_tabs is not None
    in_specs = [pl.BlockSpec((tm, D), lambda i: (i, 0)),
                _mod_spec(modg, tm, L),
                pl.BlockSpec((1, D), lambda i: (0, 0)),
                pl.BlockSpec((None, D, PROJ), lambda i: (l, 0, 0))]
    args = [x, modg, g.reshape(1, D), w_bf]
    if rope:
        in_specs += [pl.BlockSpec((tm, BR), lambda i: (i % nt, 0))] * 2
        args += list(rope_tabs)
    tok = lambda n: pl.BlockSpec((tm, n), lambda i: (i, 0))
    out_specs = [pl.BlockSpec((tm, BR), lambda i: (i % nt, i // nt)),
                 tok(2 * BR), tok(3 * BR), tok(BR), tok(BR), tok(BR)]
    out_shape = [jax.ShapeDtypeStruct((L, B * BR), F32),
                 jax.ShapeDtypeStruct((T, 2 * BR), F32),
                 jax.ShapeDtypeStruct((T, 3 * BR), F32),
                 jax.ShapeDtypeStruct((T, BR), F32),
                 jax.ShapeDtypeStruct((T, BR), F32),
                 jax.ShapeDtypeStruct((T, BR), F32)]
    return pl.pallas_call(
        functools.partial(_inproj_kernel, rope=rope),
        grid=(T // tm,), in_specs=in_specs, out_specs=out_specs, out_shape=out_shape,
        compiler_params=_cparams("parallel"), name="inproj",
    )(*args)


def _s5_kernel(uf_ref, ub_ref, h0_ref, wb_ref, wc_ref, a_ref, yf_ref, yb_ref, hfin_ref,
               xs_ref, hc_ref, *, tt):
    i = pl.program_id(1)
    last = pl.num_programs(1) - 1
    half = S5_STATES // 2

    @pl.when(i == 0)
    def _():
        hc_ref[...] = h0_ref[...]

    for d in range(2):
        u_ref = uf_ref if d == 0 else ub_ref
        y_ref = yf_ref if d == 0 else yb_ref
        u2 = u_ref[...].reshape(tt * 8, BR).astype(BF16)
        xs_ref[...] = _dot(u2, wb_ref[d])
        for c in range(2):
            cr = slice(c * half, (c + 1) * half)
            ci = slice(S5_STATES + c * half, S5_STATES + (c + 1) * half)
            ar = jnp.broadcast_to(a_ref[d, 0:1, cr], (8, half))
            ai = jnp.broadcast_to(a_ref[d, 1:2, cr], (8, half))

            def body(s, carry, cr=cr, ci=ci, ar=ar, ai=ai, d=d):
                hr, hi = carry
                t = s if d == 0 else tt - 1 - s
                r0 = pl.multiple_of(t * 8, 8)
                nr = ar * hr - ai * hi + xs_ref[pl.ds(r0, 8), cr]
                ni = ar * hi + ai * hr + xs_ref[pl.ds(r0, 8), ci]
                xs_ref[pl.ds(r0, 8), cr] = nr
                xs_ref[pl.ds(r0, 8), ci] = ni
                return nr, ni

            hr, hi = lax.fori_loop(0, tt, body, (hc_ref[d, :, cr], hc_ref[d, :, ci]), unroll=4)
            hc_ref[d, :, cr] = hr
            hc_ref[d, :, ci] = hi
        parts = []
        for ob in range(2):
            hre = xs_ref[:, ob * half:(ob + 1) * half].astype(BF16)
            him = xs_ref[:, S5_STATES + ob * half:S5_STATES + (ob + 1) * half].astype(BF16)
            parts.append(_dot(hre, wc_ref[d, 0, ob]) + _dot(him, wc_ref[d, 1, ob]))
        y_ref[...] = jnp.concatenate(parts, axis=-1).reshape(tt, 8, BR)

    @pl.when(i == last)
    def _():
        hfin_ref[...] = hc_ref[...]


def _s5(u_tm, h0, wb, wc, a, L, Bp, tt):
    nT = L // tt
    ng = Bp // 8
    blk = lambda f: pl.BlockSpec((tt, 8, BR), f)
    const = lambda shp: pl.BlockSpec(shp, lambda g, i: (0,) * len(shp))
    return pl.pallas_call(
        functools.partial(_s5_kernel, tt=tt),
        grid=(ng, nT),
        in_specs=[blk(lambda g, i: (i, g, 0)), blk(lambda g, i: (nT - 1 - i, g, 0)),
                  pl.BlockSpec((2, 8, 2 * S5_STATES), lambda g, i: (0, g, 0)),
                  const((2, BR, 2 * S5_STATES)), const((2, 2, 2, S5_STATES // 2, BR // 2)),
                  const((2, 2, S5_STATES))],
        out_specs=[blk(lambda g, i: (i, g, 0)), blk(lambda g, i: (nT - 1 - i, g, 0)),
                   pl.BlockSpec((2, 8, 2 * S5_STATES), lambda g, i: (0, g, 0))],
        out_shape=[jax.ShapeDtypeStruct((L, Bp, BR), F32),
                   jax.ShapeDtypeStruct((L, Bp, BR), F32),
                   jax.ShapeDtypeStruct((2, Bp, 2 * S5_STATES), F32)],
        scratch_shapes=[pltpu.VMEM((tt * 8, 2 * S5_STATES), F32),
                        pltpu.VMEM((2, 8, 2 * S5_STATES), F32)],
        compiler_params=_cparams("parallel", "arbitrary"), name="s5",
    )(u_tm, u_tm, h0, wb, wc, a)


def _s5_prep(lam_re, lam_im, log_dt, b_re, b_im, c_re, c_im):
    dt = jnp.exp(log_dt)[..., None]
    mag = jnp.exp(lam_re * dt)
    lb_re = mag * jnp.cos(lam_im * dt)
    lb_im = mag * jnp.sin(lam_im * dt)
    den = lam_re * lam_re + lam_im * lam_im
    nr = lb_re - 1.0
    coef_re = ((nr * lam_re + lb_im * lam_im) / den)[..., None]
    coef_im = ((lb_im * lam_re - nr * lam_im) / den)[..., None]
    bp_re = coef_re * b_re - coef_im * b_im
    bp_im = coef_re * b_im + coef_im * b_re
    eye = jnp.eye(S5_G, dtype=lam_re.dtype)

    def blockdiag_in(b):
        return jnp.einsum('dgnp,gh->dgphn', b, eye).reshape(2, BR, S5_STATES)

    def blockdiag_out(c):
        return jnp.einsum('dgpn,gh->dgnhp', c, eye).reshape(2, S5_STATES, BR)

    wb = jnp.concatenate([blockdiag_in(bp_re), blockdiag_in(bp_im)], axis=-1)
    def halves(m):
        hs, hb = S5_STATES // 2, BR // 2
        return jnp.stack([m[:, ob * hs:(ob + 1) * hs, ob * hb:(ob + 1) * hb] for ob in range(2)],
                         axis=1)

    wc = jnp.stack([halves(blockdiag_out(c_re)), -halves(blockdiag_out(c_im))], axis=1)
    a = jnp.stack([lb_re.reshape(2, S5_STATES), lb_im.reshape(2, S5_STATES)], axis=1)
    return wb.astype(BF16), wc.astype(BF16), a


def _short_kernel(z_ref, zp_ref, zn_ref, w_ref, b_ref, v_ref, x1_ref, x2_ref, *, nt):
    j = pl.program_id(0) % nt
    z = z_ref[...]
    tm = z.shape[0]
    row = lax.broadcasted_iota(jnp.int32, z.shape, 0)
    prev_row = jnp.where(j > 0, zp_ref[7:8, :], 0.0)
    next_row = jnp.where(j < nt - 1, zn_ref[0:1, :], 0.0)
    zprev = jnp.where(row == 0, prev_row, pltpu.roll(z, 1, 0))
    znext = jnp.where(row == tm - 1, next_row, pltpu.roll(z, tm - 1, 0))
    y = zprev * w_ref[0:1, :] + z * w_ref[1:2, :] + znext * w_ref[2:3, :] + b_ref[...]
    v_ref[...] = y[:, 0:BR]
    x1_ref[...] = y[:, BR:2 * BR]
    x2_ref[...] = y[:, 2 * BR:3 * BR]


def _short_conv(z, w, b, L, tm):
    T = z.shape[0]
    nt = L // tm
    r8 = tm // 8
    nblk8 = T // 8
    out = jax.ShapeDtypeStruct((T, BR), F32)
    return pl.pallas_call(
        functools.partial(_short_kernel, nt=nt),
        grid=(T // tm,),
        in_specs=[pl.BlockSpec((tm, 3 * BR), lambda i: (i, 0)),
                  pl.BlockSpec((8, 3 * BR), lambda i: (jnp.maximum(i * r8 - 1, 0), 0)),
                  pl.BlockSpec((8, 3 * BR), lambda i: (jnp.minimum((i + 1) * r8, nblk8 - 1), 0)),
                  pl.BlockSpec((3, 3 * BR), lambda i: (0, 0)),
                  pl.BlockSpec((1, 3 * BR), lambda i: (0, 0))],
        out_specs=[pl.BlockSpec((tm, BR), lambda i: (i, 0))] * 3,
        out_shape=[out, out, out],
        compiler_params=_cparams("parallel"), name="short_conv",
    )(z, z, z, w, b.reshape(1, 3 * BR))


def _cs(num, den):
    th = (2.0 * math.pi / den) * (num % den).astype(F32)
    return jnp.cos(th), jnp.sin(th)


def _iota(m):
    return jnp.arange(m, dtype=jnp.int32)


def _hy_direct_tables(L):
    n = 2 * L
    c, s = _cs(_iota(n)[:, None] * _iota(L)[None, :], n)
    wf = jnp.concatenate([c, -s], axis=0)
    c, s = _cs((_iota(L)[:, None] + L // 2) * _iota(n)[None, :], n)
    wi = jnp.concatenate([c, -s], axis=1) * (1.0 / n)
    return wf.astype(BF16), wi.astype(BF16)


def _hy_spec_direct_kernel(f_ref, wf_ref, o_ref):
    o_ref[...] = _dot(wf_ref[...], f_ref[...].astype(BF16))


def _hy_spec_direct(filt, wf):
    L, C = filt.shape
    spec = pl.pallas_call(
        _hy_spec_direct_kernel,
        out_shape=jax.ShapeDtypeStruct((4 * L, C), F32), name="hy_spec_direct",
    )(filt, wf)
    return spec.reshape(2, 2 * L, 2, BR).transpose(2, 0, 1, 3)


def _hy_direct_kernel(z_ref, cw_ref, cb_ref, wf_ref, wi_ref, hf_ref, bias_ref, o_ref, *, L, nseq):
    n = 2 * L
    row = lax.broadcasted_iota(jnp.int32, (L, 3 * BR), 0)
    for s in range(nseq):
        z = z_ref[s * L:(s + 1) * L, :]
        zprev = jnp.where(row == 0, 0.0, pltpu.roll(z, 1, 0))
        znext = jnp.where(row == L - 1, 0.0, pltpu.roll(z, L - 1, 0))
        zc = zprev * cw_ref[0:1, :] + z * cw_ref[1:2, :] + znext * cw_ref[2:3, :] + cb_ref[...]
        y = zc[:, 0:BR]
        for o in range(2):
            x = _dot(wf_ref[...], y.astype(BF16))
            xr, xi = x[0:n], x[n:2 * n]
            hr, hi = hf_ref[o, 0], hf_ref[o, 1]
            yc = jnp.concatenate([xr * hr - xi * hi, xr * hi + xi * hr], axis=0).astype(BF16)
            y = zc[:, (o + 1) * BR:(o + 2) * BR] * (_dot(wi_ref[...], yc) + y * bias_ref[o:o + 1, :])
        o_ref[s * L:(s + 1) * L, :] = y


def _hy_direct(z, cw, cb, bias, tabs, hf, L):
    T = z.shape[0]
    wf, wi = tabs
    nseq = 4
    const = lambda shp: pl.BlockSpec(shp, lambda i: (0,) * len(shp))
    return pl.pallas_call(
        functools.partial(_hy_direct_kernel, L=L, nseq=nseq), grid=(T // (nseq * L),),
        in_specs=[pl.BlockSpec((nseq * L, 3 * BR), lambda i: (i, 0)),
                  const((3, 3 * BR)), const((1, 3 * BR)), const((4 * L, L)), const((L, 4 * L)),
                  const((2, 2, 2 * L, BR)), const((2, BR))],
        out_specs=pl.BlockSpec((nseq * L, BR), lambda i: (i, 0)),
        out_shape=jax.ShapeDtypeStruct((T, BR), F32),
        compiler_params=_cparams("parallel"), name="hy_direct",
    )(z, cw, cb.reshape(1, 3 * BR), wf, wi, hf, bias)


HY_N1 = 16
HY_K1 = HY_N1 // 2 + 1
HY_HALVES = 1


def _hy_split_tables(L):
    n = 2 * L
    n2 = n // HY_N1
    hk = n2 // HY_HALVES
    k1 = _iota(HY_K1)[:, None, None]
    k2 = _iota(n2)[None, :, None]
    j2 = _iota(n2)[None, None, :]
    c, s = _cs(j2 * k2 * HY_N1 + j2 * k1, n)
    top = jnp.concatenate([c, s], axis=2).reshape(HY_K1, HY_HALVES, hk, 2 * n2)
    bot = jnp.concatenate([-s, c], axis=2).reshape(HY_K1, HY_HALVES, hk, 2 * n2)
    wf = jnp.concatenate([top, bot], axis=2)
    ct = jnp.swapaxes(c, 1, 2).reshape(HY_K1, n2, HY_HALVES, hk).transpose(0, 2, 1, 3)
    st = jnp.swapaxes(s, 1, 2).reshape(HY_K1, n2, HY_HALVES, hk).transpose(0, 2, 1, 3)
    wi = jnp.concatenate([jnp.concatenate([ct, -st], axis=3),
                          jnp.concatenate([st, ct], axis=3)], axis=2)
    kk = _iota(HY_K1)[:, None]
    c1, s1 = _cs(kk * _iota(HY_N1 // 2)[None, :], HY_N1)
    wgt = jnp.where((kk == 0) | (kk == HY_N1 // 2), 1.0, 2.0) / n
    co, so = _cs(kk * (_iota(HY_N1 // 2)[None, :] + HY_N1 // 4), HY_N1)
    coef = jnp.concatenate([c1, -s1, wgt * co, -wgt * so], axis=1)
    return wf.astype(BF16), wi.astype(BF16), coef.astype(F32)


def _hy_split_stage1(coef_ref, v_ref, a_ref, k1, n2):
    nin = HY_N1 // 2
    ar = ai = None
    for j in range(nin):
        xj = v_ref[0, j * n2:(j + 1) * n2, :]
        tr, ti = coef_ref[k1, j] * xj, coef_ref[k1, nin + j] * xj
        ar, ai = (tr, ti) if ar is None else (ar + tr, ai + ti)
    a_ref[0:n2, :] = ar.astype(BF16)
    a_ref[n2:2 * n2, :] = ai.astype(BF16)


def _hy_spec_split_kernel(coef_ref, f_ref, wf_ref, o_ref, a_ref, *, L):
    n2 = 2 * L // HY_N1

    @pl.when(pl.program_id(2) == 0)
    def _():
        _hy_split_stage1(coef_ref, f_ref, a_ref, pl.program_id(1), n2)

    o_ref[0, 0, 0] = _dot(wf_ref[0, 0], a_ref[...])


def _hy_split_kernel(coef_ref, v_ref, xg_ref, bias_ref, wf_ref, wi_ref, hf_ref, o_ref, a_ref, *, L):
    k1, hh = pl.program_id(1), pl.program_id(2)
    n2 = 2 * L // HY_N1
    hk = n2 // HY_HALVES
    nin = HY_N1 // 2

    @pl.when(hh == 0)
    def _():
        _hy_split_stage1(coef_ref, v_ref, a_ref, k1, n2)

    @pl.when((k1 == 0) & (hh == 0))
    def _():
        o_ref[...] = jnp.zeros_like(o_ref)

    x = _dot(wf_ref[0, 0], a_ref[...])
    xr, xi = x[0:hk], x[hk:2 * hk]
    hr, hi = hf_ref[0, 0, 0:hk, :], hf_ref[0, 0, hk:2 * hk, :]
    yc = jnp.concatenate([xr * hr - xi * hi, xr * hi + xi * hr], axis=0).astype(BF16)
    b = _dot(wi_ref[0, 0], yc)
    for o in range(nin):
        o_ref[0, o * n2:(o + 1) * n2, :] += (coef_ref[k1, 2 * nin + o] * b[0:n2]
                                             + coef_ref[k1, 3 * nin + o] * b[n2:2 * n2])

    @pl.when((k1 == HY_K1 - 1) & (hh == HY_HALVES - 1))
    def _():
        o_ref[0] = xg_ref[0] * (o_ref[0] + v_ref[0] * bias_ref[...])


def _hy_spec_split(filt2, tabs, L):
    wf, _, coef = tabs
    n2 = 2 * L // HY_N1
    hk = n2 // HY_HALVES
    return pl.pallas_call(
        functools.partial(_hy_spec_split_kernel, L=L), grid=(2, HY_K1, HY_HALVES),
        in_specs=[pl.BlockSpec(memory_space=pltpu.SMEM),
                  pl.BlockSpec((1, L, BR), lambda o, k, h: (o, 0, 0)),
                  pl.BlockSpec((1, 1, 2 * hk, 2 * n2), lambda o, k, h: (k, h, 0, 0))],
        out_specs=pl.BlockSpec((1, 1, 1, 2 * hk, BR), lambda o, k, h: (o, k, h, 0, 0)),
        out_shape=jax.ShapeDtypeStruct((2, HY_K1, HY_HALVES, 2 * hk, BR), F32),
        scratch_shapes=[pltpu.VMEM((2 * n2, BR), BF16)],
        compiler_params=_cparams("parallel", "arbitrary", "arbitrary"), name="hy_spec_split",
    )(coef, filt2, wf)


def _hy_split(v, xg, bias, hf, tabs, L):
    wf, wi, coef = tabs
    B = v.shape[0]
    n2 = 2 * L // HY_N1
    hk = n2 // HY_HALVES
    seq = pl.BlockSpec((1, L, BR), lambda b, k, h: (b, 0, 0))
    return pl.pallas_call(
        functools.partial(_hy_split_kernel, L=L), grid=(B, HY_K1, HY_HALVES),
        in_specs=[pl.BlockSpec(memory_space=pltpu.SMEM), seq, seq,
                  pl.BlockSpec((1, BR), lambda b, k, h: (0, 0)),
                  pl.BlockSpec((1, 1, 2 * hk, 2 * n2), lambda b, k, h: (k, h, 0, 0)),
                  pl.BlockSpec((1, 1, 2 * n2, 2 * hk), lambda b, k, h: (k, h, 0, 0)),
                  pl.BlockSpec((1, 1, 2 * hk, BR), lambda b, k, h: (k, h, 0, 0))],
        out_specs=seq,
        out_shape=jax.ShapeDtypeStruct((B, L, BR), F32),
        scratch_shapes=[pltpu.VMEM((2 * n2, BR), BF16)],
        compiler_params=_cparams("parallel", "arbitrary", "arbitrary"), name="hy_split",
    )(coef, v, xg, bias.reshape(1, BR), wf, wi, hf)


def _hy_filters(L, w1, b1, w2, b2, w3):
    dt = w1.dtype
    pos = jnp.arange(L, dtype=dt)
    t01 = jnp.linspace(0.0, 1.0, L, dtype=dt)[:, None]
    w = (2.0 * math.pi / L) * pos[:, None]
    bands = jnp.linspace(1e-4, HY_BANDS - 1, HY_BANDS, dtype=dt)[None, :]
    feats = jnp.concatenate([t01, jnp.cos(bands * w), -jnp.sin(bands * w)], axis=-1)
    h = jnp.sin(feats @ w1 + b1)
    h = jnp.sin(h @ w2 + b2)
    h = h @ w3
    dist = jnp.abs(pos - (L // 2)) / L
    decay = jnp.abs(jnp.linspace(math.log(HY_DECAY_TARGET) / HY_SLOW,
                                 math.log(HY_DECAY_TARGET) / HY_FAST, 2 * BR, dtype=dt))
    return h * jnp.exp(-dist[:, None] * decay[None, :])


V_EXT = 128


def _attn_kernel(lam_ref, q_ref, kt_ref, v_ref, sub_ref, o_ref, *, qscale, post):
    q = (q_ref[0] * qscale).astype(BF16)
    lam = lam_ref[0, 0]
    sub = sub_ref[...]
    for h in range(HEADS):
        vh = v_ref[0, :, h * V_EXT:(h + 1) * V_EXT]
        res = []
        for m in range(2):
            c0 = h * 2 * HEAD_DIM + m * HEAD_DIM
            s = _dot(q[:, c0:c0 + HEAD_DIM], kt_ref[0, c0:c0 + HEAD_DIM, :])
            p = jnp.exp2(s - jnp.max(s, axis=-1, keepdims=True)).astype(BF16)
            pv = _dot(p, vh)
            res.append(pv[:, 0:VAL_DIM] / pv[:, VAL_DIM:VAL_DIM + 1])
        o = res[0] - lam * res[1]
        o = o * lax.rsqrt(jnp.mean(o * o, axis=-1, keepdims=True) + EPS) * sub * post
        o_ref[0, :, h * VAL_DIM:(h + 1) * VAL_DIM] = o


def _attn(lam, q, kt, v, subln, tq, post):
    B, L, _ = q.shape
    Lk = kt.shape[2]
    return pl.pallas_call(
        functools.partial(_attn_kernel, qscale=HEAD_DIM ** -0.5 * LOG2E, post=post),
        grid=(B, L // tq),
        in_specs=[pl.BlockSpec(memory_space=pltpu.SMEM),
                  pl.BlockSpec((1, tq, BR), lambda b, i: (b, i, 0)),
                  pl.BlockSpec((1, BR, Lk), lambda b, i: (b, 0, 0)),
                  pl.BlockSpec((1, Lk, HEADS * V_EXT), lambda b, i: (b, 0, 0)),
                  pl.BlockSpec((1, VAL_DIM), lambda b, i: (0, 0))],
        out_specs=pl.BlockSpec((1, tq, BR), lambda b, i: (b, i, 0)),
        out_shape=jax.ShapeDtypeStruct((B, L, BR), F32),
        compiler_params=_cparams("parallel", "parallel"), name="diff_attn",
    )(lam.reshape(1, 1), q, kt, v, subln.reshape(1, VAL_DIM))


def _attn_inputs(k3, v3, ctx):
    B = k3.shape[0]
    if ctx is not None:
        k3 = jnp.concatenate([ctx[0].reshape(B, -1, BR), k3], axis=1)
        v3 = jnp.concatenate([ctx[1].reshape(B, -1, BR), v3], axis=1)
    Lk = k3.shape[1]
    v4 = v3.reshape(B, Lk, HEADS, VAL_DIM)
    ext = jnp.concatenate([v4, jnp.ones((B, Lk, HEADS, 1), v3.dtype),
                           jnp.zeros((B, Lk, HEADS, V_EXT - VAL_DIM - 1), v3.dtype)], axis=-1)
    return jnp.swapaxes(k3, 1, 2).astype(BF16), ext.reshape(B, Lk, HEADS * V_EXT).astype(BF16)


def _rope_tables(L, dt):
    n_rows = L // GRID_WIDTH
    row = jnp.repeat(jnp.arange(n_rows, dtype=dt), GRID_WIDTH)
    col = (jnp.arange(L) % GRID_WIDTH).astype(dt)
    inv = ROPE_BASE ** (-jnp.arange(0, ROPE_DIM, 2, dtype=dt) / ROPE_DIM)
    ar = row[:, None] * inv[None, :]
    ac = col[:, None] * inv[None, :]
    ang = jnp.concatenate([ar, ar, ac, ac], axis=-1)
    sign = jnp.tile(jnp.concatenate([-jnp.ones(ROPE_DIM // 2, dt), jnp.ones(ROPE_DIM // 2, dt)]), 2)
    reps = BR // HEAD_DIM
    return jnp.tile(jnp.cos(ang), (1, reps)), jnp.tile(jnp.sin(ang) * sign, (1, reps))


def _mixout_kernel(x_ref, mod_ref, u_ref, yf_ref, yb_ref, uv_ref, hy_ref, da_ref,
                   sd_ref, wg_ref, bg_ref, sn_ref, ws_ref, bs_ref, bn_ref, wo_ref, gf_ref, rt_ref,
                   xo_ref, h2_ref, cb_ref, *, route):
    tm = x_ref.shape[0]
    u = u_ref[...]
    y = jax.nn.gelu(sd_ref[...] * u + yf_ref[...] + yb_ref[...])
    y = y * jax.nn.sigmoid(_dot(y.astype(BF16), wg_ref[...]) + bg_ref[...])
    acc = _dot(_rms(y, bn_ref[0:1, :]).astype(BF16), wo_ref[0:BR, :])
    guv = jax.nn.gelu(uv_ref[...])
    gu = guv[:, 0:BR]
    gv = _rms(guv[:, BR:2 * BR], sn_ref[...]).astype(BF16)
    lane = lax.broadcasted_iota(jnp.int32, (SGU_CHUNK, BR), 1)
    hd = BR // SGU_HEADS
    zs = []
    for c in range(tm // SGU_CHUNK):
        vc = gv[c * SGU_CHUNK:(c + 1) * SGU_CHUNK, :]
        z = bs_ref[...]
        for h in range(SGU_HEADS):
            z = z + jnp.where(lane // hd == h, _dot(ws_ref[h], vc), 0.0)
        zs.append(z)
    z = zs[0] if len(zs) == 1 else jnp.concatenate(zs, axis=0)
    acc = acc + _dot(_rms(gu * z, bn_ref[1:2, :]).astype(BF16), wo_ref[BR:2 * BR, :])
    acc = acc + _dot(_rms(hy_ref[...], bn_ref[2:3, :]).astype(BF16), wo_ref[2 * BR:3 * BR, :])
    acc = acc + _dot(da_ref[...].astype(BF16), wo_ref[3 * BR:4 * BR, :])
    xn = x_ref[...] + mod_ref[:, 2 * D:3 * D] * acc
    xo_ref[...] = xn
    h2 = _rms(xn, gf_ref[...]) * (1.0 + mod_ref[:, 4 * D:5 * D]) + mod_ref[:, 3 * D:4 * D]
    h2_ref[...] = h2.astype(BF16)
    if not route:
        cb_ref[...] = jnp.zeros_like(cb_ref)
        return
    logits = _dot3(h2, rt_ref[...])
    el = lax.broadcasted_iota(jnp.int32, logits.shape, 1)
    logits = jnp.where(el < N_EXPERTS, logits, -jnp.inf)
    e = jnp.exp(logits - jnp.max(logits, axis=-1, keepdims=True))
    probs = e / jnp.sum(e, axis=-1, keepdims=True)
    big = logits.shape[1]
    m1 = jnp.max(probs, axis=-1, keepdims=True)
    i1 = jnp.min(jnp.where(probs == m1, el, big), axis=-1, keepdims=True)
    p2 = jnp.where((el == i1) | (el >= N_EXPERTS), -1.0, probs)
    m2 = jnp.max(p2, axis=-1, keepdims=True)
    i2 = jnp.min(jnp.where(p2 == m2, el, big), axis=-1, keepdims=True)
    tot = m1 + m2
    cb_ref[...] = jnp.where(el == i1, m1 / tot, 0.0) + jnp.where(el == i2, m2 / tot, 0.0)


def _mixout(x, modg, u_tm2, yf2, yb2, uv, hy, da, lp, B, L, tm, route):
    T = B * L
    nt = L // tm
    tok = lambda n: pl.BlockSpec((tm, n), lambda i: (i, 0))
    tmaj = pl.BlockSpec((tm, BR), lambda i: (i % nt, i // nt))
    const = lambda shp: pl.BlockSpec(shp, lambda i: (0,) * len(shp))
    return pl.pallas_call(
        functools.partial(_mixout_kernel, route=route), grid=(T // tm,),
        in_specs=[tok(D), _mod_spec(modg, tm, L),
                  tmaj, tmaj, tmaj, tok(2 * BR), tok(BR), tok(BR),
                  const((1, BR)), const((BR, BR)), const((1, BR)), const((1, BR)),
                  const((SGU_HEADS, SGU_CHUNK, SGU_CHUNK)), const((SGU_CHUNK, BR)),
                  const((3, BR)), pl.BlockSpec((None, D, D), lambda i: (lp['l'], 0, 0)),
                  const((1, D)), const((D, 128))],
        out_specs=[tok(D), tok(D), tok(128)],
        out_shape=[jax.ShapeDtypeStruct((T, D), F32), jax.ShapeDtypeStruct((T, D), BF16),
                   jax.ShapeDtypeStruct((T, 128), F32)],
        compiler_params=_cparams("parallel"), name="mix_out",
    )(x, modg, u_tm2, yf2, yb2, uv, hy, da,
      lp['s5_d'], lp['s5_w_glu'], lp['s5_b_glu'], lp['sgu_norm'], lp['sgu_w_s'], lp['sgu_b'],
      lp['branch_norm'], lp['w_out'], lp['norm_ffn'], lp['router'])


def _ffn_kernel(h_ref, x_ref, mod_ref, w1_ref, w3_ref, w2_ref, o_ref, acc_ref):
    j = pl.program_id(1)
    h = h_ref[...]
    a = _dot(h, w1_ref[...])
    t = (a * jax.nn.sigmoid(a) * _dot(h, w3_ref[...])).astype(BF16)
    part = _dot(t, w2_ref[...])

    @pl.when(j == 0)
    def _():
        acc_ref[...] = part

    @pl.when(j > 0)
    def _():
        acc_ref[...] += part

    @pl.when(j == pl.num_programs(1) - 1)
    def _():
        o_ref[...] = x_ref[...] + mod_ref[:, 5 * D:6 * D] * acc_ref[...]


def _ffn(h2, x, modg, w1, w3, w2, jl, L, tm, tf):
    T = x.shape[0]
    dff = w1.shape[2]
    return pl.pallas_call(
        _ffn_kernel, grid=(T // tm, dff // tf),
        in_specs=[pl.BlockSpec((tm, D), lambda i, j: (i, 0)),
                  pl.BlockSpec((tm, D), lambda i, j: (i, 0)),
                  _mod_spec(modg, tm, L),
                  pl.BlockSpec((None, D, tf), lambda i, j: (jl, 0, j)),
                  pl.BlockSpec((None, D, tf), lambda i, j: (jl, 0, j)),
                  pl.BlockSpec((None, tf, D), lambda i, j: (jl, j, 0))],
        out_specs=pl.BlockSpec((tm, D), lambda i, j: (i, 0)),
        out_shape=jax.ShapeDtypeStruct((T, D), F32),
        scratch_shapes=[pltpu.VMEM((tm, D), F32)],
        compiler_params=_cparams("parallel", "arbitrary"), name="ffn",
    )(h2, x, modg, w1, w3, w2)


def _route_kernel(cb_ref, rk_ref, rkt_ref, cnt_ref):
    tm = cb_ref.shape[0]
    mask = cb_ref[...] > 0.0
    mf = jnp.where(mask, 1.0, 0.0)
    r = lax.broadcasted_iota(jnp.int32, (tm, tm), 0)
    c = lax.broadcasted_iota(jnp.int32, (tm, tm), 1)
    before = jnp.where(c < r, 1.0, 0.0).astype(BF16)
    rank = jnp.where(mask, _dot(before, mf.astype(BF16)), -1.0)
    rk_ref[...] = rank
    rkt_ref[...] = rank.T[0:N_EXPERTS, :]
    cnt_ref[...] = jnp.sum(mf, axis=0, keepdims=True)


def _route(cb, tm):
    T = cb.shape[0]
    nt = T // tm
    rk, rkt, cnt = pl.pallas_call(
        _route_kernel, grid=(nt,),
        in_specs=[pl.BlockSpec((tm, 128), lambda i: (i, 0))],
        out_specs=[pl.BlockSpec((tm, 128), lambda i: (i, 0)),
                   pl.BlockSpec((None, N_EXPERTS, tm), lambda i: (i, 0, 0)),
                   pl.BlockSpec((None, 1, 128), lambda i: (i, 0, 0))],
        out_shape=[jax.ShapeDtypeStruct((T, 128), F32),
                   jax.ShapeDtypeStruct((nt, N_EXPERTS, tm), F32),
                   jax.ShapeDtypeStruct((nt, 1, 128), F32)],
        compiler_params=_cparams("parallel"), name="moe_route",
    )(cb)
    return rk, rkt, cnt[:, 0, :N_EXPERTS].astype(jnp.int32).reshape(-1)


MOE_TS = 1024
MOE_CHUNK = 128


def _moe_kernel(cnt_ref, h_ref, x_ref, mod_ref, cb_ref, rk_ref, rkt_ref, w1_ref, w3_ref, w2_ref,
                o_ref, xg_ref, y_ref, *, nsub):
    i, e, j = pl.program_id(0), pl.program_id(1), pl.program_id(2)
    ne, nj = pl.num_programs(1), pl.num_programs(2)
    ts, ch = MOE_TS, MOE_CHUNK

    def rows(c):
        return pl.ds(pl.multiple_of(c * ch, ch), ch)

    @pl.when((e == 0) & (j == 0))
    def _():
        o_ref[...] = jnp.zeros_like(o_ref)
        y_ref[...] = jnp.zeros_like(y_ref)

    for s in range(nsub):
        tok = slice(s * ts, (s + 1) * ts)
        nch = (cnt_ref[(i * nsub + s) * ne + e] + (ch - 1)) // ch

        @pl.when(j == 0)
        def _(s=s, tok=tok, nch=nch):
            rid = lax.broadcasted_iota(jnp.int32, (ch, ts), 0).astype(F32)
            rrow = rkt_ref[s, pl.ds(e, 1), :]

            def gather(c, _):
                sel = jnp.where(rid == rrow - (c * ch).astype(F32), 1.0, 0.0).astype(BF16)
                xg_ref[s, rows(c), :] = _dot(sel, h_ref[tok, :]).astype(BF16)
                return 0

            lax.fori_loop(0, nch, gather, 0)

        def expert(c, _, s=s):
            xg = xg_ref[s, rows(c), :]
            a = _dot(xg, w1_ref[0])
            t = (a * jax.nn.sigmoid(a) * _dot(xg, w3_ref[0])).astype(BF16)
            part = _dot(t, w2_ref[0])

            @pl.when(j == 0)
            def _():
                y_ref[s, rows(c), :] = part

            @pl.when(j > 0)
            def _():
                y_ref[s, rows(c), :] += part

            return 0

        lax.fori_loop(0, nch, expert, 0)

        @pl.when(j == nj - 1)
        def _(s=s, tok=tok, nch=nch):
            el = lax.broadcasted_iota(jnp.int32, (ts, 128), 1)
            rcol = jnp.sum(jnp.where(el == e, rk_ref[tok, :], 0.0), axis=-1, keepdims=True)
            wcol = jnp.sum(jnp.where(el == e, cb_ref[tok, :], 0.0), axis=-1, keepdims=True)
            cid = lax.broadcasted_iota(jnp.int32, (ts, 2 * ch), 1).astype(F32)

            def scatter(c, _):
                r0 = pl.multiple_of(c * (2 * ch), 2 * ch)
                selt = jnp.where(cid == rcol - r0.astype(F32), 1.0, 0.0).astype(BF16)
                o_ref[tok, :] += wcol * _dot(selt, y_ref[s, pl.ds(r0, 2 * ch), :].astype(BF16))
                return 0

            lax.fori_loop(0, (nch + 1) // 2, scatter, 0)

    @pl.when((e == ne - 1) & (j == nj - 1))
    def _():
        o_ref[...] = x_ref[...] + mod_ref[:, 5 * D:6 * D] * o_ref[...]


def _moe(h2, x, modg, cb, w1, w3, w2, jl, L, tm, tf):
    T = x.shape[0]
    _, ne, _, dff = w1.shape
    ts = MOE_TS
    nsub = tm // ts
    rk, rkt, cnt = _route(cb, ts)
    nmod = modg.shape[0]
    rows = -(-ts // (2 * MOE_CHUNK)) * 2 * MOE_CHUNK
    grid_spec = pltpu.PrefetchScalarGridSpec(
        num_scalar_prefetch=1, grid=(T // tm, ne, dff // tf),
        in_specs=[pl.BlockSpec((tm, D), lambda i, e, j, c: (i, 0)),
                  pl.BlockSpec((tm, D), lambda i, e, j, c: (i, 0)),
                  pl.BlockSpec((None, 1, 6 * D), lambda i, e, j, c: ((i * tm // L) % nmod, 0, 0)),
                  pl.BlockSpec((tm, 128), lambda i, e, j, c: (i, 0)),
                  pl.BlockSpec((tm, 128), lambda i, e, j, c: (i, 0)),
                  pl.BlockSpec((nsub, N_EXPERTS, ts), lambda i, e, j, c: (i, 0, 0)),
                  pl.BlockSpec((None, 1, D, tf), lambda i, e, j, c: (jl, e, 0, j)),
                  pl.BlockSpec((None, 1, D, tf), lambda i, e, j, c: (jl, e, 0, j)),
                  pl.BlockSpec((None, 1, tf, D), lambda i, e, j, c: (jl, e, j, 0))],
        out_specs=pl.BlockSpec((tm, D), lambda i, e, j, c: (i, 0)),
        scratch_shapes=[pltpu.VMEM((nsub, rows, D), BF16), pltpu.VMEM((nsub, rows, D), F32)])
    return pl.pallas_call(
        functools.partial(_moe_kernel, nsub=nsub), grid_spec=grid_spec,
        out_shape=jax.ShapeDtypeStruct((T, D), F32),
        compiler_params=_cparams("parallel", "arbitrary", "arbitrary"), name="moe",
    )(cnt, h2, x, modg, cb, rk, rkt, w1, w3, w2)


def _final_kernel(x_ref, g_ref, o_ref):
    o_ref[...] = _rms(x_ref[...], g_ref[...])


def _final_norm(x, g, tm):
    T = x.shape[0]
    return pl.pallas_call(
        _final_kernel, grid=(T // tm,),
        in_specs=[pl.BlockSpec((tm, D), lambda i: (i, 0)), pl.BlockSpec((1, D), lambda i: (0, 0))],
        out_specs=pl.BlockSpec((tm, D), lambda i: (i, 0)),
        out_shape=jax.ShapeDtypeStruct((T, D), F32),
        compiler_params=_cparams("parallel"), name="final_norm",
    )(x, g.reshape(1, D))


def _layer(x, modg, lp, l, B, L, ctx):
    is_ctx = ctx is None
    T = B * L
    Bp = -(-B // 8) * 8
    tm = min(L, 512)
    rope_tabs = None if is_ctx else _rope_tables(L, F32)
    u_tm, uv, z, q, k, v = _inproj(x, modg, lp['norm_mix'], lp['w_in'], l, B, L, tm, rope_tabs)

    u3 = jnp.pad(u_tm.reshape(L, B, BR), ((0, 0), (0, Bp - B), (0, 0)))
    if is_ctx:
        h0 = jnp.zeros((2, Bp, 2 * S5_STATES), F32)
    else:
        h0 = ctx[2]
        h0 = jnp.transpose(h0, (1, 0, 4, 2, 3)).reshape(2, B, 2 * S5_STATES)
        h0 = jnp.pad(h0, ((0, 0), (0, Bp - B), (0, 0)))
    yf, yb, hfin = _s5(u3, h0, lp['s5_wb'], lp['s5_wc'], lp['s5_a'], L, Bp, min(L, 256))

    tabs, hf = lp['hy'][L]
    if L <= HY_DIRECT_MAX:
        y_hy = _hy_direct(z, lp['hy_conv_w'], lp['hy_conv_b'], lp['hy_bias'], tabs, hf, L)
    else:
        hv, hx1, hx2 = _short_conv(z, lp['hy_conv_w'], lp['hy_conv_b'], L, tm)
        y1 = _hy_split(hv.reshape(B, L, BR), hx1.reshape(B, L, BR), lp['hy_bias'][0], hf[0], tabs, L)
        y_hy = _hy_split(y1, hx2.reshape(B, L, BR), lp['hy_bias'][1], hf[1], tabs, L).reshape(T, BR)

    q3 = q.reshape(B, L, BR)
    k3 = k.reshape(B, L, BR)
    v3 = v.reshape(B, L, BR)
    kt, vb = _attn_inputs(k3, v3, None if is_ctx else ctx)
    da = _attn(lp['da_lam'], q3, kt, vb, lp['da_subln'], min(L, 256), 1.0 - lp['lam_init'])

    x, h2, cb = _mixout(x, modg, u_tm, yf.reshape(L, Bp * BR), yb.reshape(L, Bp * BR), uv,
                        y_hy, da.reshape(T, BR), lp, B, L, tm, l % 2 == 1)
    tmf = min(T, 1024)
    if l % 2 == 0:
        x = _ffn(h2, x, modg, lp['ffn_w1'], lp['ffn_w3'], lp['ffn_w2'], l // 2, L, tmf,
                 lp['ffn_w1'].shape[2] // 2)
    else:
        tmm = min(T, MOE_TS)
        if modg.shape[0] > 1:
            tmm = min(tmm, L)
        x = _moe(h2, x, modg, cb, lp['moe_w1'], lp['moe_w3'], lp['moe_w2'], l // 2, L, tmm, 1792)
    if is_ctx:
        fin = hfin[:, :B].reshape(2, B, 2, S5_G, S5_N)
        fin = jnp.transpose(fin, (1, 0, 3, 4, 2))
        return x, k3.reshape(B, L, HEADS, 2 * HEAD_DIM), v3.reshape(B, L, HEADS, VAL_DIM), fin
    return x


def kernel(x_prompt, x_sample, cache_k, cache_v, state_ssm, c, c_ctx, w_ada, b_ada, norm_mix, norm_ffn, w_in, w_out, branch_norm, s5_lam_re, s5_lam_im, s5_log_dt, s5_b_re, s5_b_im, s5_c_re, s5_c_im, s5_d, s5_w_glu, s5_b_glu, sgu_norm, sgu_w_s, sgu_b_s, hy_conv_w, hy_conv_b, hy_w1, hy_b1, hy_w2, hy_b2, hy_w3, hy_bias, da_lq1, da_lk1, da_lq2, da_lk2, da_subln, ffn_w1, ffn_w3, ffn_w2, moe_router, moe_w1, moe_w3, moe_w2, norm_final):
    depth = w_in.shape[0]
    Bc, Lc, _ = x_prompt.shape
    Bs, Ls, _ = x_sample.shape

    cond = jnp.concatenate([c_ctx[None, :], c], axis=0)
    cond8 = jnp.pad(cond, ((0, 8 - cond.shape[0]), (0, 0)))
    mod = _ada(cond8, w_ada, b_ada)

    def hyena_tables(L):
        return _hy_direct_tables(L) if L <= HY_DIRECT_MAX else _hy_split_tables(L)

    def hyena_spectrum(L, tabs, *hy_args):
        filt = _hy_filters(L, *hy_args)
        if L <= HY_DIRECT_MAX:
            return _hy_spec_direct(filt, tabs[0])
        return _hy_spec_split(filt.reshape(L, 2, BR).transpose(1, 0, 2), tabs, L)

    hy_tabs = {L: hyena_tables(L) for L in {Lc, Ls}}

    bf = {name: w.astype(BF16) for name, w in dict(
        w_in=w_in, w_out=w_out, ffn_w1=ffn_w1, ffn_w3=ffn_w3, ffn_w2=ffn_w2,
        moe_w1=moe_w1, moe_w3=moe_w3, moe_w2=moe_w2).items()}

    layers = []
    for l in range(depth):
        j = l // 2
        wb, wc, a = _s5_prep(s5_lam_re[l], s5_lam_im[l], s5_log_dt[l], s5_b_re[l], s5_b_im[l],
                             s5_c_re[l], s5_c_im[l])
        lam_init = 0.8 - 0.6 * math.exp(-0.3 * l)
        lam = (jnp.exp(jnp.sum(da_lq1[l] * da_lk1[l])) - jnp.exp(jnp.sum(da_lq2[l] * da_lk2[l]))
               + lam_init)
        hy_args = (hy_w1[l], hy_b1[l], hy_w2[l], hy_b2[l], hy_w3[l])
        lp = dict(
            l=l, norm_mix=norm_mix[l], norm_ffn=norm_ffn[l].reshape(1, D), branch_norm=branch_norm[l],
            s5_wb=wb, s5_wc=wc, s5_a=a, s5_d=s5_d[l].reshape(1, BR),
            s5_w_glu=s5_w_glu[l].astype(BF16), s5_b_glu=s5_b_glu[l].reshape(1, BR),
            sgu_norm=sgu_norm[l].reshape(1, BR), sgu_w_s=sgu_w_s[l].astype(BF16),
            sgu_b=jnp.repeat(sgu_b_s[l].T, BR // SGU_HEADS, axis=1),
            hy_conv_w=hy_conv_w[l], hy_conv_b=hy_conv_b[l], hy_bias=hy_bias[l],
            hy={L: (t, hyena_spectrum(L, t, *hy_args)) for L, t in hy_tabs.items()},
            da_lam=lam, lam_init=lam_init, da_subln=da_subln[l],
        )
        lp.update(bf)
        if l % 2 == 0:
            lp.update(router=jnp.zeros((D, 128), F32))
        else:
            lp.update(router=jnp.pad(moe_router[j], ((0, 0), (0, 128 - N_EXPERTS))))
        layers.append(lp)

    xc = x_prompt.reshape(Bc * Lc, D)
    ks, vs, ss = [], [], []
    for l in range(depth):
        xc, k_l, v_l, s_l = _layer(xc, mod[l, 0:1].reshape(1, 1, 6 * D), layers[l], l, Bc, Lc, None)
        ks.append(k_l)
        vs.append(v_l)
        ss.append(s_l)
    y_prompt = _final_norm(xc, norm_final, 512).reshape(Bc, Lc, D)

    xs = x_sample.reshape(Bs * Ls, D)
    for l in range(depth):
        xs = _layer(xs, mod[l, 1:1 + Bs].reshape(Bs, 1, 6 * D), layers[l], l, Bs, Ls,
                    (cache_k[:, l], cache_v[:, l], state_ssm[:, l]))
    y_sample = _final_norm(xs, norm_final, 512).reshape(Bs, Ls, D)
    return (y_prompt, y_sample, jnp.stack(ks, axis=1), jnp.stack(vs, axis=1), jnp.stack(ss, axis=1))
```
